```python
import math
import jax
import jax.numpy as jnp
from jax import lax
import numpy as np

D_MODEL = 2048
BATCH = 4
SEQ = 8192
DEPTH = 4
DEC_BATCH = 16
DEC_SEQ = 64
PAST_LEN = 2048

CHUNK = 64
N_META = 16
N_MIXERS = 3
ROPE_THETA = 500000.0
NORM_EPS = 1e-6
Q_BLOCK = 128
NEG_INF = -1e30

DA_HEADS = 8
DA_HD = 128
DA_QK = DA_HEADS * 2 * DA_HD
DA_WIDTH = DA_HEADS * 2 * DA_HD
SW_HEADS = 32
SW_KV = 4
SW_HD = 64
SW_WIDTH = SW_HEADS * SW_HD
WINDOW = 128
DS_HEADS = 16
DS_KV = 4
DS_HD = 128
DS_WIDTH = DS_HEADS * DS_HD
IDX_HEADS = 16
IDX_HD = 64
TOPK_MAX = 256

kernel_name = 'hybrid_streaming_encoder_step'


def rms_norm(x, g):
    xf = x.astype(jnp.float32)
    y = xf * lax.rsqrt(jnp.mean(xf * xf, axis=-1, keepdims=True) + NORM_EPS)
    return (y * g.astype(jnp.float32)).astype(x.dtype)


def partial_rope(x, pos):
    d = x.shape[-1]
    rd = d // 4
    half = rd // 2
    inv_freq = ROPE_THETA ** (-(jnp.arange(half, dtype=jnp.float32) * 2.0 / rd))
    ang = pos.astype(jnp.float32)[:, None] * inv_freq[None, :]
    cos = jnp.cos(ang)[None, :, None, :]
    sin = jnp.sin(ang)[None, :, None, :]
    xf = x.astype(jnp.float32)
    x1 = xf[..., :half]
    x2 = xf[..., half:rd]
    out = jnp.concatenate([x1 * cos - x2 * sin, x2 * cos + x1 * sin, xf[..., rd:]], axis=-1)
    return out.astype(x.dtype)


def split_cols(p, sizes):
    bounds = [int(b) for b in np.cumsum(sizes)[:-1]]
    return jnp.split(p, bounds, axis=-1)


def prompt_chunk_ids(n):
    return (jnp.arange(n) - N_META) // CHUNK


def to_blocks(a, tq):
    a = jnp.pad(a, [(0, 0), (0, tq - a.shape[1])] + [(0, 0)] * (a.ndim - 2))
    a = a.reshape(a.shape[0], tq // Q_BLOCK, Q_BLOCK, *a.shape[2:])
    return jnp.moveaxis(a, 1, 0)


def from_blocks(o, t):
    o = jnp.moveaxis(o, 0, 1)
    return o.reshape(o.shape[0], o.shape[1] * o.shape[2], *o.shape[3:])[:, :t]


def diff_combine(q, k, v, lam, mask):
    s = jnp.einsum('bqhjd,bkhjd->bhjqk', q, k).astype(jnp.float32) * (DA_HD ** -0.5)
    if mask is not None:
        s = jnp.where(mask, s, NEG_INF)
    p = jax.nn.softmax(s, axis=-1)
    w = p[:, :, 0] - lam * p[:, :, 1]
    return jnp.einsum('bhqk,bkhe->bqhe', w.astype(v.dtype), v)


def diff_mixer(h, pos, w_in, w_out, lam_q1, lam_k1, lam_q2, lam_k2, subln, lam_init, cache_k=None, cache_v=None):
    b, t, _ = h.shape
    q, k, v, z = split_cols(h @ w_in, (DA_QK, DA_QK, DA_WIDTH, DA_WIDTH))
    q = partial_rope(q.reshape(b, t, 2 * DA_HEADS, DA_HD), pos).reshape(b, t, DA_HEADS, 2, DA_HD)
    k = partial_rope(k.reshape(b, t, 2 * DA_HEADS, DA_HD), pos).reshape(b, t, DA_HEADS, 2, DA_HD)
    v = v.reshape(b, t, DA_HEADS, 2 * DA_HD)
    f32 = jnp.float32
    lam = (jnp.exp(jnp.sum(lam_q1.astype(f32) * lam_k1.astype(f32)))
           - jnp.exp(jnp.sum(lam_q2.astype(f32) * lam_k2.astype(f32))) + lam_init)
    if cache_k is None:
        tq = -(-t // Q_BLOCK) * Q_BLOCK
        k_chunk = prompt_chunk_ids(t)
        q_chunk = prompt_chunk_ids(tq).reshape(-1, Q_BLOCK)

        def block(args):
            qb, qc = args
            return diff_combine(qb, k, v, lam, k_chunk[None, :] <= qc[:, None])

        o = from_blocks(lax.map(block, (to_blocks(q, tq), q_chunk)), t)
    else:
        nb, npast = cache_k.shape[:2]
        k_all = jnp.concatenate([cache_k.reshape(nb, npast, DA_HEADS, 2, DA_HD), k], axis=1)
        v_all = jnp.concatenate([cache_v, v], axis=1)
        o = diff_combine(q, k_all, v_all, lam, None)
    o = rms_norm(o, subln) * (1.0 - lam_init)
    o = o.reshape(b, t, DA_WIDTH) * jax.nn.silu(z)
    return o @ w_out, k.reshape(b, t, DA_HEADS, 2 * DA_HD), v


def sink_attend(s, v, sinks, eq):
    sk = sinks.astype(jnp.float32).reshape(SW_KV, -1)[:, :, None]
    m = jnp.maximum(jnp.max(s, axis=-1), sk)
    e = jnp.exp(s - m[..., None])
    p = e / (jnp.sum(e, axis=-1) + jnp.exp(sk - m))[..., None]
    return jnp.einsum(eq, p.astype(v.dtype), v)


def swa_mixer(h, pos, w_in, w_out, sinks, state_rows, cache_k=None, cache_v=None):
    b, t, _ = h.shape
    kvd = SW_KV * SW_HD
    rep = SW_HEADS // SW_KV
    q, k, v, z = split_cols(h @ w_in, (SW_WIDTH, kvd, kvd, SW_WIDTH))
    q = partial_rope(q.reshape(b, t, SW_HEADS, SW_HD), pos)
    k = partial_rope(k.reshape(b, t, SW_KV, SW_HD), pos)
    v = v.reshape(b, t, SW_KV, SW_HD)
    scale = SW_HD ** -0.5
    if cache_k is None:
        lead = CHUNK - N_META
        n_prev = WINDOW // CHUNK
        kpad = lead + n_prev * CHUNK
        nc = (t + lead) // CHUNK
        qc = jnp.pad(q, ((0, 0), (lead, 0), (0, 0), (0, 0))).reshape(b, nc, CHUNK, SW_KV, rep, SW_HD)
        kc = jnp.pad(k, ((0, 0), (kpad, 0), (0, 0), (0, 0))).reshape(b, nc + n_prev, CHUNK, SW_KV, SW_HD)
        vc = jnp.pad(v, ((0, 0), (kpad, 0), (0, 0), (0, 0))).reshape(b, nc + n_prev, CHUNK, SW_KV, SW_HD)
        valid = jnp.pad(jnp.ones((t,), dtype=bool), (kpad, 0)).reshape(nc + n_prev, CHUNK)
        kb = jnp.concatenate([kc[:, j:j + nc] for j in range(n_prev + 1)], axis=2)
        vb = jnp.concatenate([vc[:, j:j + nc] for j in range(n_prev + 1)], axis=2)
        vmask = jnp.concatenate([valid[j:j + nc] for j in range(n_prev + 1)], axis=1)
        s = jnp.einsum('bcqgrd,bckgd->bcgrqk', qc, kb).astype(jnp.float32) * scale
        s = jnp.where(vmask[None, :, None, None, None, :], s, NEG_INF)
        o = sink_attend(s, vb, sinks, 'bcgrqk,bckgd->bcqgrd')
        o = o.reshape(b, nc * CHUNK, SW_WIDTH)[:, lead:]
        new_k = k[:, t - state_rows:]
        new_v = v[:, t - state_rows:]
    else:
        k_all = jnp.concatenate([cache_k, k], axis=1)
        v_all = jnp.concatenate([cache_v, v], axis=1)
        s = jnp.einsum('bqgrd,bkgd->bgrqk', q.reshape(b, t, SW_KV, rep, SW_HD), k_all).astype(jnp.float32) * scale
        o = sink_attend(s, v_all, sinks, 'bgrqk,bkgd->bqgrd').reshape(b, t, SW_WIDTH)
        new_k = k_all[:, k_all.shape[1] - state_rows:]
        new_v = v_all[:, v_all.shape[1] - state_rows:]
    o = o * jax.nn.silu(z)
    return o @ w_out, new_k, new_v


def dsa_select_attend(q, qi, wi, k, v, ki, topk, q_chunk, k_chunk):
    sc = jnp.einsum('bqhe,bse->bqhs', qi, ki).astype(jnp.float32)
    score = jnp.einsum('bqhs,bqh->bqs', jax.nn.relu(sc), wi.astype(jnp.float32))
    if q_chunk is not None:
        score = jnp.where(k_chunk[None, None, :] <= q_chunk[None, :, None], score, -jnp.inf)
    _, idx = lax.top_k(score, topk)
    gather = jax.vmap(lambda a, i: a[i])
    ks = gather(k, idx)
    vs = gather(v, idx)
    s = jnp.einsum('bqgrd,bqkgd->bqgrk', q, ks).astype(jnp.float32) * (DS_HD ** -0.5)
    if q_chunk is not None:
        ok = k_chunk[idx] <= q_chunk[None, :, None]
        s = jnp.where(ok[:, :, None, None, :], s, NEG_INF)
    p = jax.nn.softmax(s, axis=-1)
    return jnp.einsum('bqgrk,bqkgd->bqgrd', p.astype(v.dtype), vs)


def dsa_mixer(h, pos, w_in, w_out, cache_k=None, cache_v=None, cache_ki=None):
    b, t, _ = h.shape
    kvd = DS_KV * DS_HD
    rep = DS_HEADS // DS_KV
    q, k, v, z, qi, ki, wi = split_cols(
        h @ w_in, (DS_WIDTH, kvd, kvd, DS_WIDTH, IDX_HEADS * IDX_HD, IDX_HD, IDX_HEADS))
    q = partial_rope(q.reshape(b, t, DS_HEADS, DS_HD), pos).reshape(b, t, DS_KV, rep, DS_HD)
    k = partial_rope(k.reshape(b, t, DS_KV, DS_HD), pos)
    v = v.reshape(b, t, DS_KV, DS_HD)
    qi = partial_rope(qi.reshape(b, t, IDX_HEADS, IDX_HD), pos)
    ki = partial_rope(ki.reshape(b, t, 1, IDX_HD), pos)[:, :, 0]
    wi = wi * (IDX_HEADS ** -0.5 * IDX_HD ** -0.5)
    if cache_k is None:
        topk = min(TOPK_MAX, (t - N_META) // 4)
        tq = -(-t // Q_BLOCK) * Q_BLOCK
        k_chunk = prompt_chunk_ids(t)
        q_chunk = prompt_chunk_ids(tq).reshape(-1, Q_BLOCK)

        def block(args):
            qb, qib, wib, qc = args
            return dsa_select_attend(qb, qib, wib, k, v, ki, topk, qc, k_chunk)

        o = from_blocks(lax.map(block, (to_blocks(q, tq), to_blocks(qi, tq), to_blocks(wi, tq), q_chunk)), t)
    else:
        k_all = jnp.concatenate([cache_k, k], axis=1)
        v_all = jnp.concatenate([cache_v, v], axis=1)
        ki_all = jnp.concatenate([cache_ki, ki], axis=1)
        topk = min(TOPK_MAX, k_all.shape[1] // 4)
        o = dsa_select_attend(q, qi, wi, k_all, v_all, ki_all, topk, None, None)
    o = o.reshape(b, t, DS_WIDTH) * jax.nn.silu(z)
    return o @ w_out, k, v, ki


def setup_inputs(seed: int = 0) -> dict:
    key = jax.random.key(seed)
    ks = jax.random.split(key, 48)
    counter = [0]

    def nrm(shape, scale=1.0):
        kk = ks[counter[0]]
        counter[0] += 1
        return jax.random.normal(kk, shape, jnp.float32) * scale

    def gain(n):
        return 1.0 + nrm((n,), 0.02)

    sw_rows = min(WINDOW, PAST_LEN)
    d_in_a = 2 * DA_QK + 2 * DA_WIDTH
    d_in_b = 2 * SW_WIDTH + 2 * SW_KV * SW_HD
    d_in_c = 2 * DS_WIDTH + 2 * DS_KV * DS_HD + IDX_HEADS * IDX_HD + IDX_HD + IDX_HEADS
    din = D_MODEL ** -0.5
    x_prompt = nrm((BATCH, SEQ, D_MODEL))
    x_sample = nrm((DEC_BATCH, DEC_SEQ, D_MODEL))
    cache_l0_k = nrm((DEC_BATCH, PAST_LEN, DA_HEADS, 2 * DA_HD))
    cache_l0_v = nrm((DEC_BATCH, PAST_LEN, DA_HEADS, 2 * DA_HD))
    cache_l1_k = nrm((DEC_BATCH, sw_rows, SW_KV, SW_HD))
    cache_l1_v = nrm((DEC_BATCH, sw_rows, SW_KV, SW_HD))
    cache_l2_k = nrm((DEC_BATCH, PAST_LEN, DS_KV, DS_HD))
    cache_l2_v = nrm((DEC_BATCH, PAST_LEN, DS_KV, DS_HD))
    cache_l2_kidx = nrm((DEC_BATCH, PAST_LEN, IDX_HD))
    cache_l3_k = nrm((DEC_BATCH, PAST_LEN, DA_HEADS, 2 * DA_HD))
    cache_l3_v = nrm((DEC_BATCH, PAST_LEN, DA_HEADS, 2 * DA_HD))
    meta_tokens = nrm((N_META, D_MODEL))
    l0_norm = gain(D_MODEL)
    l0_w_in = nrm((D_MODEL, d_in_a), din)
    l0_w_out = nrm((DA_WIDTH, D_MODEL), DA_WIDTH ** -0.5)
    l0_lam_q1 = nrm((DA_HD,), 0.1)
    l0_lam_k1 = nrm((DA_HD,), 0.1)
    l0_lam_q2 = nrm((DA_HD,), 0.1)
    l0_lam_k2 = nrm((DA_HD,), 0.1)
    l0_subln = gain(2 * DA_HD)
    l1_norm = gain(D_MODEL)
    l1_w_in = nrm((D_MODEL, d_in_b), din)
    l1_w_out = nrm((SW_WIDTH, D_MODEL), SW_WIDTH ** -0.5)
    l1_sinks = nrm((SW_HEADS,), 0.5)
    l2_norm = gain(D_MODEL)
    l2_w_in = nrm((D_MODEL, d_in_c), din)
    l2_w_out = nrm((DS_WIDTH, D_MODEL), DS_WIDTH ** -0.5)
    l3_norm = gain(D_MODEL)
    l3_w_in = nrm((D_MODEL, d_in_a), din)
    l3_w_out = nrm((DA_WIDTH, D_MODEL), DA_WIDTH ** -0.5)
    l3_lam_q1 = nrm((DA_HD,), 0.1)
    l3_lam_k1 = nrm((DA_HD,), 0.1)
    l3_lam_q2 = nrm((DA_HD,), 0.1)
    l3_lam_k2 = nrm((DA_HD,), 0.1)
    l3_subln = gain(2 * DA_HD)
    final_norm = gain(D_MODEL)
    return {
        'x_prompt': x_prompt, 'x_sample': x_sample,
        'cache_l0_k': cache_l0_k, 'cache_l0_v': cache_l0_v,
        'cache_l1_k': cache_l1_k, 'cache_l1_v': cache_l1_v,
        'cache_l2_k': cache_l2_k, 'cache_l2_v': cache_l2_v, 'cache_l2_kidx': cache_l2_kidx,
        'cache_l3_k': cache_l3_k, 'cache_l3_v': cache_l3_v,
        'meta_tokens': meta_tokens,
        'l0_norm': l0_norm, 'l0_w_in': l0_w_in, 'l0_w_out': l0_w_out,
        'l0_lam_q1': l0_lam_q1, 'l0_lam_k1': l0_lam_k1, 'l0_lam_q2': l0_lam_q2, 'l0_lam_k2': l0_lam_k2,
        'l0_subln': l0_subln,
        'l1_norm': l1_norm, 'l1_w_in': l1_w_in, 'l1_w_out': l1_w_out, 'l1_sinks': l1_sinks,
        'l2_norm': l2_norm, 'l2_w_in': l2_w_in, 'l2_w_out': l2_w_out,
        'l3_norm': l3_norm, 'l3_w_in': l3_w_in, 'l3_w_out': l3_w_out,
        'l3_lam_q1': l3_lam_q1, 'l3_lam_k1': l3_lam_k1, 'l3_lam_q2': l3_lam_q2, 'l3_lam_k2': l3_lam_k2,
        'l3_subln': l3_subln,
        'final_norm': final_norm,
    }


def reference(x_prompt, x_sample, cache_l0_k, cache_l0_v, cache_l1_k, cache_l1_v,
              cache_l2_k, cache_l2_v, cache_l2_kidx, cache_l3_k, cache_l3_v, meta_tokens,
              l0_norm, l0_w_in, l0_w_out, l0_lam_q1, l0_lam_k1, l0_lam_q2, l0_lam_k2, l0_subln,
              l1_norm, l1_w_in, l1_w_out, l1_sinks,
              l2_norm, l2_w_in, l2_w_out,
              l3_norm, l3_w_in, l3_w_out, l3_lam_q1, l3_lam_k1, l3_lam_q2, l3_lam_k2, l3_subln,
              final_norm):
    b = x_prompt.shape[0]
    t = N_META + x_prompt.shape[1]
    meta = jnp.broadcast_to(meta_tokens.astype(x_prompt.dtype)[None], (b, N_META, D_MODEL))
    xp = jnp.concatenate([meta, x_prompt], axis=1)
    xs = x_sample
    pos_p = jnp.arange(t)
    pos_s = cache_l0_k.shape[1] + jnp.arange(xs.shape[1])
    layers = [
        dict(norm=l0_norm, w_in=l0_w_in, w_out=l0_w_out, lam=(l0_lam_q1, l0_lam_k1, l0_lam_q2, l0_lam_k2),
             subln=l0_subln, cache=(cache_l0_k, cache_l0_v)),
        dict(norm=l1_norm, w_in=l1_w_in, w_out=l1_w_out, sinks=l1_sinks, cache=(cache_l1_k, cache_l1_v)),
        dict(norm=l2_norm, w_in=l2_w_in, w_out=l2_w_out, cache=(cache_l2_k, cache_l2_v, cache_l2_kidx)),
        dict(norm=l3_norm, w_in=l3_w_in, w_out=l3_w_out, lam=(l3_lam_q1, l3_lam_k1, l3_lam_q2, l3_lam_k2),
             subln=l3_subln, cache=(cache_l3_k, cache_l3_v)),
    ]
    p_st = []
    s_st = []
    for i in range(DEPTH):
        lp = layers[i]
        hp = rms_norm(xp, lp['norm'])
        hs = rms_norm(xs, lp['norm'])
        kind = i % N_MIXERS
        if kind == 0:
            lam_init = 0.8 - 0.6 * math.exp(-0.3 * i)
            args = (lp['w_in'], lp['w_out'], *lp['lam'], lp['subln'], lam_init)
            op, *sp = diff_mixer(hp, pos_p, *args)
            os_, *ss = diff_mixer(hs, pos_s, *args, *lp['cache'])
        elif kind == 1:
            rows = lp['cache'][0].shape[1]
            op, *sp = swa_mixer(hp, pos_p, lp['w_in'], lp['w_out'], lp['sinks'], rows)
            os_, *ss = swa_mixer(hs, pos_s, lp['w_in'], lp['w_out'], lp['sinks'], rows, *lp['cache'])
        else:
            op, *sp = dsa_mixer(hp, pos_p, lp['w_in'], lp['w_out'])
            os_, *ss = dsa_mixer(hs, pos_s, lp['w_in'], lp['w_out'], *lp['cache'])
        xp = xp + op
        xs = xs + os_
        p_st.append(sp)
        s_st.append(ss)
    y_prompt = rms_norm(xp, final_norm)[:, N_META:]
    y_sample = rms_norm(xs, final_norm)
    return (y_prompt, y_sample,
            p_st[0][0], p_st[0][1], s_st[0][0], s_st[0][1],
            p_st[1][0], p_st[1][1], s_st[1][0], s_st[1][1],
            p_st[2][0], p_st[2][1], p_st[2][2], s_st[2][0], s_st[2][1], s_st[2][2],
            p_st[3][0], p_st[3][1], s_st[3][0], s_st[3][1])
```

```python
import functools
import math

import jax
import jax.numpy as jnp
from jax import lax
from jax.experimental import pallas as pl
from jax.experimental.pallas import tpu as pltpu

CHUNK = 64
N_META = 16
ROPE_THETA = 500000.0
NORM_EPS = 1e-6
NEG_INF = -1e30
DA_HEADS = 8
DA_HD = 128
DA_WIDTH = DA_HEADS * 2 * DA_HD
SW_HEADS = 32
SW_KV = 4
SW_HD = 64
SW_WIDTH = SW_HEADS * SW_HD
SW_REP = SW_HEADS // SW_KV
WINDOW = 128
DS_HEADS = 16
DS_KV = 4
DS_HD = 128
DS_WIDTH = DS_HEADS * DS_HD
DS_REP = DS_HEADS // DS_KV
IDX_HEADS = 16
IDX_HD = 64
TOPK_MAX = 256
N_MIXERS = 3

LANES = 128
INT_MIN = -(2 ** 31)
VMEM_LIMIT = 56 * 1024 * 1024

_MXU = jnp.bfloat16
_F32 = jnp.float32

_NT = (((1,), (1,)), ((), ()))


def _cparams(sem):
    return pltpu.CompilerParams(dimension_semantics=sem, vmem_limit_bytes=VMEM_LIMIT)


def _row_tile(m, pref):
    t = pref
    while t > 8 and m % t:
        t //= 2
    assert m % t == 0, (m, pref)
    return t


def _dot(a, b):
    return jnp.dot(a, b, preferred_element_type=_F32)


def _dot_nt(a, b):
    return lax.dot_general(a, b, _NT, preferred_element_type=_F32)


def _norm_kernel(x_ref, g_ref, h_ref):
    x = x_ref[...]
    y = x * lax.rsqrt(jnp.mean(x * x, axis=-1, keepdims=True) + NORM_EPS)
    h_ref[...] = (y * g_ref[...]).astype(h_ref.dtype)


def _rms_norm_rows(x, g):
    m, d = x.shape
    out_dtype = _MXU
    tm = _row_tile(m, 512)
    return pl.pallas_call(
        _norm_kernel,
        grid=(m // tm,),
        in_specs=[pl.BlockSpec((tm, d), lambda i: (i, 0)),
                  pl.BlockSpec((1, d), lambda i: (0, 0))],
        out_specs=pl.BlockSpec((tm, d), lambda i: (i, 0)),
        out_shape=jax.ShapeDtypeStruct((m, d), out_dtype),
        compiler_params=_cparams(("parallel",)),
        name="rms_norm",
    )(x, g.reshape(1, d).astype(_F32))


def _proj_kernel(*refs, rope_shift, scale, n_out, tn):
    h_ref, w_ref = refs[0], refs[1]
    pos = 2
    if rope_shift is not None:
        c_ref, a_ref, b_ref = refs[2:5]
        pos = 5
    outs = refs[pos:pos + n_out]
    h = h_ref[...]
    cw = 2 * LANES if tn % (2 * LANES) == 0 else LANES
    for c in range(tn // cw):
        wide = _dot(h, w_ref[:, c * cw:(c + 1) * cw])
        for sub in range(cw // LANES):
            acc = wide[:, sub * LANES:(sub + 1) * LANES]
            if rope_shift is not None:
                acc = (acc * c_ref[...]
                       + pltpu.roll(acc, rope_shift, 1) * a_ref[...]
                       + pltpu.roll(acc, LANES - rope_shift, 1) * b_ref[...])
            elif scale is not None:
                acc = acc * scale
            cols = slice(c * cw + sub * LANES, c * cw + (sub + 1) * LANES)
            for o in outs:
                o[:, cols] = acc.astype(o.dtype)


def _project(h, w, out_dtypes, tabs=None, rope_shift=None, scale=None):
    m, d = h.shape
    n = w.shape[1]
    tm = _row_tile(m if tabs is None else math.gcd(m, tabs[0].shape[0]), 512)
    tn = n if n <= 1024 else 1024
    assert n % tn == 0 and tn % LANES == 0
    in_specs = [pl.BlockSpec((tm, d), lambda j, i: (i, 0)),
                pl.BlockSpec((d, tn), lambda j, i: (0, j))]
    args = [h, w]
    if tabs is not None:
        trows = tabs[0].shape[0]
        assert trows % tm == 0
        nrep = trows // tm
        for t in tabs:
            in_specs.append(pl.BlockSpec((tm, LANES), lambda j, i: (i % nrep, 0)))
            args.append(t)
    else:
        rope_shift = None
    outs = pl.pallas_call(
        functools.partial(_proj_kernel, rope_shift=rope_shift, scale=scale,
                          n_out=len(out_dtypes), tn=tn),
        grid=(n // tn, m // tm),
        in_specs=in_specs,
        out_specs=[pl.BlockSpec((tm, tn), lambda j, i: (i, j)) for _ in out_dtypes],
        out_shape=[jax.ShapeDtypeStruct((m, n), dt) for dt in out_dtypes],
        compiler_params=_cparams(("parallel", "parallel")),
        name="in_proj",
    )(*args)
    return outs


def _outproj_kernel(o_ref, z_ref, w_ref, x_ref, g_ref, *outs, emit_x):
    z = z_ref[...].astype(_F32)
    a = o_ref[...].astype(_F32) * (z / (1.0 + jnp.exp(-z)))
    y = x_ref[...] + _dot(a.astype(w_ref.dtype), w_ref[...])
    if emit_x:
        outs[0][...] = y
    n = y * lax.rsqrt(jnp.mean(y * y, axis=-1, keepdims=True) + NORM_EPS) * g_ref[...]
    outs[-1][...] = n.astype(outs[-1].dtype)


def _out_project(o, z, w, x, g_next, last):
    m, d = x.shape
    wd = o.shape[1]
    tm = _row_tile(m, 256)
    row = lambda i: (i, 0)
    fixed = lambda i: (0, 0)
    if last:
        out_shape = [jax.ShapeDtypeStruct((m, d), _F32)]
    else:
        out_shape = [jax.ShapeDtypeStruct((m, d), _F32), jax.ShapeDtypeStruct((m, d), _MXU)]
    return pl.pallas_call(
        functools.partial(_outproj_kernel, emit_x=not last),
        grid=(m // tm,),
        in_specs=[pl.BlockSpec((tm, wd), row), pl.BlockSpec((tm, wd), row),
                  pl.BlockSpec((wd, d), fixed), pl.BlockSpec((tm, d), row),
                  pl.BlockSpec((1, d), fixed)],
        out_specs=[pl.BlockSpec((tm, d), row) for _ in out_shape],
        out_shape=out_shape,
        compiler_params=_cparams(("parallel",)),
        name="out_proj",
    )(o, z, w, x, g_next.reshape(1, d).astype(_F32))


def _rope_tables(pos, hd, scale=1.0):
    rd = hd // 4
    half = rd // 2
    lane = jnp.arange(LANES)
    d = lane % hd
    f = d % half
    inv_freq = ROPE_THETA ** (-(jnp.arange(half, dtype=_F32) * 2.0 / rd))
    ang = pos.astype(_F32)[:, None] * inv_freq[f][None, :]
    cos, sin = jnp.cos(ang), jnp.sin(ang)
    in_rot = (d < rd)[None, :]
    c = jnp.where(in_rot, cos, 1.0)
    a = jnp.where(((d >= half) & (d < rd))[None, :], sin, 0.0)
    b = jnp.where((d < half)[None, :], -sin, 0.0)
    s = jnp.float32(scale)
    return (c * s, a * s, b * s), half


def _lam_value(lam_ref, lam_init):
    l = lam_ref[...]
    s1 = jnp.sum(l[0:1] * l[1:2], axis=-1, keepdims=True)
    s2 = jnp.sum(l[2:3] * l[3:4], axis=-1, keepdims=True)
    return jnp.exp(s1) - jnp.exp(s2) + lam_init


def _diff_finish(acc1, l1, acc2, l2, lam, sub, lam_init):
    o = acc1 / l1 - lam * (acc2 / l2)
    o = o * lax.rsqrt(jnp.mean(o * o, axis=-1, keepdims=True) + NORM_EPS) * sub
    return o * (1.0 - lam_init)


def _diff_frames_kernel(q_ref, k_ref, v_ref, km_ref, vm_ref, lam_ref, sub_ref, o_ref,
                        m_s, l_s, acc_s, *, tq, lam_init):
    i = pl.program_id(2)
    q = q_ref[...]
    qs = (q[:, :DA_HD], q[:, DA_HD:])

    km = km_ref[...]
    vm = vm_ref[...]
    for j in range(2):
        s = _dot_nt(qs[j], km[:, j * DA_HD:(j + 1) * DA_HD])
        m = jnp.max(s, axis=-1, keepdims=True)
        p = jnp.exp(s - m)
        m_s[j] = m
        l_s[j] = jnp.sum(p, axis=-1, keepdims=True)
        acc_s[j] = _dot(p.astype(vm.dtype), vm)

    def step(t, masked):
        start = pl.multiple_of(t * tq, tq)
        ks = k_ref[pl.ds(start, tq), :]
        vs = v_ref[pl.ds(start, tq), :]
        if masked:
            rc = lax.broadcasted_iota(jnp.int32, (tq, tq), 0) // CHUNK
            cc = lax.broadcasted_iota(jnp.int32, (tq, tq), 1) // CHUNK
            ok = cc <= rc
        for j in range(2):
            s = _dot_nt(qs[j], ks[:, j * DA_HD:(j + 1) * DA_HD])
            if masked:
                s = jnp.where(ok, s, NEG_INF)
            m_prev = m_s[j]
            m_new = jnp.maximum(m_prev, jnp.max(s, axis=-1, keepdims=True))
            alpha = jnp.exp(m_prev - m_new)
            p = jnp.exp(s - m_new)
            l_s[j] = alpha * l_s[j] + jnp.sum(p, axis=-1, keepdims=True)
            acc_s[j] = alpha * acc_s[j] + _dot(p.astype(vs.dtype), vs)
            m_s[j] = m_new

    def body(t, carry):
        step(t, False)
        return carry

    lax.fori_loop(0, i, body, 0)
    step(i, True)

    lam = _lam_value(lam_ref, lam_init)
    o = _diff_finish(acc_s[0], l_s[0], acc_s[1], l_s[1], lam, sub_ref[...], lam_init)
    o_ref[...] = o.astype(o_ref.dtype)


def _diff_frames(q, k, v, km, vm, lams, subln, lam_init, nb, seq):
    m = q.shape[0]
    hw = 2 * DA_HD
    tq = _row_tile(seq, 512)
    nt = seq // tq
    return pl.pallas_call(
        functools.partial(_diff_frames_kernel, tq=tq, lam_init=lam_init),
        grid=(nb, DA_HEADS, nt),
        in_specs=[pl.BlockSpec((tq, hw), lambda b, h, i: (b * nt + i, h)),
                  pl.BlockSpec((seq, hw), lambda b, h, i: (b, h)),
                  pl.BlockSpec((seq, hw), lambda b, h, i: (b, h)),
                  pl.BlockSpec((N_META, hw), lambda b, h, i: (0, h)),
                  pl.BlockSpec((N_META, hw), lambda b, h, i: (0, h)),
                  pl.BlockSpec((4, DA_HD), lambda b, h, i: (0, 0)),
                  pl.BlockSpec((1, hw), lambda b, h, i: (0, 0))],
        out_specs=pl.BlockSpec((tq, hw), lambda b, h, i: (b * nt + i, h)),
        out_shape=jax.ShapeDtypeStruct((m, DA_WIDTH), _MXU),
        scratch_shapes=[pltpu.VMEM((2, tq, 1), _F32), pltpu.VMEM((2, tq, 1), _F32),
                        pltpu.VMEM((2, tq, hw), _F32)],
        compiler_params=_cparams(("parallel", "parallel", "arbitrary")),
        name="diff_frames",
    )(q, k, v, km, vm, lams, subln)


def _diff_small_kernel(*refs, has_cache, lam_init):
    if has_cache:
        q_ref, k_ref, v_ref, ck_ref, cv_ref, lam_ref, sub_ref, o_ref = refs
    else:
        q_ref, k_ref, v_ref, lam_ref, sub_ref, o_ref = refs
    q = q_ref[...]
    kn = k_ref[...]
    vn = v_ref[...]
    parts = [(kn, vn)]
    if has_cache:
        parts.append((ck_ref[0].astype(kn.dtype), cv_ref[0].astype(vn.dtype)))
    accs, ls = [], []
    for j in range(2):
        qj = q[:, j * DA_HD:(j + 1) * DA_HD]
        ss = [_dot_nt(qj, kp[:, j * DA_HD:(j + 1) * DA_HD]) for kp, _ in parts]
        m = ss[0].max(axis=-1, keepdims=True)
        for s in ss[1:]:
            m = jnp.maximum(m, s.max(axis=-1, keepdims=True))
        l = 0.0
        acc = 0.0
        for s, (_, vp) in zip(ss, parts):
            p = jnp.exp(s - m)
            l = l + jnp.sum(p, axis=-1, keepdims=True)
            acc = acc + _dot(p.astype(vp.dtype), vp)
        accs.append(acc)
        ls.append(l)
    lam = _lam_value(lam_ref, lam_init)
    o = _diff_finish(accs[0], ls[0], accs[1], ls[1], lam, sub_ref[...], lam_init)
    o_ref[...] = o.astype(o_ref.dtype)


def _diff_small(q, k, v, cache_k, cache_v, lams, subln, lam_init, nb, nq):
    hw = 2 * DA_HD
    has_cache = cache_k is not None
    qspec = pl.BlockSpec((nq, hw), lambda b, h: (b, h))
    in_specs = [qspec, qspec, qspec]
    args = [q, k, v]
    if has_cache:
        past = cache_k.shape[1]
        cspec = pl.BlockSpec((1, past, hw), lambda b, h: (b, 0, h))
        in_specs += [cspec, cspec]
        args += [cache_k, cache_v]
    in_specs += [pl.BlockSpec((4, DA_HD), lambda b, h: (0, 0)),
                 pl.BlockSpec((1, hw), lambda b, h: (0, 0))]
    args += [lams, subln]
    return pl.pallas_call(
        functools.partial(_diff_small_kernel, has_cache=has_cache, lam_init=lam_init),
        grid=(nb, DA_HEADS),
        in_specs=in_specs,
        out_specs=qspec,
        out_shape=jax.ShapeDtypeStruct((nb * nq, DA_WIDTH), _MXU),
        compiler_params=_cparams(("parallel", "parallel")),
        name="diff_small",
    )(*args)


def _stack_heads(qg, nheads):
    n = qg.shape[0]
    lane = lax.broadcasted_iota(jnp.int32, (n, LANES), 1)
    lo = lane < SW_HD
    pieces = []
    for r in range(nheads):
        blk = qg[:, (r // 2) * LANES:(r // 2 + 1) * LANES]
        keep = lo if r % 2 == 0 else jnp.logical_not(lo)
        pieces.append(jnp.where(keep, blk, jnp.zeros_like(blk)))
    return jnp.concatenate(pieces, axis=0)


def _unstack_heads(o, n, nheads):
    lane = lax.broadcasted_iota(jnp.int32, (n, LANES), 1)
    lo = lane < SW_HD
    return [jnp.where(lo, o[(2 * p) * n:(2 * p + 1) * n], o[(2 * p + 1) * n:(2 * p + 2) * n])
            for p in range(nheads // 2)]


def _sink_softmax_pv(ss, vs, sk):
    m = sk
    for s in ss:
        m = jnp.maximum(m, jnp.max(s, axis=-1, keepdims=True))
    den = jnp.exp(sk - m)
    acc = 0.0
    for s, v in zip(ss, vs):
        e = jnp.exp(s - m)
        den = den + jnp.sum(e, axis=-1, keepdims=True)
        acc = acc + _dot(e.astype(v.dtype), v)
    return acc / den


def _swa_frames_kernel(q_ref, kc_ref, vc_ref, kp_ref, vp_ref, km_ref, vm_ref, sk_ref, o_ref,
                       kcat, vcat, *, tq):
    i = pl.program_id(1)
    first = i == 0
    prev_rows = WINDOW
    kcat[0:prev_rows] = jnp.where(first, km_ref[...], kp_ref[...])
    vcat[0:prev_rows] = jnp.where(first, vm_ref[...], vp_ref[...])
    kcat[prev_rows:prev_rows + tq] = kc_ref[...]
    vcat[prev_rows:prev_rows + tq] = vc_ref[...]
    nwin = WINDOW + CHUNK
    for c in range(tq // CHUNK):
        rows = slice(c * CHUNK, (c + 1) * CHUNK)
        win = slice(c * CHUNK, c * CHUNK + nwin)
        for g in range(SW_KV):
            gl = slice(g * LANES, (g + 1) * LANES)
            qst = _stack_heads(q_ref[rows, g * SW_REP * SW_HD:(g + 1) * SW_REP * SW_HD], SW_REP)
            s = _dot_nt(qst, kcat[win, gl])
            if c * CHUNK < prev_rows - N_META:
                col = lax.broadcasted_iota(jnp.int32, s.shape, 1) + c * CHUNK
                ok = jnp.logical_or(col >= prev_rows - N_META, jnp.logical_not(first))
                s = jnp.where(ok, s, NEG_INF)
            o = _sink_softmax_pv([s], [vcat[win, gl]], sk_ref[g][:, 0:1])
            for p, blk in enumerate(_unstack_heads(o, CHUNK, SW_REP)):
                lo = g * SW_REP * SW_HD + p * LANES
                o_ref[rows, lo:lo + LANES] = blk.astype(o_ref.dtype)


def _swa_frames(q, kd, vd, kmeta_blk, vmeta_blk, sink_rows, nb, seq):
    m = q.shape[0]
    tq = _row_tile(seq, 512)
    nt = seq // tq
    pw = SW_KV * LANES
    per = tq // WINDOW
    cur = lambda b, i: (b * nt + i, 0)
    prev = lambda b, i: (jnp.maximum((b * nt + i) * per - 1, 0), 0)
    fixed = lambda b, i: (0, 0)
    return pl.pallas_call(
        functools.partial(_swa_frames_kernel, tq=tq),
        grid=(nb, nt),
        in_specs=[pl.BlockSpec((tq, SW_WIDTH), cur),
                  pl.BlockSpec((tq, pw), cur), pl.BlockSpec((tq, pw), cur),
                  pl.BlockSpec((WINDOW, pw), prev), pl.BlockSpec((WINDOW, pw), prev),
                  pl.BlockSpec((WINDOW, pw), fixed), pl.BlockSpec((WINDOW, pw), fixed),
                  pl.BlockSpec((SW_KV, SW_REP * CHUNK, LANES), lambda b, i: (0, 0, 0))],
        out_specs=pl.BlockSpec((tq, SW_WIDTH), cur),
        out_shape=jax.ShapeDtypeStruct((m, SW_WIDTH), _MXU),
        scratch_shapes=[pltpu.VMEM((WINDOW + tq, pw), _MXU), pltpu.VMEM((WINDOW + tq, pw), _MXU)],
        compiler_params=_cparams(("parallel", "arbitrary")),
        name="swa_frames",
    )(q, kd, vd, kd, vd, kmeta_blk, vmeta_blk, sink_rows)


def _swa_small_kernel(*refs, has_cache, nq):
    if has_cache:
        q_ref, k_ref, v_ref, ck_ref, cv_ref, sk_ref, o_ref = refs
    else:
        q_ref, k_ref, v_ref, sk_ref, o_ref = refs
    for g in range(SW_KV):
        gl = slice(g * LANES, (g + 1) * LANES)
        qst = _stack_heads(q_ref[:, g * SW_REP * SW_HD:(g + 1) * SW_REP * SW_HD], SW_REP)
        ks, vs = [k_ref[:, gl]], [v_ref[:, gl]]
        if has_cache:
            ks.append(ck_ref[0][:, gl])
            vs.append(cv_ref[0][:, gl])
        ss = [_dot_nt(qst, kk) for kk in ks]
        o = _sink_softmax_pv(ss, vs, sk_ref[g][:, 0:1])
        for p, blk in enumerate(_unstack_heads(o, nq, SW_REP)):
            lo = g * SW_REP * SW_HD + p * LANES
            o_ref[:, lo:lo + LANES] = blk.astype(o_ref.dtype)


def _swa_small(q, kd, vd, cache_kd, cache_vd, sink_rows, nb, nq):
    pw = SW_KV * LANES
    has_cache = cache_kd is not None
    row = lambda b: (b, 0)
    in_specs = [pl.BlockSpec((nq, SW_WIDTH), row), pl.BlockSpec((nq, pw), row), pl.BlockSpec((nq, pw), row)]
    args = [q, kd, vd]
    if has_cache:
        past = cache_kd.shape[1]
        cspec = pl.BlockSpec((1, past, pw), lambda b: (b, 0, 0))
        in_specs += [cspec, cspec]
        args += [cache_kd, cache_vd]
    in_specs.append(pl.BlockSpec((SW_KV, SW_REP * nq, LANES), lambda b: (0, 0, 0)))
    args.append(sink_rows)
    return pl.pallas_call(
        functools.partial(_swa_small_kernel, has_cache=has_cache, nq=nq),
        grid=(nb,),
        in_specs=in_specs,
        out_specs=pl.BlockSpec((nq, SW_WIDTH), row),
        out_shape=jax.ShapeDtypeStruct((nb * nq, SW_WIDTH), _MXU),
        compiler_params=_cparams(("parallel",)),
        name="swa_small",
    )(*args)


def _order_key(score, valid):
    bits = lax.bitcast_convert_type(score + 0.0, jnp.int32)
    key = jnp.where(bits < 0, bits ^ jnp.int32(0x7FFFFFFF), bits)
    return jnp.where(valid, key, jnp.int32(INT_MIN))


def _stack_idx_heads(qi):
    return _stack_heads(qi, IDX_HEADS)


def _index_scores(qst, wb_of, kid, n):
    score = None
    hb = 4
    for h0 in range(0, IDX_HEADS, hb):
        sc = _dot_nt(qst[h0 * n:(h0 + hb) * n], kid)
        for h in range(h0, h0 + hb):
            part = jnp.maximum(sc[(h - h0) * n:(h - h0 + 1) * n], 0.0) * wb_of(h)
            score = part if score is None else score + part
    return score


def _dsa_frames_kernel(q_ref, k_ref, v_ref, qi_ref, kid_ref, wi_ref, km_ref, vm_ref, kidm_ref, o_ref,
                       keym_s, key_s, biasm_s, bias_s, qst_s, wb_s, m_s, l_s, acc_s, *, tq, tk, topk):
    i = pl.program_id(1)
    n_t = (i * tq + tq + tk - 1) // tk
    qst_s[...] = _stack_idx_heads(qi_ref[...])
    wi = wi_ref[...]
    for h in range(IDX_HEADS):
        wb_s[h] = jnp.broadcast_to(wi[:, h:h + 1], (tq, LANES))

    def wb_tile(width):
        return lambda h: jnp.concatenate([wb_s[h]] * (width // LANES), axis=1)

    qst = qst_s[...]
    sc_m = _index_scores(qst, wb_tile(LANES), kidm_ref[...], tq)
    col_m = lax.broadcasted_iota(jnp.int32, (tq, LANES), 1)
    keym_s[...] = _order_key(sc_m, col_m < N_META)

    row_chunk = (lax.broadcasted_iota(jnp.int32, (tq, tk), 0) + i * tq) // CHUNK

    def score_body(t, carry):
        start = pl.multiple_of(t * tk, tk)
        sc = _index_scores(qst_s[...], wb_tile(tk), kid_ref[pl.ds(start, tk), :], tq)
        col_chunk = (lax.broadcasted_iota(jnp.int32, (tq, tk), 1) + t * tk) // CHUNK
        key_s[t] = _order_key(sc, col_chunk <= row_chunk)
        return carry

    lax.fori_loop(0, n_t, score_body, 0)

    def count_ge(thr):
        thr_b = jnp.broadcast_to(thr, (tq, LANES))
        part = jnp.where(keym_s[...] >= thr_b, 1.0, 0.0)

        def cbody(t, acc):
            blk = key_s[t]
            for c in range(tk // LANES):
                acc = acc + jnp.where(blk[:, c * LANES:(c + 1) * LANES] >= thr_b, 1.0, 0.0)
            return acc

        part = lax.fori_loop(0, n_t, cbody, part)
        return jnp.sum(part, axis=-1, keepdims=True)

    def bit_body(it, u):
        bit = jnp.int32(31) - it
        cand = u | lax.shift_left(jnp.int32(1), bit)
        cnt = count_ge(cand ^ jnp.int32(INT_MIN))
        return jnp.where(cnt >= float(topk), cand, u)

    u = lax.fori_loop(0, 32, bit_body, jnp.zeros((tq, 1), jnp.int32))
    thr = jnp.maximum(u ^ jnp.int32(INT_MIN), jnp.int32(INT_MIN + 1))

    biasm_s[...] = jnp.where(keym_s[...] >= thr, 0.0, NEG_INF)

    def bias_body(t, carry):
        bias_s[t] = jnp.where(key_s[t] >= thr, 0.0, NEG_INF)
        return carry

    lax.fori_loop(0, n_t, bias_body, 0)

    for g in range(DS_KV):
        gl = slice(g * DS_HD, (g + 1) * DS_HD)
        qg = jnp.concatenate(
            [q_ref[:, (g * DS_REP + r) * DS_HD:(g * DS_REP + r + 1) * DS_HD] for r in range(DS_REP)], axis=0)
        bm = jnp.concatenate([biasm_s[...]] * DS_REP, axis=0)
        s = _dot_nt(qg, km_ref[:, gl]) + bm
        m = jnp.max(s, axis=-1, keepdims=True)
        p = jnp.exp(s - m)
        m_s[...] = m
        l_s[...] = jnp.sum(p, axis=-1, keepdims=True)
        acc_s[...] = _dot(p.astype(_MXU), vm_ref[:, gl])

        def att_body(t, carry, qg=qg, gl=gl):
            start = pl.multiple_of(t * tk, tk)
            bt = bias_s[t]
            s = _dot_nt(qg, k_ref[pl.ds(start, tk), gl]) + jnp.concatenate([bt] * DS_REP, axis=0)
            m_prev = m_s[...]
            m_new = jnp.maximum(m_prev, jnp.max(s, axis=-1, keepdims=True))
            alpha = jnp.exp(m_prev - m_new)
            p = jnp.exp(s - m_new)
            l_s[...] = alpha * l_s[...] + jnp.sum(p, axis=-1, keepdims=True)
            acc_s[...] = alpha * acc_s[...] + _dot(p.astype(_MXU), v_ref[pl.ds(start, tk), gl])
            m_s[...] = m_new
            return carry

        lax.fori_loop(0, n_t, att_body, 0)
        og = acc_s[...] / l_s[...]
        for r in range(DS_REP):
            lo = (g * DS_REP + r) * DS_HD
            o_ref[:, lo:lo + DS_HD] = og[r * tq:(r + 1) * tq].astype(o_ref.dtype)


def _dsa_frames(q, k, v, qi, kid, wi, km_blk, vm_blk, kidm_blk, nb, seq, topk):
    m = q.shape[0]
    tq = _row_tile(seq, 128)
    tk = _row_tile(seq, 512)
    nt = seq // tq
    kvw = DS_KV * DS_HD
    cur = lambda b, i: (b * nt + i, 0)
    per_b = lambda b, i: (b, 0)
    fixed = lambda b, i: (0, 0)
    return pl.pallas_call(
        functools.partial(_dsa_frames_kernel, tq=tq, tk=tk, topk=topk),
        grid=(nb, nt),
        in_specs=[pl.BlockSpec((tq, DS_WIDTH), cur),
                  pl.BlockSpec((seq, kvw), per_b), pl.BlockSpec((seq, kvw), per_b),
                  pl.BlockSpec((tq, IDX_HEADS * IDX_HD), cur),
                  pl.BlockSpec((seq, LANES), per_b),
                  pl.BlockSpec((tq, LANES), cur),
                  pl.BlockSpec((LANES, kvw), fixed), pl.BlockSpec((LANES, kvw), fixed),
                  pl.BlockSpec((LANES, LANES), fixed)],
        out_specs=pl.BlockSpec((tq, DS_WIDTH), cur),
        out_shape=jax.ShapeDtypeStruct((m, DS_WIDTH), _MXU),
        scratch_shapes=[pltpu.VMEM((tq, LANES), jnp.int32), pltpu.VMEM((seq // tk, tq, tk), jnp.int32),
                        pltpu.VMEM((tq, LANES), _F32), pltpu.VMEM((seq // tk, tq, tk), _F32),
                        pltpu.VMEM((IDX_HEADS * tq, LANES), _MXU),
                        pltpu.VMEM((IDX_HEADS, tq, LANES), _F32),
                        pltpu.VMEM((DS_REP * tq, 1), _F32), pltpu.VMEM((DS_REP * tq, 1), _F32),
                        pltpu.VMEM((DS_REP * tq, DS_HD), _F32)],
        compiler_params=_cparams(("parallel", "arbitrary")),
        name="dsa_frames",
    )(q, k, v, qi, kid, wi, km_blk, vm_blk, kidm_blk)


def _dsa_small_kernel(*refs, has_cache, nq, n_new, topk):
    if has_cache:
        q_ref, kn_ref, vn_ref, qi_ref, kidn_ref, wi_ref, ck_ref, cv_ref, ckid_ref, o_ref = refs
    else:
        q_ref, kn_ref, vn_ref, qi_ref, kidn_ref, wi_ref, o_ref = refs
    qst = _stack_idx_heads(qi_ref[...])
    wi = wi_ref[...]
    wbs = [jnp.broadcast_to(wi[:, h:h + 1], (nq, LANES)) for h in range(IDX_HEADS)]

    def wb_tile(width):
        return lambda h: jnp.concatenate([wbs[h]] * (width // LANES), axis=1)

    col_n = lax.broadcasted_iota(jnp.int32, (nq, LANES), 1)
    keys = [_order_key(_index_scores(qst, wb_tile(LANES), kidn_ref[...], nq), col_n < n_new)]
    if has_cache:
        past = ckid_ref.shape[1]
        sc = _index_scores(qst, wb_tile(past), ckid_ref[0], nq)
        keys.append(_order_key(sc, jnp.full(sc.shape, True)))

    def count_ge(thr):
        cnt = 0.0
        for kk in keys:
            cnt = cnt + jnp.sum(jnp.where(kk >= thr, 1.0, 0.0), axis=-1, keepdims=True)
        return cnt

    def bit_body(it, u):
        bit = jnp.int32(31) - it
        cand = u | lax.shift_left(jnp.int32(1), bit)
        cnt = count_ge(cand ^ jnp.int32(INT_MIN))
        return jnp.where(cnt >= float(topk), cand, u)

    u = lax.fori_loop(0, 32, bit_body, jnp.zeros((nq, 1), jnp.int32))
    thr = jnp.maximum(u ^ jnp.int32(INT_MIN), jnp.int32(INT_MIN + 1))
    biases = [jnp.where(kk >= thr, 0.0, NEG_INF) for kk in keys]

    for g in range(DS_KV):
        gl = slice(g * DS_HD, (g + 1) * DS_HD)
        qg = jnp.concatenate(
            [q_ref[:, (g * DS_REP + r) * DS_HD:(g * DS_REP + r + 1) * DS_HD] for r in range(DS_REP)], axis=0)
        ks, vs = [kn_ref[:, gl]], [vn_ref[:, gl]]
        if has_cache:
            ks.append(ck_ref[0][:, gl].astype(_MXU))
            vs.append(cv_ref[0][:, gl].astype(_MXU))
        ss = [_dot_nt(qg, kk) + jnp.concatenate([bb] * DS_REP, axis=0) for kk, bb in zip(ks, biases)]
        m = ss[0].max(axis=-1, keepdims=True)
        for s in ss[1:]:
            m = jnp.maximum(m, s.max(axis=-1, keepdims=True))
        l = 0.0
        acc = 0.0
        for s, vv in zip(ss, vs):
            p = jnp.exp(s - m)
            l = l + jnp.sum(p, axis=-1, keepdims=True)
            acc = acc + _dot(p.astype(_MXU), vv)
        og = acc / l
        for r in range(DS_REP):
            lo = (g * DS_REP + r) * DS_HD
            o_ref[:, lo:lo + DS_HD] = og[r * nq:(r + 1) * nq].astype(o_ref.dtype)


def _dsa_small(q, kn_blk, vn_blk, qi, kidn_blk, wi, cache_k, cache_v, cache_kid, nb, nq, n_new, topk):
    kvw = DS_KV * DS_HD
    has_cache = cache_k is not None
    row = lambda b: (b, 0)
    in_specs = [pl.BlockSpec((nq, DS_WIDTH), row),
                pl.BlockSpec((LANES, kvw), row), pl.BlockSpec((LANES, kvw), row),
                pl.BlockSpec((nq, IDX_HEADS * IDX_HD), row),
                pl.BlockSpec((LANES, LANES), row),
                pl.BlockSpec((nq, LANES), row)]
    args = [q, kn_blk, vn_blk, qi, kidn_blk, wi]
    if has_cache:
        past = cache_k.shape[1]
        in_specs += [pl.BlockSpec((1, past, kvw), lambda b: (b, 0, 0)),
                     pl.BlockSpec((1, past, kvw), lambda b: (b, 0, 0)),
                     pl.BlockSpec((1, past, LANES), lambda b: (b, 0, 0))]
        args += [cache_k, cache_v, cache_kid]
    return pl.pallas_call(
        functools.partial(_dsa_small_kernel, has_cache=has_cache, nq=nq, n_new=n_new, topk=topk),
        grid=(nb,),
        in_specs=in_specs,
        out_specs=pl.BlockSpec((nq, DS_WIDTH), row),
        out_shape=jax.ShapeDtypeStruct((nb * nq, DS_WIDTH), _MXU),
        compiler_params=_cparams(("parallel",)),
        name="dsa_small",
    )(*args)


def _pad_rows_front(a, rows):
    return jnp.pad(a, ((rows - a.shape[0], 0), (0, 0)))


def _pad_rows_back(a, nb, nq, rows):
    w = a.shape[1]
    return jnp.pad(a.reshape(nb, nq, w), ((0, 0), (0, rows - nq), (0, 0))).reshape(nb * rows, w)


def _dup_pairs(a, hd):
    lead = a.shape[:-1]
    g = a.shape[-1] // hd
    a = a.reshape(*lead, g, 1, hd)
    return jnp.broadcast_to(a, (*lead, g, 2, hd)).reshape(*lead, g * 2 * hd)


def kernel(x_prompt, x_sample, cache_l0_k, cache_l0_v, cache_l1_k, cache_l1_v, cache_l2_k, cache_l2_v, cache_l2_kidx, cache_l3_k, cache_l3_v, meta_tokens, l0_norm, l0_w_in, l0_w_out, l0_lam_q1, l0_lam_k1, l0_lam_q2, l0_lam_k2, l0_subln, l1_norm, l1_w_in, l1_w_out, l1_sinks, l2_norm, l2_w_in, l2_w_out, l3_norm, l3_w_in, l3_w_out, l3_lam_q1, l3_lam_k1, l3_lam_q2, l3_lam_k2, l3_subln, final_norm):
    nb, seq, d = x_prompt.shape
    db, ds, _ = x_sample.shape
    past = cache_l0_k.shape[1]
    sw_rows = cache_l1_k.shape[1]
    assert seq % CHUNK == 0 and meta_tokens.shape[0] == N_META and ds <= LANES

    xs = {"f": x_prompt.reshape(nb * seq, d), "m": meta_tokens.astype(_F32), "s": x_sample.reshape(db * ds, d)}
    pos = {"f": N_META + jnp.arange(seq), "m": jnp.arange(N_META), "s": past + jnp.arange(ds)}
    pos["s"] = jnp.tile(pos["s"], db)

    tabs128 = {n: _rope_tables(p, 128, 1.0) for n, p in pos.items()}
    tabs128_da = {n: _rope_tables(p, 128, DA_HD ** -0.5) for n, p in pos.items()}
    tabs128_ds = {n: _rope_tables(p, 128, DS_HD ** -0.5) for n, p in pos.items()}
    tabs64 = {n: _rope_tables(p, 64, 1.0) for n, p in pos.items()}
    tabs64_sw = {n: _rope_tables(p, 64, SW_HD ** -0.5) for n, p in pos.items()}

    layers = [
        dict(norm=l0_norm, w_in=l0_w_in, w_out=l0_w_out, lam=(l0_lam_q1, l0_lam_k1, l0_lam_q2, l0_lam_k2),
             subln=l0_subln, cache=(cache_l0_k, cache_l0_v)),
        dict(norm=l1_norm, w_in=l1_w_in, w_out=l1_w_out, sinks=l1_sinks, cache=(cache_l1_k, cache_l1_v)),
        dict(norm=l2_norm, w_in=l2_w_in, w_out=l2_w_out, cache=(cache_l2_k, cache_l2_v, cache_l2_kidx)),
        dict(norm=l3_norm, w_in=l3_w_in, w_out=l3_w_out, lam=(l3_lam_q1, l3_lam_k1, l3_lam_q2, l3_lam_k2),
             subln=l3_subln, cache=(cache_l3_k, cache_l3_v)),
    ]
    norms = [lp["norm"] for lp in layers[1:]] + [final_norm]
    depth = len(layers)

    hs = {n: _rms_norm_rows(x, layers[0]["norm"]) for n, x in xs.items()}
    p_st, s_st = [], []
    ys = {}
    for li, lp in enumerate(layers):
        kind = li % N_MIXERS
        w_in = lp["w_in"].astype(_MXU)
        w_out = lp["w_out"].astype(_MXU)
        o, z = {}, {}
        if kind == 0:
            lam_init = 0.8 - 0.6 * math.exp(-0.3 * li)
            wq, wk, wv, wz = (w_in[:, i * DA_WIDTH:(i + 1) * DA_WIDTH] for i in range(4))
            lams = jnp.stack([l.astype(_F32) for l in lp["lam"]])
            subln = lp["subln"].reshape(1, 2 * DA_HD).astype(_F32)
            q, k32, k16, v32, v16 = {}, {}, {}, {}, {}
            for n in xs:
                (q[n],) = _project(hs[n], wq, [_MXU], *tabs128_da[n])
                k32[n], k16[n] = _project(hs[n], wk, [_F32, _MXU], *tabs128[n])
                v32[n], v16[n] = _project(hs[n], wv, [_F32, _MXU])
                (z[n],) = _project(hs[n], wz, [_MXU])
            o["f"] = _diff_frames(q["f"], k16["f"], v16["f"], k16["m"], v16["m"], lams, subln, lam_init, nb, seq)
            o["m"] = _diff_small(q["m"], k16["m"], v16["m"], None, None, lams, subln, lam_init, 1, N_META)
            ck, cv = lp["cache"]
            o["s"] = _diff_small(q["s"], k16["s"], v16["s"], ck.reshape(db, past, DA_WIDTH),
                                 cv.reshape(db, past, DA_WIDTH), lams, subln, lam_init, db, ds)
            hshape = (DA_HEADS, 2 * DA_HD)

            def with_meta(f, mrows):
                mm = jnp.broadcast_to(mrows.reshape(1, N_META, *hshape), (nb, N_META, *hshape))
                return jnp.concatenate([mm, f.reshape(nb, seq, *hshape)], axis=1)

            p_st.append([with_meta(k32["f"], k32["m"]), with_meta(v32["f"], v32["m"])])
            s_st.append([k32["s"].reshape(db, ds, *hshape), v32["s"].reshape(db, ds, *hshape)])
        elif kind == 1:
            kvd = SW_KV * SW_HD
            wq = w_in[:, :SW_WIDTH]
            wk = w_in[:, SW_WIDTH:SW_WIDTH + kvd]
            wv = w_in[:, SW_WIDTH + kvd:SW_WIDTH + 2 * kvd]
            wz = w_in[:, SW_WIDTH + 2 * kvd:]
            wkd, wvd = _dup_pairs(wk, SW_HD), _dup_pairs(wv, SW_HD)
            sink_rows = {}
            q, kv32, kd, vd = {}, {}, {}, {}
            for n, rows in (("f", CHUNK), ("m", N_META), ("s", ds)):
                sk = lp["sinks"].astype(_F32).reshape(SW_KV, SW_REP, 1, 1)
                sink_rows[n] = jnp.broadcast_to(sk, (SW_KV, SW_REP, rows, LANES)).reshape(SW_KV, SW_REP * rows, LANES)
                (q[n],) = _project(hs[n], wq, [_MXU], *tabs64_sw[n])
                (kd[n],) = _project(hs[n], wkd, [_MXU], *tabs64[n])
                (vd[n],) = _project(hs[n], wvd, [_MXU])
                (z[n],) = _project(hs[n], wz, [_MXU])
            kmeta_blk = _pad_rows_front(kd["m"], WINDOW)
            vmeta_blk = _pad_rows_front(vd["m"], WINDOW)
            o["f"] = _swa_frames(q["f"], kd["f"], vd["f"], kmeta_blk, vmeta_blk, sink_rows["f"], nb, seq)
            o["m"] = _swa_small(q["m"], kd["m"], vd["m"], None, None, sink_rows["m"], 1, N_META)
            ck, cv = lp["cache"]
            ckd = _dup_pairs(ck.reshape(db, sw_rows, kvd), SW_HD).astype(_MXU)
            cvd = _dup_pairs(cv.reshape(db, sw_rows, kvd), SW_HD).astype(_MXU)
            o["s"] = _swa_small(q["s"], kd["s"], vd["s"], ckd, cvd, sink_rows["s"], db, ds)
            assert seq >= sw_rows and sw_rows % 8 == 0
            h_tail = hs["f"].reshape(nb, seq, d)[:, seq - sw_rows:].reshape(nb * sw_rows, d)
            tail_pos = N_META + seq - sw_rows + jnp.arange(sw_rows)
            tail_tabs, tail_shift = _rope_tables(tail_pos, 64, 1.0)
            (kt,) = _project(h_tail, _pad_cols(wk), [_F32], tail_tabs, tail_shift)
            (vt,) = _project(h_tail, _pad_cols(wv), [_F32])
            p_st.append([kt[:, :kvd].reshape(nb, sw_rows, SW_KV, SW_HD), vt[:, :kvd].reshape(nb, sw_rows, SW_KV, SW_HD)])
            (ks,) = _project(hs["s"], _pad_cols(wk), [_F32], *tabs64["s"])
            (vs_,) = _project(hs["s"], _pad_cols(wv), [_F32])
            ks = ks[:, :kvd].reshape(db, ds, SW_KV, SW_HD)
            vs_ = vs_[:, :kvd].reshape(db, ds, SW_KV, SW_HD)
            k_all = jnp.concatenate([ck, ks], axis=1)
            v_all = jnp.concatenate([cv, vs_], axis=1)
            s_st.append([k_all[:, k_all.shape[1] - sw_rows:], v_all[:, v_all.shape[1] - sw_rows:]])
        else:
            kvd = DS_KV * DS_HD
            c0 = 0
            wq = w_in[:, c0:c0 + DS_WIDTH]; c0 += DS_WIDTH
            wk = w_in[:, c0:c0 + kvd]; c0 += kvd
            wv = w_in[:, c0:c0 + kvd]; c0 += kvd
            wz = w_in[:, c0:c0 + DS_WIDTH]; c0 += DS_WIDTH
            wqi = w_in[:, c0:c0 + IDX_HEADS * IDX_HD]; c0 += IDX_HEADS * IDX_HD
            wki = w_in[:, c0:c0 + IDX_HD]; c0 += IDX_HD
            wwi = w_in[:, c0:c0 + IDX_HEADS]
            wkid = jnp.concatenate([wki, wki], axis=1)
            wwi_p = _pad_cols(wwi)
            wi_scale = IDX_HEADS ** -0.5 * IDX_HD ** -0.5
            q, k32, k16, v32, v16, qi, kid32, kid16, wi = {}, {}, {}, {}, {}, {}, {}, {}, {}
            for n in xs:
                (q[n],) = _project(hs[n], wq, [_MXU], *tabs128_ds[n])
                k32[n], k16[n] = _project(hs[n], wk, [_F32, _MXU], *tabs128[n])
                v32[n], v16[n] = _project(hs[n], wv, [_F32, _MXU])
                (z[n],) = _project(hs[n], wz, [_MXU])
                (qi[n],) = _project(hs[n], wqi, [_MXU], *tabs64[n])
                kid32[n], kid16[n] = _project(hs[n], wkid, [_F32, _MXU], *tabs64[n])
                (wi[n],) = _project(hs[n], wwi_p, [_F32], scale=wi_scale)
            topk_p = min(TOPK_MAX, seq // 4)
            topk_s = min(TOPK_MAX, (past + ds) // 4)
            km_blk = _pad_rows_back(k16["m"], 1, N_META, LANES)
            vm_blk = _pad_rows_back(v16["m"], 1, N_META, LANES)
            kidm_blk = _pad_rows_back(kid16["m"], 1, N_META, LANES)
            o["f"] = _dsa_frames(q["f"], k16["f"], v16["f"], qi["f"], kid16["f"], wi["f"],
                                 km_blk, vm_blk, kidm_blk, nb, seq, topk_p)
            o["m"] = _dsa_small(q["m"], km_blk, vm_blk, qi["m"], kidm_blk, wi["m"], None, None, None,
                                1, N_META, N_META, topk_p)
            ck, cv, cki = lp["cache"]
            ckid = jnp.concatenate([cki, cki], axis=-1).astype(_MXU)
            o["s"] = _dsa_small(q["s"], _pad_rows_back(k16["s"], db, ds, LANES), _pad_rows_back(v16["s"], db, ds, LANES),
                                qi["s"], _pad_rows_back(kid16["s"], db, ds, LANES), wi["s"],
                                ck.reshape(db, past, kvd), cv.reshape(db, past, kvd), ckid,
                                db, ds, ds, topk_s)

            def with_meta(f, mrows, *tail):
                mm = jnp.broadcast_to(mrows.reshape(1, N_META, *tail), (nb, N_META, *tail))
                return jnp.concatenate([mm, f.reshape(nb, seq, *tail)], axis=1)

            p_st.append([with_meta(k32["f"], k32["m"], DS_KV, DS_HD), with_meta(v32["f"], v32["m"], DS_KV, DS_HD),
                         with_meta(kid32["f"][:, :IDX_HD], kid32["m"][:, :IDX_HD], IDX_HD)])
            s_st.append([k32["s"].reshape(db, ds, DS_KV, DS_HD), v32["s"].reshape(db, ds, DS_KV, DS_HD),
                         kid32["s"][:, :IDX_HD].reshape(db, ds, IDX_HD)])
        last = li == depth - 1
        for n in xs:
            res = _out_project(o[n], z[n], w_out, xs[n], norms[li], last)
            if last:
                (ys[n],) = res
            else:
                xs[n], hs[n] = res

    y_prompt = ys["f"].reshape(nb, seq, d)
    y_sample = ys["s"].reshape(db, ds, d)
    return (y_prompt, y_sample,
            p_st[0][0], p_st[0][1], s_st[0][0], s_st[0][1],
            p_st[1][0], p_st[1][1], s_st[1][0], s_st[1][1],
            p_st[2][0], p_st[2][1], p_st[2][2], s_st[2][0], s_st[2][1], s_st[2][2],
            p_st[3][0], p_st[3][1], s_st[3][0], s_st[3][1])


def _pad_cols(w):
    n = w.shape[1]
    return jnp.pad(w, ((0, 0), (0, (-n) % LANES)))
```

```python
import functools
import math

import jax
import jax.numpy as jnp
from jax import lax
from jax.experimental import pallas as pl
from jax.experimental.pallas import tpu as pltpu

CHUNK = 64
N_META = 16
ROPE_THETA = 500000.0
NORM_EPS = 1e-6
NEG_INF = -1e30
DA_HEADS = 8
DA_HD = 128
DA_WIDTH = DA_HEADS * 2 * DA_HD
SW_HEADS = 32
SW_KV = 4
SW_HD = 64
SW_WIDTH = SW_HEADS * SW_HD
SW_REP = SW_HEADS // SW_KV
WINDOW = 128
DS_HEADS = 16
DS_KV = 4
DS_HD = 128
DS_WIDTH = DS_HEADS * DS_HD
DS_REP = DS_HEADS // DS_KV
IDX_HEADS = 16
IDX_HD = 64
TOPK_MAX = 256
N_MIXERS = 3

LANES = 128
INT_MIN = -(2 ** 31)
LOG2E = math.log2(math.e)
VMEM_LIMIT = 56 * 1024 * 1024

_MXU = jnp.bfloat16
_F32 = jnp.float32

_NT = (((1,), (1,)), ((), ()))


def _cparams(sem):
    return pltpu.CompilerParams(dimension_semantics=sem, vmem_limit_bytes=VMEM_LIMIT)


def _row_tile(m, pref):
    t = pref
    while t > 8 and m % t:
        t //= 2
    assert m % t == 0, (m, pref)
    return t


def _dot(a, b):
    return jnp.dot(a, b, preferred_element_type=_F32)


def _dot_nt(a, b):
    return lax.dot_general(a, b, _NT, preferred_element_type=_F32)


def _norm_kernel(x_ref, g_ref, h_ref):
    x = x_ref[...]
    y = x * lax.rsqrt(jnp.mean(x * x, axis=-1, keepdims=True) + NORM_EPS)
    h_ref[...] = (y * g_ref[...]).astype(h_ref.dtype)


def _rms_norm_rows(x, g):
    m, d = x.shape
    out_dtype = _MXU
    tm = _row_tile(m, 512)
    return pl.pallas_call(
        _norm_kernel,
        grid=(m // tm,),
        in_specs=[pl.BlockSpec((tm, d), lambda i: (i, 0)),
                  pl.BlockSpec((1, d), lambda i: (0, 0))],
        out_specs=pl.BlockSpec((tm, d), lambda i: (i, 0)),
        out_shape=jax.ShapeDtypeStruct((m, d), out_dtype),
        compiler_params=_cparams(("parallel",)),
        name="rms_norm",
    )(x, g.reshape(1, d).astype(_F32))


def _proj_kernel(*refs, rope_shift, scale, n_out, tn, transposed_out):
    h_ref, w_ref = refs[0], refs[1]
    pos = 2
    if rope_shift is not None:
        c_ref, a_ref, b_ref = refs[2:5]
        pos = 5
    outs = refs[pos:pos + n_out]
    ot_ref = refs[pos + n_out] if transposed_out else None
    h = h_ref[...]
    cw = 2 * LANES if tn % (2 * LANES) == 0 else LANES
    for c in range(tn // cw):
        wide = _dot(h, w_ref[:, c * cw:(c + 1) * cw])
        for sub in range(cw // LANES):
            acc = wide[:, sub * LANES:(sub + 1) * LANES]
            if rope_shift is not None:
                acc = (acc * c_ref[...]
                       + pltpu.roll(acc, rope_shift, 1) * a_ref[...]
                       + pltpu.roll(acc, LANES - rope_shift, 1) * b_ref[...])
            elif scale is not None:
                acc = acc * scale
            cols = slice(c * cw + sub * LANES, c * cw + (sub + 1) * LANES)
            for o in outs:
                o[:, cols] = acc.astype(o.dtype)
            if ot_ref is not None:
                ot_ref[0, cols, :] = acc.T.astype(ot_ref.dtype)


def _project(h, w, out_dtypes, tabs=None, rope_shift=None, scale=None, transposed_out=False):
    m, d = h.shape
    n = w.shape[1]
    tm = _row_tile(m if tabs is None else math.gcd(m, tabs[0].shape[0]), 512)
    tn = n if n <= 1024 else 1024
    assert n % tn == 0 and tn % LANES == 0
    in_specs = [pl.BlockSpec((tm, d), lambda j, i: (i, 0)),
                pl.BlockSpec((d, tn), lambda j, i: (0, j))]
    args = [h, w]
    if tabs is not None:
        trows = tabs[0].shape[0]
        assert trows % tm == 0
        nrep = trows // tm
        for t in tabs:
            in_specs.append(pl.BlockSpec((tm, LANES), lambda j, i: (i % nrep, 0)))
            args.append(t)
    else:
        rope_shift = None
    out_specs = [pl.BlockSpec((tm, tn), lambda j, i: (i, j)) for _ in out_dtypes]
    out_shape = [jax.ShapeDtypeStruct((m, n), dt) for dt in out_dtypes]
    if transposed_out:
        out_specs.append(pl.BlockSpec((1, tn, tm), lambda j, i: (i, j, 0)))
        out_shape.append(jax.ShapeDtypeStruct((m // tm, n, tm), _MXU))
    outs = pl.pallas_call(
        functools.partial(_proj_kernel, rope_shift=rope_shift, scale=scale,
                          n_out=len(out_dtypes), tn=tn, transposed_out=transposed_out),
        grid=(n // tn, m // tm),
        in_specs=in_specs,
        out_specs=out_specs,
        out_shape=out_shape,
        compiler_params=_cparams(("parallel", "parallel")),
        name="in_proj",
    )(*args)
    return outs


def _outproj_kernel(o_ref, z_ref, w_ref, x_ref, g_ref, *outs, emit_x):
    z = z_ref[...].astype(_F32)
    a = o_ref[...].astype(_F32) * (z / (1.0 + jnp.exp(-z)))
    y = x_ref[...] + _dot(a.astype(w_ref.dtype), w_ref[...])
    if emit_x:
        outs[0][...] = y
    n = y * lax.rsqrt(jnp.mean(y * y, axis=-1, keepdims=True) + NORM_EPS) * g_ref[...]
    outs[-1][...] = n.astype(outs[-1].dtype)


def _out_project(o, z, w, x, g_next, last):
    m, d = x.shape
    wd = o.shape[1]
    tm = _row_tile(m, 256)
    row = lambda i: (i, 0)
    fixed = lambda i: (0, 0)
    if last:
        out_shape = [jax.ShapeDtypeStruct((m, d), _F32)]
    else:
        out_shape = [jax.ShapeDtypeStruct((m, d), _F32), jax.ShapeDtypeStruct((m, d), _MXU)]
    return pl.pallas_call(
        functools.partial(_outproj_kernel, emit_x=not last),
        grid=(m // tm,),
        in_specs=[pl.BlockSpec((tm, wd), row), pl.BlockSpec((tm, wd), row),
                  pl.BlockSpec((wd, d), fixed), pl.BlockSpec((tm, d), row),
                  pl.BlockSpec((1, d), fixed)],
        out_specs=[pl.BlockSpec((tm, d), row) for _ in out_shape],
        out_shape=out_shape,
        compiler_params=_cparams(("parallel",)),
        name="out_proj",
    )(o, z, w, x, g_next.reshape(1, d).astype(_F32))


def _rope_tables(pos, hd, scale=1.0):
    rd = hd // 4
    half = rd // 2
    lane = jnp.arange(LANES)
    d = lane % hd
    f = d % half
    inv_freq = ROPE_THETA ** (-(jnp.arange(half, dtype=_F32) * 2.0 / rd))
    ang = pos.astype(_F32)[:, None] * inv_freq[f][None, :]
    cos, sin = jnp.cos(ang), jnp.sin(ang)
    in_rot = (d < rd)[None, :]
    c = jnp.where(in_rot, cos, 1.0)
    a = jnp.where(((d >= half) & (d < rd))[None, :], sin, 0.0)
    b = jnp.where((d < half)[None, :], -sin, 0.0)
    s = jnp.float32(scale)
    return (c * s, a * s, b * s), half


def _lam_value(lam_ref, lam_init):
    l = lam_ref[...]
    s1 = jnp.sum(l[0:1] * l[1:2], axis=-1, keepdims=True)
    s2 = jnp.sum(l[2:3] * l[3:4], axis=-1, keepdims=True)
    return jnp.exp(s1) - jnp.exp(s2) + lam_init


def _diff_finish(acc1, l1, acc2, l2, lam, sub, lam_init):
    o = acc1 / l1 - lam * (acc2 / l2)
    o = o * lax.rsqrt(jnp.mean(o * o, axis=-1, keepdims=True) + NORM_EPS) * sub
    return o * (1.0 - lam_init)


def _diff_frames_kernel(q_ref, k_ref, vt_ref, km_ref, vmt_ref, lam_ref, subc_ref, o_ref,
                        acc_s, *, tq, lam_init):
    i = pl.program_id(2)
    q = q_ref[...]
    qs = (q[:, :DA_HD], q[:, DA_HD:])

    km = km_ref[...]
    vmt = vmt_ref[...]
    ms, ls = [], []
    for j in range(2):
        s = _dot_nt(km[:, j * DA_HD:(j + 1) * DA_HD], qs[j])
        m = jnp.max(s, axis=0, keepdims=True)
        p = jnp.exp2(s - m)
        ms.append(m)
        ls.append(jnp.sum(p, axis=0, keepdims=True))
        acc_s[j] = _dot(vmt, p.astype(vmt.dtype))

    def step(t, carry, masked):
        start = pl.multiple_of(t * tq, tq)
        vt = vt_ref[t]
        if masked:
            kc = lax.broadcasted_iota(jnp.int32, (tq, tq), 0) // CHUNK
            qc = lax.broadcasted_iota(jnp.int32, (tq, tq), 1) // CHUNK
            ok = kc <= qc
        out = []
        ss = [_dot_nt(k_ref[pl.ds(start, tq), j * DA_HD:(j + 1) * DA_HD], qs[j]) for j in range(2)]
        for j in range(2):
            m_prev, l_prev = carry[2 * j], carry[2 * j + 1]
            s = ss[j]
            if masked:
                s = jnp.where(ok, s, NEG_INF)
            m_new = jnp.maximum(m_prev, jnp.max(s, axis=0, keepdims=True))
            alpha = jnp.exp2(m_prev - m_new)
            p = jnp.exp2(s - m_new)
            l_new = alpha * l_prev + jnp.sum(p, axis=0, keepdims=True)
            acc_s[j] = alpha * acc_s[j] + _dot(vt, p.astype(vt.dtype))
            out += [m_new, l_new]
        return tuple(out)

    carry = (ms[0], ls[0], ms[1], ls[1])
    carry = lax.fori_loop(0, i, functools.partial(step, masked=False), carry)
    m1, l1, m2, l2 = step(i, carry, True)

    lam = _lam_value(lam_ref, lam_init)
    ot = acc_s[0] / l1 - lam * (acc_s[1] / l2)
    sub = jnp.concatenate([subc_ref[...]] * (tq // LANES), axis=1)
    ot = ot * lax.rsqrt(jnp.mean(ot * ot, axis=0, keepdims=True) + NORM_EPS) * sub * (1.0 - lam_init)
    o_ref[...] = ot.T.astype(o_ref.dtype)


def _diff_frames(q, k, vt, km, vmt, lams, subln_col, lam_init, nb, seq):
    m = q.shape[0]
    hw = 2 * DA_HD
    tq = vt.shape[2]
    assert seq % tq == 0 and tq % LANES == 0
    nt = seq // tq
    return pl.pallas_call(
        functools.partial(_diff_frames_kernel, tq=tq, lam_init=lam_init),
        grid=(nb, DA_HEADS, nt),
        in_specs=[pl.BlockSpec((tq, hw), lambda b, h, i: (b * nt + i, h)),
                  pl.BlockSpec((seq, hw), lambda b, h, i: (b, h)),
                  pl.BlockSpec((nt, hw, tq), lambda b, h, i: (b, h, 0)),
                  pl.BlockSpec((N_META, hw), lambda b, h, i: (0, h)),
                  pl.BlockSpec((hw, N_META), lambda b, h, i: (h, 0)),
                  pl.BlockSpec((4, DA_HD), lambda b, h, i: (0, 0)),
                  pl.BlockSpec((hw, LANES), lambda b, h, i: (0, 0))],
        out_specs=pl.BlockSpec((tq, hw), lambda b, h, i: (b * nt + i, h)),
        out_shape=jax.ShapeDtypeStruct((m, DA_WIDTH), _MXU),
        scratch_shapes=[pltpu.VMEM((2, hw, tq), _F32)],
        compiler_params=_cparams(("parallel", "parallel", "arbitrary")),
        name="diff_frames",
    )(q, k, vt, km, vmt, lams, subln_col)


def _diff_small_kernel(*refs, has_cache, lam_init):
    if has_cache:
        q_ref, k_ref, v_ref, ck_ref, cv_ref, lam_ref, sub_ref, o_ref = refs
    else:
        q_ref, k_ref, v_ref, lam_ref, sub_ref, o_ref = refs
    q = q_ref[...]
    kn = k_ref[...]
    vn = v_ref[...]
    parts = [(kn, vn)]
    if has_cache:
        parts.append((ck_ref[0].astype(kn.dtype), cv_ref[0].astype(vn.dtype)))
    accs, ls = [], []
    for j in range(2):
        qj = q[:, j * DA_HD:(j + 1) * DA_HD]
        ss = [_dot_nt(qj, kp[:, j * DA_HD:(j + 1) * DA_HD]) for kp, _ in parts]
        m = ss[0].max(axis=-1, keepdims=True)
        for s in ss[1:]:
            m = jnp.maximum(m, s.max(axis=-1, keepdims=True))
        l = 0.0
        acc = 0.0
        for s, (_, vp) in zip(ss, parts):
            p = jnp.exp2(s - m)
            l = l + jnp.sum(p, axis=-1, keepdims=True)
            acc = acc + _dot(p.astype(vp.dtype), vp)
        accs.append(acc)
        ls.append(l)
    lam = _lam_value(lam_ref, lam_init)
    o = _diff_finish(accs[0], ls[0], accs[1], ls[1], lam, sub_ref[...], lam_init)
    o_ref[...] = o.astype(o_ref.dtype)


def _diff_small(q, k, v, cache_k, cache_v, lams, subln, lam_init, nb, nq):
    hw = 2 * DA_HD
    has_cache = cache_k is not None
    qspec = pl.BlockSpec((nq, hw), lambda b, h: (b, h))
    in_specs = [qspec, qspec, qspec]
    args = [q, k, v]
    if has_cache:
        past = cache_k.shape[1]
        cspec = pl.BlockSpec((1, past, hw), lambda b, h: (b, 0, h))
        in_specs += [cspec, cspec]
        args += [cache_k, cache_v]
    in_specs += [pl.BlockSpec((4, DA_HD), lambda b, h: (0, 0)),
                 pl.BlockSpec((1, hw), lambda b, h: (0, 0))]
    args += [lams, subln]
    return pl.pallas_call(
        functools.partial(_diff_small_kernel, has_cache=has_cache, lam_init=lam_init),
        grid=(nb, DA_HEADS),
        in_specs=in_specs,
        out_specs=qspec,
        out_shape=jax.ShapeDtypeStruct((nb * nq, DA_WIDTH), _MXU),
        compiler_params=_cparams(("parallel", "parallel")),
        name="diff_small",
    )(*args)


def _stack_heads(qg, nheads):
    n = qg.shape[0]
    lane = lax.broadcasted_iota(jnp.int32, (n, LANES), 1)
    lo = lane < SW_HD
    pieces = []
    for r in range(nheads):
        blk = qg[:, (r // 2) * LANES:(r // 2 + 1) * LANES]
        keep = lo if r % 2 == 0 else jnp.logical_not(lo)
        pieces.append(jnp.where(keep, blk, jnp.zeros_like(blk)))
    return jnp.concatenate(pieces, axis=0)


def _unstack_heads(o, n, nheads):
    lane = lax.broadcasted_iota(jnp.int32, (n, LANES), 1)
    lo = lane < SW_HD
    return [jnp.where(lo, o[(2 * p) * n:(2 * p + 1) * n], o[(2 * p + 1) * n:(2 * p + 2) * n])
            for p in range(nheads // 2)]


def _sink_softmax_pv(ss, vs, sk):
    m = sk
    for s in ss:
        m = jnp.maximum(m, jnp.max(s, axis=-1, keepdims=True))
    den = jnp.exp2(sk - m)
    acc = 0.0
    for s, v in zip(ss, vs):
        e = jnp.exp2(s - m)
        den = den + jnp.sum(e, axis=-1, keepdims=True)
        acc = acc + _dot(e.astype(v.dtype), v)
    return acc / den


def _swa_frames_kernel(q_ref, kc_ref, vc_ref, kp_ref, vp_ref, km_ref, vm_ref, sk_ref, o_ref,
                       kcat, vcat, *, tq):
    i = pl.program_id(1)
    first = i == 0
    prev_rows = WINDOW
    kcat[0:prev_rows] = jnp.where(first, km_ref[...], kp_ref[...])
    vcat[0:prev_rows] = jnp.where(first, vm_ref[...], vp_ref[...])
    kcat[prev_rows:prev_rows + tq] = kc_ref[...]
    vcat[prev_rows:prev_rows + tq] = vc_ref[...]
    nwin = WINDOW + CHUNK
    for c in range(tq // CHUNK):
        rows = slice(c * CHUNK, (c + 1) * CHUNK)
        win = slice(c * CHUNK, c * CHUNK + nwin)
        for g in range(SW_KV):
            gl = slice(g * LANES, (g + 1) * LANES)
            qst = _stack_heads(q_ref[rows, g * SW_REP * SW_HD:(g + 1) * SW_REP * SW_HD], SW_REP)
            s = _dot_nt(qst, kcat[win, gl])
            if c * CHUNK < prev_rows - N_META:
                col = lax.broadcasted_iota(jnp.int32, s.shape, 1) + c * CHUNK
                ok = jnp.logical_or(col >= prev_rows - N_META, jnp.logical_not(first))
                s = jnp.where(ok, s, NEG_INF)
            o = _sink_softmax_pv([s], [vcat[win, gl]], sk_ref[g][:, 0:1])
            for p, blk in enumerate(_unstack_heads(o, CHUNK, SW_REP)):
                lo = g * SW_REP * SW_HD + p * LANES
                o_ref[rows, lo:lo + LANES] = blk.astype(o_ref.dtype)


def _swa_frames(q, kd, vd, kmeta_blk, vmeta_blk, sink_rows, nb, seq):
    m = q.shape[0]
    tq = _row_tile(seq, 512)
    nt = seq // tq
    pw = SW_KV * LANES
    per = tq // WINDOW
    cur = lambda b, i: (b * nt + i, 0)
    prev = lambda b, i: (jnp.maximum((b * nt + i) * per - 1, 0), 0)
    fixed = lambda b, i: (0, 0)
    return pl.pallas_call(
        functools.partial(_swa_frames_kernel, tq=tq),
        grid=(nb, nt),
        in_specs=[pl.BlockSpec((tq, SW_WIDTH), cur),
                  pl.BlockSpec((tq, pw), cur), pl.BlockSpec((tq, pw), cur),
                  pl.BlockSpec((WINDOW, pw), prev), pl.BlockSpec((WINDOW, pw), prev),
                  pl.BlockSpec((WINDOW, pw), fixed), pl.BlockSpec((WINDOW, pw), fixed),
                  pl.BlockSpec((SW_KV, SW_REP * CHUNK, LANES), lambda b, i: (0, 0, 0))],
        out_specs=pl.BlockSpec((tq, SW_WIDTH), cur),
        out_shape=jax.ShapeDtypeStruct((m, SW_WIDTH), _MXU),
        scratch_shapes=[pltpu.VMEM((WINDOW + tq, pw), _MXU), pltpu.VMEM((WINDOW + tq, pw), _MXU)],
        compiler_params=_cparams(("parallel", "arbitrary")),
        name="swa_frames",
    )(q, kd, vd, kd, vd, kmeta_blk, vmeta_blk, sink_rows)


def _swa_small_kernel(*refs, has_cache, nq):
    if has_cache:
        q_ref, k_ref, v_ref, ck_ref, cv_ref, sk_ref, o_ref = refs
    else:
        q_ref, k_ref, v_ref, sk_ref, o_ref = refs
    for g in range(SW_KV):
        gl = slice(g * LANES, (g + 1) * LANES)
        qst = _stack_heads(q_ref[:, g * SW_REP * SW_HD:(g + 1) * SW_REP * SW_HD], SW_REP)
        ks, vs = [k_ref[:, gl]], [v_ref[:, gl]]
        if has_cache:
            ks.append(ck_ref[0][:, gl])
            vs.append(cv_ref[0][:, gl])
        ss = [_dot_nt(qst, kk) for kk in ks]
        o = _sink_softmax_pv(ss, vs, sk_ref[g][:, 0:1])
        for p, blk in enumerate(_unstack_heads(o, nq, SW_REP)):
            lo = g * SW_REP * SW_HD + p * LANES
            o_ref[:, lo:lo + LANES] = blk.astype(o_ref.dtype)


def _swa_small(q, kd, vd, cache_kd, cache_vd, sink_rows, nb, nq):
    pw = SW_KV * LANES
    has_cache = cache_kd is not None
    row = lambda b: (b, 0)
    in_specs = [pl.BlockSpec((nq, SW_WIDTH), row), pl.BlockSpec((nq, pw), row), pl.BlockSpec((nq, pw), row)]
    args = [q, kd, vd]
    if has_cache:
        past = cache_kd.shape[1]
        cspec = pl.BlockSpec((1, past, pw), lambda b: (b, 0, 0))
        in_specs += [cspec, cspec]
        args += [cache_kd, cache_vd]
    in_specs.append(pl.BlockSpec((SW_KV, SW_REP * nq, LANES), lambda b: (0, 0, 0)))
    args.append(sink_rows)
    return pl.pallas_call(
        functools.partial(_swa_small_kernel, has_cache=has_cache, nq=nq),
        grid=(nb,),
        in_specs=in_specs,
        out_specs=pl.BlockSpec((nq, SW_WIDTH), row),
        out_shape=jax.ShapeDtypeStruct((nb * nq, SW_WIDTH), _MXU),
        compiler_params=_cparams(("parallel",)),
        name="swa_small",
    )(*args)


def _order_key(score, valid):
    bits = lax.bitcast_convert_type(score + 0.0, jnp.int32)
    key = jnp.where(bits < 0, bits ^ jnp.int32(0x7FFFFFFF), bits)
    return jnp.where(valid, key, jnp.int32(INT_MIN))


def _stack_idx_heads(qi):
    return _stack_heads(qi, IDX_HEADS)


def _index_scores(qst, wb_of, kid, n):
    score = None
    hb = 4
    for h0 in range(0, IDX_HEADS, hb):
        sc = _dot_nt(qst[h0 * n:(h0 + hb) * n], kid)
        for h in range(h0, h0 + hb):
            part = jnp.maximum(sc[(h - h0) * n:(h - h0 + 1) * n], 0.0) * wb_of(h)
            score = part if score is None else score + part
    return score


def _index_scores_t(kid, qst, wit, n):
    score = None
    hb = 4
    for h0 in range(0, IDX_HEADS, hb):
        sc = _dot_nt(kid, qst[h0 * n:(h0 + hb) * n])
        for h in range(h0, h0 + hb):
            part = jnp.maximum(sc[:, (h - h0) * n:(h - h0 + 1) * n], 0.0) * wit[h:h + 1, :]
            score = part if score is None else score + part
    return score


def _dsa_frames_kernel(q_ref, k_ref, vt_ref, qi_ref, kid_ref, wi_ref, km_ref, vmt_ref, kidm_ref, o_ref,
                       keym_s, key_s, biasm_s, bias_s, qst_s, wit_s, qg_s, m_s, l_s, acc_s, *, tq, tk, topk):
    i = pl.program_id(1)
    n_t = (i * tq + tq + tk - 1) // tk
    qst_s[...] = _stack_idx_heads(qi_ref[...])
    wit_s[...] = wi_ref[...].T
    sc_m = _index_scores_t(kidm_ref[...], qst_s[...], wit_s[...], tq)
    row_m = lax.broadcasted_iota(jnp.int32, (LANES, tq), 0)
    keym_s[...] = _order_key(sc_m, row_m < N_META)

    q_chunk = (lax.broadcasted_iota(jnp.int32, (tk, tq), 1) + i * tq) // CHUNK

    def score_body(t, carry):
        start = pl.multiple_of(t * tk, tk)
        sc = _index_scores_t(kid_ref[pl.ds(start, tk), :], qst_s[...], wit_s[...], tq)
        k_chunk = (lax.broadcasted_iota(jnp.int32, (tk, tq), 0) + t * tk) // CHUNK
        key_s[t] = _order_key(sc, k_chunk <= q_chunk)
        return carry

    lax.fori_loop(0, n_t, score_body, 0)

    def count_ge(thr):
        def fold(x):
            return jnp.sum(x.reshape(x.shape[0] // 64, 64, tq), axis=0)

        part = fold(jnp.where(keym_s[...] >= thr, 1.0, 0.0))

        def cbody(t, acc):
            return acc + fold(jnp.where(key_s[t] >= thr, 1.0, 0.0))

        part = lax.fori_loop(0, n_t, cbody, part)
        return jnp.sum(part, axis=0, keepdims=True)

    def bit_body(it, u):
        bit = jnp.int32(31) - it
        cand = u | lax.shift_left(jnp.int32(1), bit)
        cnt = count_ge(cand ^ jnp.int32(INT_MIN))
        return jnp.where(cnt >= float(topk), cand, u)

    u = lax.fori_loop(0, 32, bit_body, jnp.zeros((1, tq), jnp.int32))
    thr = jnp.maximum(u ^ jnp.int32(INT_MIN), jnp.int32(INT_MIN + 1))

    biasm_s[...] = jnp.where(keym_s[...] >= thr, 0.0, NEG_INF)

    def bias_body(t, carry):
        bias_s[t] = jnp.where(key_s[t] >= thr, 0.0, NEG_INF)
        return carry

    lax.fori_loop(0, n_t, bias_body, 0)

    groups = [slice(g * DS_HD, (g + 1) * DS_HD) for g in range(DS_KV)]
    for g, gl in enumerate(groups):
        qg_s[g] = jnp.concatenate(
            [q_ref[:, (g * DS_REP + r) * DS_HD:(g * DS_REP + r + 1) * DS_HD] for r in range(DS_REP)], axis=0)
        s = _dot_nt(km_ref[:, gl], qg_s[g]) + jnp.concatenate([biasm_s[...]] * DS_REP, axis=1)
        m0 = jnp.max(s, axis=0, keepdims=True)
        p = jnp.exp2(s - m0)
        m_s[g] = m0
        l_s[g] = jnp.sum(p, axis=0, keepdims=True)
        acc_s[g] = _dot(vmt_ref[gl, :], p.astype(_MXU))

    def att_body(t, carry):
        start = pl.multiple_of(t * tk, tk)
        bias = jnp.concatenate([bias_s[t]] * DS_REP, axis=1)
        ss = [_dot_nt(k_ref[pl.ds(start, tk), gl], qg_s[g]) for g, gl in enumerate(groups)]
        for g, gl in enumerate(groups):
            s = ss[g] + bias
            m_prev = m_s[g]
            m_new = jnp.maximum(m_prev, jnp.max(s, axis=0, keepdims=True))
            alpha = jnp.exp2(m_prev - m_new)
            p = jnp.exp2(s - m_new)
            l_s[g] = alpha * l_s[g] + jnp.sum(p, axis=0, keepdims=True)
            acc_s[g] = alpha * acc_s[g] + _dot(vt_ref[t, gl, :], p.astype(_MXU))
            m_s[g] = m_new
        return carry

    lax.fori_loop(0, n_t, att_body, 0)
    for g in range(DS_KV):
        ot = acc_s[g] / l_s[g]
        for r in range(DS_REP):
            lo = (g * DS_REP + r) * DS_HD
            o_ref[:, lo:lo + DS_HD] = ot[:, r * tq:(r + 1) * tq].T.astype(o_ref.dtype)


def _dsa_frames(q, k, vt, qi, kid, wi, km_blk, vmt_blk, kidm_blk, nb, seq, topk):
    m = q.shape[0]
    tq = _row_tile(seq, 128)
    tk = vt.shape[2]
    assert seq % tk == 0 and tq == LANES
    nt = seq // tq
    nkt = seq // tk
    kvw = DS_KV * DS_HD
    cur = lambda b, i: (b * nt + i, 0)
    per_b = lambda b, i: (b, 0)
    fixed = lambda b, i: (0, 0)
    return pl.pallas_call(
        functools.partial(_dsa_frames_kernel, tq=tq, tk=tk, topk=topk),
        grid=(nb, nt),
        in_specs=[pl.BlockSpec((tq, DS_WIDTH), cur),
                  pl.BlockSpec((seq, kvw), per_b),
                  pl.BlockSpec((nkt, kvw, tk), lambda b, i: (b, 0, 0)),
                  pl.BlockSpec((tq, IDX_HEADS * IDX_HD), cur),
                  pl.BlockSpec((seq, LANES), per_b),
                  pl.BlockSpec((tq, LANES), cur),
                  pl.BlockSpec((LANES, kvw), fixed), pl.BlockSpec((kvw, LANES), fixed),
                  pl.BlockSpec((LANES, LANES), fixed)],
        out_specs=pl.BlockSpec((tq, DS_WIDTH), cur),
        out_shape=jax.ShapeDtypeStruct((m, DS_WIDTH), _MXU),
        scratch_shapes=[pltpu.VMEM((LANES, tq), jnp.int32), pltpu.VMEM((nkt, tk, tq), jnp.int32),
                        pltpu.VMEM((LANES, tq), _F32), pltpu.VMEM((nkt, tk, tq), _F32),
                        pltpu.VMEM((IDX_HEADS * tq, LANES), _MXU),
                        pltpu.VMEM((LANES, tq), _F32),
                        pltpu.VMEM((DS_KV, DS_REP * tq, DS_HD), _MXU),
                        pltpu.VMEM((DS_KV, 1, DS_REP * tq), _F32), pltpu.VMEM((DS_KV, 1, DS_REP * tq), _F32),
                        pltpu.VMEM((DS_KV, DS_HD, DS_REP * tq), _F32)],
        compiler_params=_cparams(("parallel", "arbitrary")),
        name="dsa_frames",
    )(q, k, vt, qi, kid, wi, km_blk, vmt_blk, kidm_blk)


def _dsa_small_kernel(*refs, has_cache, nq, n_new, topk):
    if has_cache:
        q_ref, kn_ref, vn_ref, qi_ref, kidn_ref, wi_ref, ck_ref, cv_ref, ckid_ref, o_ref = refs
    else:
        q_ref, kn_ref, vn_ref, qi_ref, kidn_ref, wi_ref, o_ref = refs
    qst = _stack_idx_heads(qi_ref[...])
    wi = wi_ref[...]
    wbs = [jnp.broadcast_to(wi[:, h:h + 1], (nq, LANES)) for h in range(IDX_HEADS)]

    def wb_tile(width):
        return lambda h: jnp.concatenate([wbs[h]] * (width // LANES), axis=1)

    col_n = lax.broadcasted_iota(jnp.int32, (nq, LANES), 1)
    keys = [_order_key(_index_scores(qst, wb_tile(LANES), kidn_ref[...], nq), col_n < n_new)]
    if has_cache:
        past = ckid_ref.shape[1]
        sc = _index_scores(qst, wb_tile(past), ckid_ref[0], nq)
        keys.append(_order_key(sc, jnp.full(sc.shape, True)))

    def count_ge(thr):
        cnt = 0.0
        for kk in keys:
            cnt = cnt + jnp.sum(jnp.where(kk >= thr, 1.0, 0.0), axis=-1, keepdims=True)
        return cnt

    def bit_body(it, u):
        bit = jnp.int32(31) - it
        cand = u | lax.shift_left(jnp.int32(1), bit)
        cnt = count_ge(cand ^ jnp.int32(INT_MIN))
        return jnp.where(cnt >= float(topk), cand, u)

    u = lax.fori_loop(0, 32, bit_body, jnp.zeros((nq, 1), jnp.int32))
    thr = jnp.maximum(u ^ jnp.int32(INT_MIN), jnp.int32(INT_MIN + 1))
    biases = [jnp.where(kk >= thr, 0.0, NEG_INF) for kk in keys]

    for g in range(DS_KV):
        gl = slice(g * DS_HD, (g + 1) * DS_HD)
        qg = jnp.concatenate(
            [q_ref[:, (g * DS_REP + r) * DS_HD:(g * DS_REP + r + 1) * DS_HD] for r in range(DS_REP)], axis=0)
        ks, vs = [kn_ref[:, gl]], [vn_ref[:, gl]]
        if has_cache:
            ks.append(ck_ref[0][:, gl].astype(_MXU))
            vs.append(cv_ref[0][:, gl].astype(_MXU))
        ss = [_dot_nt(qg, kk) + jnp.concatenate([bb] * DS_REP, axis=0) for kk, bb in zip(ks, biases)]
        m = ss[0].max(axis=-1, keepdims=True)
        for s in ss[1:]:
            m = jnp.maximum(m, s.max(axis=-1, keepdims=True))
        l = 0.0
        acc = 0.0
        for s, vv in zip(ss, vs):
            p = jnp.exp2(s - m)
            l = l + jnp.sum(p, axis=-1, keepdims=True)
            acc = acc + _dot(p.astype(_MXU), vv)
        og = acc / l
        for r in range(DS_REP):
            lo = (g * DS_REP + r) * DS_HD
            o_ref[:, lo:lo + DS_HD] = og[r * nq:(r + 1) * nq].astype(o_ref.dtype)


def _dsa_small(q, kn_blk, vn_blk, qi, kidn_blk, wi, cache_k, cache_v, cache_kid, nb, nq, n_new, topk):
    kvw = DS_KV * DS_HD
    has_cache = cache_k is not None
    row = lambda b: (b, 0)
    in_specs = [pl.BlockSpec((nq, DS_WIDTH), row),
                pl.BlockSpec((LANES, kvw), row), pl.BlockSpec((LANES, kvw), row),
                pl.BlockSpec((nq, IDX_HEADS * IDX_HD), row),
                pl.BlockSpec((LANES, LANES), row),
                pl.BlockSpec((nq, LANES), row)]
    args = [q, kn_blk, vn_blk, qi, kidn_blk, wi]
    if has_cache:
        past = cache_k.shape[1]
        in_specs += [pl.BlockSpec((1, past, kvw), lambda b: (b, 0, 0)),
                     pl.BlockSpec((1, past, kvw), lambda b: (b, 0, 0)),
                     pl.BlockSpec((1, past, LANES), lambda b: (b, 0, 0))]
        args += [cache_k, cache_v, cache_kid]
    return pl.pallas_call(
        functools.partial(_dsa_small_kernel, has_cache=has_cache, nq=nq, n_new=n_new, topk=topk),
        grid=(nb,),
        in_specs=in_specs,
        out_specs=pl.BlockSpec((nq, DS_WIDTH), row),
        out_shape=jax.ShapeDtypeStruct((nb * nq, DS_WIDTH), _MXU),
        compiler_params=_cparams(("parallel",)),
        name="dsa_small",
    )(*args)


def _pad_rows_front(a, rows):
    return jnp.pad(a, ((rows - a.shape[0], 0), (0, 0)))


def _pad_rows_back(a, nb, nq, rows):
    w = a.shape[1]
    return jnp.pad(a.reshape(nb, nq, w), ((0, 0), (0, rows - nq), (0, 0))).reshape(nb * rows, w)


def _dup_pairs(a, hd):
    lead = a.shape[:-1]
    g = a.shape[-1] // hd
    a = a.reshape(*lead, g, 1, hd)
    return jnp.broadcast_to(a, (*lead, g, 2, hd)).reshape(*lead, g * 2 * hd)


def kernel(x_prompt, x_sample, cache_l0_k, cache_l0_v, cache_l1_k, cache_l1_v, cache_l2_k, cache_l2_v, cache_l2_kidx, cache_l3_k, cache_l3_v, meta_tokens, l0_norm, l0_w_in, l0_w_out, l0_lam_q1, l0_lam_k1, l0_lam_q2, l0_lam_k2, l0_subln, l1_norm, l1_w_in, l1_w_out, l1_sinks, l2_norm, l2_w_in, l2_w_out, l3_norm, l3_w_in, l3_w_out, l3_lam_q1, l3_lam_k1, l3_lam_q2, l3_lam_k2, l3_subln, final_norm):
    nb, seq, d = x_prompt.shape
    db, ds, _ = x_sample.shape
    past = cache_l0_k.shape[1]
    sw_rows = cache_l1_k.shape[1]
    assert seq % CHUNK == 0 and meta_tokens.shape[0] == N_META and ds <= LANES

    xs = {"f": x_prompt.reshape(nb * seq, d), "m": meta_tokens.astype(_F32), "s": x_sample.reshape(db * ds, d)}
    pos = {"f": N_META + jnp.arange(seq), "m": jnp.arange(N_META), "s": past + jnp.arange(ds)}
    pos["s"] = jnp.tile(pos["s"], db)

    tabs128 = {n: _rope_tables(p, 128, 1.0) for n, p in pos.items()}
    tabs128_da = {n: _rope_tables(p, 128, DA_HD ** -0.5 * LOG2E) for n, p in pos.items()}
    tabs128_ds = {n: _rope_tables(p, 128, DS_HD ** -0.5 * LOG2E) for n, p in pos.items()}
    tabs64 = {n: _rope_tables(p, 64, 1.0) for n, p in pos.items()}
    tabs64_sw = {n: _rope_tables(p, 64, SW_HD ** -0.5 * LOG2E) for n, p in pos.items()}

    layers = [
        dict(norm=l0_norm, w_in=l0_w_in, w_out=l0_w_out, lam=(l0_lam_q1, l0_lam_k1, l0_lam_q2, l0_lam_k2),
             subln=l0_subln, cache=(cache_l0_k, cache_l0_v)),
        dict(norm=l1_norm, w_in=l1_w_in, w_out=l1_w_out, sinks=l1_sinks, cache=(cache_l1_k, cache_l1_v)),
        dict(norm=l2_norm, w_in=l2_w_in, w_out=l2_w_out, cache=(cache_l2_k, cache_l2_v, cache_l2_kidx)),
        dict(norm=l3_norm, w_in=l3_w_in, w_out=l3_w_out, lam=(l3_lam_q1, l3_lam_k1, l3_lam_q2, l3_lam_k2),
             subln=l3_subln, cache=(cache_l3_k, cache_l3_v)),
    ]
    norms = [lp["norm"] for lp in layers[1:]] + [final_norm]
    depth = len(layers)

    hs = {n: _rms_norm_rows(x, layers[0]["norm"]) for n, x in xs.items()}
    p_st, s_st = [], []
    ys = {}
    for li, lp in enumerate(layers):
        kind = li % N_MIXERS
        w_in = lp["w_in"].astype(_MXU)
        w_out = lp["w_out"].astype(_MXU)
        o, z = {}, {}
        if kind == 0:
            lam_init = 0.8 - 0.6 * math.exp(-0.3 * li)
            wq, wk, wv, wz = (w_in[:, i * DA_WIDTH:(i + 1) * DA_WIDTH] for i in range(4))
            lams = jnp.stack([l.astype(_F32) for l in lp["lam"]])
            subln = lp["subln"].reshape(1, 2 * DA_HD).astype(_F32)
            subln_col = jnp.broadcast_to(lp["subln"].astype(_F32)[:, None], (2 * DA_HD, LANES))
            q, k32, k16, v32, v16 = {}, {}, {}, {}, {}
            for n in xs:
                (q[n],) = _project(hs[n], wq, [_MXU], *tabs128_da[n])
                k32[n], k16[n] = _project(hs[n], wk, [_F32, _MXU], *tabs128[n])
                if n == "f":
                    v32[n], vt_f = _project(hs[n], wv, [_F32], transposed_out=True)
                else:
                    v32[n], v16[n] = _project(hs[n], wv, [_F32, _MXU])
                (z[n],) = _project(hs[n], wz, [_MXU])
            o["f"] = _diff_frames(q["f"], k16["f"], vt_f, k16["m"], v16["m"].T, lams, subln_col, lam_init, nb, seq)
            o["m"] = _diff_small(q["m"], k16["m"], v16["m"], None, None, lams, subln, lam_init, 1, N_META)
            ck, cv = lp["cache"]
            o["s"] = _diff_small(q["s"], k16["s"], v16["s"], ck.reshape(db, past, DA_WIDTH),
                                 cv.reshape(db, past, DA_WIDTH), lams, subln, lam_init, db, ds)
            hshape = (DA_HEADS, 2 * DA_HD)

            def with_meta(f, mrows):
                mm = jnp.broadcast_to(mrows.reshape(1, N_META, *hshape), (nb, N_META, *hshape))
                return jnp.concatenate([mm, f.reshape(nb, seq, *hshape)], axis=1)

            p_st.append([with_meta(k32["f"], k32["m"]), with_meta(v32["f"], v32["m"])])
            s_st.append([k32["s"].reshape(db, ds, *hshape), v32["s"].reshape(db, ds, *hshape)])
        elif kind == 1:
            kvd = SW_KV * SW_HD
            wq = w_in[:, :SW_WIDTH]
            wk = w_in[:, SW_WIDTH:SW_WIDTH + kvd]
            wv = w_in[:, SW_WIDTH + kvd:SW_WIDTH + 2 * kvd]
            wz = w_in[:, SW_WIDTH + 2 * kvd:]
            wkd, wvd = _dup_pairs(wk, SW_HD), _dup_pairs(wv, SW_HD)
            sink_rows = {}
            q, kv32, kd, vd = {}, {}, {}, {}
            for n, rows in (("f", CHUNK), ("m", N_META), ("s", ds)):
                sk = (lp["sinks"].astype(_F32) * LOG2E).reshape(SW_KV, SW_REP, 1, 1)
                sink_rows[n] = jnp.broadcast_to(sk, (SW_KV, SW_REP, rows, LANES)).reshape(SW_KV, SW_REP * rows, LANES)
                (q[n],) = _project(hs[n], wq, [_MXU], *tabs64_sw[n])
                (kd[n],) = _project(hs[n], wkd, [_MXU], *tabs64[n])
                (vd[n],) = _project(hs[n], wvd, [_MXU])
                (z[n],) = _project(hs[n], wz, [_MXU])
            kmeta_blk = _pad_rows_front(kd["m"], WINDOW)
            vmeta_blk = _pad_rows_front(vd["m"], WINDOW)
            o["f"] = _swa_frames(q["f"], kd["f"], vd["f"], kmeta_blk, vmeta_blk, sink_rows["f"], nb, seq)
            o["m"] = _swa_small(q["m"], kd["m"], vd["m"], None, None, sink_rows["m"], 1, N_META)
            ck, cv = lp["cache"]
            ckd = _dup_pairs(ck.reshape(db, sw_rows, kvd), SW_HD).astype(_MXU)
            cvd = _dup_pairs(cv.reshape(db, sw_rows, kvd), SW_HD).astype(_MXU)
            o["s"] = _swa_small(q["s"], kd["s"], vd["s"], ckd, cvd, sink_rows["s"], db, ds)
            assert seq >= sw_rows and sw_rows % 8 == 0
            h_tail = hs["f"].reshape(nb, seq, d)[:, seq - sw_rows:].reshape(nb * sw_rows, d)
            tail_pos = N_META + seq - sw_rows + jnp.arange(sw_rows)
            tail_tabs, tail_shift = _rope_tables(tail_pos, 64, 1.0)
            (kt,) = _project(h_tail, _pad_cols(wk), [_F32], tail_tabs, tail_shift)
            (vt,) = _project(h_tail, _pad_cols(wv), [_F32])
            p_st.append([kt[:, :kvd].reshape(nb, sw_rows, SW_KV, SW_HD), vt[:, :kvd].reshape(nb, sw_rows, SW_KV, SW_HD)])
            (ks,) = _project(hs["s"], _pad_cols(wk), [_F32], *tabs64["s"])
            (vs_,) = _project(hs["s"], _pad_cols(wv), [_F32])
            ks = ks[:, :kvd].reshape(db, ds, SW_KV, SW_HD)
            vs_ = vs_[:, :kvd].reshape(db, ds, SW_KV, SW_HD)
            k_all = jnp.concatenate([ck, ks], axis=1)
            v_all = jnp.concatenate([cv, vs_], axis=1)
            s_st.append([k_all[:, k_all.shape[1] - sw_rows:], v_all[:, v_all.shape[1] - sw_rows:]])
        else:
            kvd = DS_KV * DS_HD
            c0 = 0
            wq = w_in[:, c0:c0 + DS_WIDTH]; c0 += DS_WIDTH
            wk = w_in[:, c0:c0 + kvd]; c0 += kvd
            wv = w_in[:, c0:c0 + kvd]; c0 += kvd
            wz = w_in[:, c0:c0 + DS_WIDTH]; c0 += DS_WIDTH
            wqi = w_in[:, c0:c0 + IDX_HEADS * IDX_HD]; c0 += IDX_HEADS * IDX_HD
            wki = w_in[:, c0:c0 + IDX_HD]; c0 += IDX_HD
            wwi = w_in[:, c0:c0 + IDX_HEADS]
            wkid = jnp.concatenate([wki, wki], axis=1)
            wwi_p = _pad_cols(wwi)
            wi_scale = IDX_HEADS ** -0.5 * IDX_HD ** -0.5
            q, k32, k16, v32, v16, qi, kid32, kid16, wi = {}, {}, {}, {}, {}, {}, {}, {}, {}
            for n in xs:
                (q[n],) = _project(hs[n], wq, [_MXU], *tabs128_ds[n])
                k32[n], k16[n] = _project(hs[n], wk, [_F32, _MXU], *tabs128[n])
                if n == "f":
                    v32[n], vt_f = _project(hs[n], wv, [_F32], transposed_out=True)
                else:
                    v32[n], v16[n] = _project(hs[n], wv, [_F32, _MXU])
                (z[n],) = _project(hs[n], wz, [_MXU])
                (qi[n],) = _project(hs[n], wqi, [_MXU], *tabs64[n])
                kid32[n], kid16[n] = _project(hs[n], wkid, [_F32, _MXU], *tabs64[n])
                (wi[n],) = _project(hs[n], wwi_p, [_F32], scale=wi_scale)
            topk_p = min(TOPK_MAX, seq // 4)
            topk_s = min(TOPK_MAX, (past + ds) // 4)
            km_blk = _pad_rows_back(k16["m"], 1, N_META, LANES)
            vm_blk = _pad_rows_back(v16["m"], 1, N_META, LANES)
            kidm_blk = _pad_rows_back(kid16["m"], 1, N_META, LANES)
            o["f"] = _dsa_frames(q["f"], k16["f"], vt_f, qi["f"], kid16["f"], wi["f"],
                                 km_blk, vm_blk.T, kidm_blk, nb, seq, topk_p)
            o["m"] = _dsa_small(q["m"], km_blk, vm_blk, qi["m"], kidm_blk, wi["m"], None, None, None,
                                1, N_META, N_META, topk_p)
            ck, cv, cki = lp["cache"]
            ckid = jnp.concatenate([cki, cki], axis=-1).astype(_MXU)
            o["s"] = _dsa_small(q["s"], _pad_rows_back(k16["s"], db, ds, LANES), _pad_rows_back(v16["s"], db, ds, LANES),
                                qi["s"], _pad_rows_back(kid16["s"], db, ds, LANES), wi["s"],
                                ck.reshape(db, past, kvd), cv.reshape(db, past, kvd), ckid,
                                db, ds, ds, topk_s)

            def with_meta(f, mrows, *tail):
                mm = jnp.broadcast_to(mrows.reshape(1, N_META, *tail), (nb, N_META, *tail))
                return jnp.concatenate([mm, f.reshape(nb, seq, *tail)], axis=1)

            p_st.append([with_meta(k32["f"], k32["m"], DS_KV, DS_HD), with_meta(v32["f"], v32["m"], DS_KV, DS_HD),
                         with_meta(kid32["f"][:, :IDX_HD], kid32["m"][:, :IDX_HD], IDX_HD)])
            s_st.append([k32["s"].reshape(db, ds, DS_KV, DS_HD), v32["s"].reshape(db, ds, DS_KV, DS_HD),
                         kid32["s"][:, :IDX_HD].reshape(db, ds, IDX_HD)])
        last = li == depth - 1
        for n in xs:
            res = _out_project(o[n], z[n], w_out, xs[n], norms[li], last)
            if last:
                (ys[n],) = res
            else:
                xs[n], hs[n] = res

    y_prompt = ys["f"].reshape(nb, seq, d)
    y_sample = ys["s"].reshape(db, ds, d)
    return (y_prompt, y_sample,
            p_st[0][0], p_st[0][1], s_st[0][0], s_st[0][1],
            p_st[1][0], p_st[1][1], s_st[1][0], s_st[1][1],
            p_st[2][0], p_st[2][1], p_st[2][2], s_st[2][0], s_st[2][1], s_st[2][2],
            p_st[3][0], p_st[3][1], s_st[3][0], s_st[3][1])


def _pad_cols(w):
    n = w.shape[1]
    return jnp.pad(w, ((0, 0), (0, (-n) % LANES)))
```

```python
import functools
import math

import jax
import jax.numpy as jnp
from jax import lax
from jax.experimental import pallas as pl
from jax.experimental.pallas import tpu as pltpu

CHUNK = 64
N_META = 16
ROPE_THETA = 500000.0
NORM_EPS = 1e-6
NEG_INF = -1e30
DA_HEADS = 8
DA_HD = 128
DA_WIDTH = DA_HEADS * 2 * DA_HD
SW_HEADS = 32
SW_KV = 4
SW_HD = 64
SW_WIDTH = SW_HEADS * SW_HD
SW_REP = SW_HEADS // SW_KV
WINDOW = 128
DS_HEADS = 16
DS_KV = 4
DS_HD = 128
DS_WIDTH = DS_HEADS * DS_HD
DS_REP = DS_HEADS // DS_KV
IDX_HEADS = 16
IDX_HD = 64
TOPK_MAX = 256
N_MIXERS = 3

LANES = 128
INT_MIN = -(2 ** 31)
LOG2E = math.log2(math.e)
ONES_ROWS = 16
VMEM_LIMIT = 56 * 1024 * 1024

_MXU = jnp.bfloat16
_F32 = jnp.float32

_NT = (((1,), (1,)), ((), ()))


def _cparams(sem):
    return pltpu.CompilerParams(dimension_semantics=sem, vmem_limit_bytes=VMEM_LIMIT)


def _row_tile(m, pref):
    t = pref
    while t > 8 and m % t:
        t //= 2
    assert m % t == 0, (m, pref)
    return t


def _dot(a, b):
    return jnp.dot(a, b, preferred_element_type=_F32)


def _dot_nt(a, b):
    return lax.dot_general(a, b, _NT, preferred_element_type=_F32)


def _norm_kernel(x_ref, g_ref, h_ref):
    x = x_ref[...]
    y = x * lax.rsqrt(jnp.mean(x * x, axis=-1, keepdims=True) + NORM_EPS)
    h_ref[...] = (y * g_ref[...]).astype(h_ref.dtype)


def _rms_norm_rows(x, g):
    m, d = x.shape
    out_dtype = _MXU
    tm = _row_tile(m, 512)
    return pl.pallas_call(
        _norm_kernel,
        grid=(m // tm,),
        in_specs=[pl.BlockSpec((tm, d), lambda i: (i, 0)),
                  pl.BlockSpec((1, d), lambda i: (0, 0))],
        out_specs=pl.BlockSpec((tm, d), lambda i: (i, 0)),
        out_shape=jax.ShapeDtypeStruct((m, d), out_dtype),
        compiler_params=_cparams(("parallel",)),
        name="rms_norm",
    )(x, g.reshape(1, d).astype(_F32))


def _proj_kernel(*refs, rope_shift, scale, n_out, tn, group_cols):
    h_ref, w_ref = refs[0], refs[1]
    pos = 2
    if rope_shift is not None:
        c_ref, a_ref, b_ref = refs[2:5]
        pos = 5
    outs = refs[pos:pos + n_out]
    ot_ref = refs[pos + n_out] if group_cols else None
    if ot_ref is not None:
        tm = ot_ref.shape[2]
        row = lax.broadcasted_iota(jnp.int32, (ONES_ROWS, tm), 0)
        ones_blk = jnp.where(row == 0, 1.0, 0.0).astype(ot_ref.dtype)
        for g in range(tn // group_cols):
            base = g * (group_cols + ONES_ROWS) + group_cols
            ot_ref[0, base:base + ONES_ROWS, :] = ones_blk
    h = h_ref[...]
    cw = 2 * LANES if tn % (2 * LANES) == 0 else LANES
    for c in range(tn // cw):
        wide = _dot(h, w_ref[:, c * cw:(c + 1) * cw])
        for sub in range(cw // LANES):
            acc = wide[:, sub * LANES:(sub + 1) * LANES]
            if rope_shift is not None:
                acc = (acc * c_ref[...]
                       + pltpu.roll(acc, rope_shift, 1) * a_ref[...]
                       + pltpu.roll(acc, LANES - rope_shift, 1) * b_ref[...])
            elif scale is not None:
                acc = acc * scale
            cols = slice(c * cw + sub * LANES, c * cw + (sub + 1) * LANES)
            for o in outs:
                o[:, cols] = acc.astype(o.dtype)
            if ot_ref is not None:
                lo = cols.start
                r0 = (lo // group_cols) * (group_cols + ONES_ROWS) + lo % group_cols
                ot_ref[0, r0:r0 + LANES, :] = acc.T.astype(ot_ref.dtype)


def _project(h, w, out_dtypes, tabs=None, rope_shift=None, scale=None, group_cols=0):
    m, d = h.shape
    n = w.shape[1]
    tm = _row_tile(m if tabs is None else math.gcd(m, tabs[0].shape[0]), 512)
    tn = n if n <= 1024 else 1024
    assert n % tn == 0 and tn % LANES == 0
    in_specs = [pl.BlockSpec((tm, d), lambda j, i: (i, 0)),
                pl.BlockSpec((d, tn), lambda j, i: (0, j))]
    args = [h, w]
    if tabs is not None:
        trows = tabs[0].shape[0]
        assert trows % tm == 0
        nrep = trows // tm
        for t in tabs:
            in_specs.append(pl.BlockSpec((tm, LANES), lambda j, i: (i % nrep, 0)))
            args.append(t)
    else:
        rope_shift = None
    out_specs = [pl.BlockSpec((tm, tn), lambda j, i: (i, j)) for _ in out_dtypes]
    out_shape = [jax.ShapeDtypeStruct((m, n), dt) for dt in out_dtypes]
    if group_cols:
        assert tn % group_cols == 0 and group_cols % LANES == 0
        grow = group_cols + ONES_ROWS
        out_specs.append(pl.BlockSpec((1, tn // group_cols * grow, tm), lambda j, i: (i, j, 0)))
        out_shape.append(jax.ShapeDtypeStruct((m // tm, n // group_cols * grow, tm), _MXU))
    outs = pl.pallas_call(
        functools.partial(_proj_kernel, rope_shift=rope_shift, scale=scale,
                          n_out=len(out_dtypes), tn=tn, group_cols=group_cols),
        grid=(n // tn, m // tm),
        in_specs=in_specs,
        out_specs=out_specs,
        out_shape=out_shape,
        compiler_params=_cparams(("parallel", "parallel")),
        name="in_proj",
    )(*args)
    return outs


def _outproj_kernel(o_ref, z_ref, w_ref, x_ref, g_ref, *outs, emit_x):
    z = z_ref[...].astype(_F32)
    a = o_ref[...].astype(_F32) * (z / (1.0 + jnp.exp(-z)))
    y = x_ref[...] + _dot(a.astype(w_ref.dtype), w_ref[...])
    if emit_x:
        outs[0][...] = y
    n = y * lax.rsqrt(jnp.mean(y * y, axis=-1, keepdims=True) + NORM_EPS) * g_ref[...]
    outs[-1][...] = n.astype(outs[-1].dtype)


def _out_project(o, z, w, x, g_next, last):
    m, d = x.shape
    wd = o.shape[1]
    tm = _row_tile(m, 256)
    row = lambda i: (i, 0)
    fixed = lambda i: (0, 0)
    if last:
        out_shape = [jax.ShapeDtypeStruct((m, d), _F32)]
    else:
        out_shape = [jax.ShapeDtypeStruct((m, d), _F32), jax.ShapeDtypeStruct((m, d), _MXU)]
    return pl.pallas_call(
        functools.partial(_outproj_kernel, emit_x=not last),
        grid=(m // tm,),
        in_specs=[pl.BlockSpec((tm, wd), row), pl.BlockSpec((tm, wd), row),
                  pl.BlockSpec((wd, d), fixed), pl.BlockSpec((tm, d), row),
                  pl.BlockSpec((1, d), fixed)],
        out_specs=[pl.BlockSpec((tm, d), row) for _ in out_shape],
        out_shape=out_shape,
        compiler_params=_cparams(("parallel",)),
        name="out_proj",
    )(o, z, w, x, g_next.reshape(1, d).astype(_F32))


def _rope_tables(pos, hd, scale=1.0):
    rd = hd // 4
    half = rd // 2
    lane = jnp.arange(LANES)
    d = lane % hd
    f = d % half
    inv_freq = ROPE_THETA ** (-(jnp.arange(half, dtype=_F32) * 2.0 / rd))
    ang = pos.astype(_F32)[:, None] * inv_freq[f][None, :]
    cos, sin = jnp.cos(ang), jnp.sin(ang)
    in_rot = (d < rd)[None, :]
    c = jnp.where(in_rot, cos, 1.0)
    a = jnp.where(((d >= half) & (d < rd))[None, :], sin, 0.0)
    b = jnp.where((d < half)[None, :], -sin, 0.0)
    s = jnp.float32(scale)
    return (c * s, a * s, b * s), half


def _lam_value(lam_ref, lam_init):
    l = lam_ref[...]
    s1 = jnp.sum(l[0:1] * l[1:2], axis=-1, keepdims=True)
    s2 = jnp.sum(l[2:3] * l[3:4], axis=-1, keepdims=True)
    return jnp.exp(s1) - jnp.exp(s2) + lam_init


def _diff_finish(acc1, l1, acc2, l2, lam, sub, lam_init):
    o = acc1 / l1 - lam * (acc2 / l2)
    o = o * lax.rsqrt(jnp.mean(o * o, axis=-1, keepdims=True) + NORM_EPS) * sub
    return o * (1.0 - lam_init)


def _diff_frames_kernel(q_ref, k_ref, vt_ref, km_ref, vmt_ref, lam_ref, subc_ref, o_ref,
                        acc_s, sa_s, sb_s, pa_s, pb_s, *, tq, lam_init):
    i = pl.program_id(2)
    hw = 2 * DA_HD

    def q_half(j):
        return q_ref[:, j * DA_HD:(j + 1) * DA_HD]

    km = km_ref[...]
    vmt = vmt_ref[...]
    ms = []
    for j in range(2):
        s = _dot_nt(km[:, j * DA_HD:(j + 1) * DA_HD], q_half(j))
        m = jnp.max(s, axis=0, keepdims=True)
        ms.append(m)
        acc_s[j] = _dot(vmt, jnp.exp2(s - m).astype(vmt.dtype))

    def scores_into(dst, t):
        start = pl.multiple_of(t * tq, tq)
        for j in range(2):
            dst[j] = _dot_nt(k_ref[pl.ds(start, tq), j * DA_HD:(j + 1) * DA_HD], q_half(j))

    def value_update(t, a, p_rd):
        vt = vt_ref[t]
        for j in range(2):
            acc_s[j] = a[j] * acc_s[j] + _dot(vt, p_rd[j])

    def tile_step(t, carry, src, dst, p_rd, p_wr, masked, has_next):
        m, a = carry
        if masked:
            kc = lax.broadcasted_iota(jnp.int32, (tq, tq), 0) // CHUNK
            qc = lax.broadcasted_iota(jnp.int32, (tq, tq), 1) // CHUNK
            ok = kc <= qc
        m_out, a_out = [], []
        for j in range(2):
            s = src[j]
            if masked:
                s = jnp.where(ok, s, NEG_INF)
            m_new = jnp.maximum(m[j], jnp.max(s, axis=0, keepdims=True))
            a_out.append(jnp.exp2(m[j] - m_new))
            m_out.append(m_new)
            p_wr[j] = jnp.exp2(s - m_new).astype(p_wr.dtype)
        value_update(jnp.maximum(t - 1, 0), a, p_rd)
        if has_next:
            scores_into(dst, t + 1)
        return tuple(m_out), tuple(a_out)

    scores_into(sa_s, 0)
    pb_s[...] = jnp.zeros_like(pb_s)
    one = jnp.ones((1, tq), _F32)
    carry = ((ms[0], ms[1]), (one, one))

    def pair_body(u, carry):
        carry = tile_step(2 * u, carry, sa_s, sb_s, pb_s, pa_s, False, True)
        return tile_step(2 * u + 1, carry, sb_s, sa_s, pa_s, pb_s, False, True)

    carry = lax.fori_loop(0, i // 2, pair_body, carry)

    @pl.when(i % 2 == 0)
    def _():
        _, a = tile_step(i, carry, sa_s, sb_s, pb_s, pa_s, True, False)
        value_update(i, a, pa_s)

    @pl.when(i % 2 == 1)
    def _():
        c = tile_step(i - 1, carry, sa_s, sb_s, pb_s, pa_s, False, True)
        _, a = tile_step(i, c, sb_s, sa_s, pa_s, pb_s, True, False)
        value_update(i, a, pb_s)

    acc0 = acc_s[0]
    acc1 = acc_s[1]
    lam = _lam_value(lam_ref, lam_init)
    inv0 = 1.0 / acc0[hw:hw + 1]
    inv1 = lam / acc1[hw:hw + 1]
    ot = acc0[:hw] * inv0 - acc1[:hw] * inv1
    sub = jnp.concatenate([subc_ref[...]] * (tq // LANES), axis=1)
    ot = ot * lax.rsqrt(jnp.mean(ot * ot, axis=0, keepdims=True) + NORM_EPS) * sub * (1.0 - lam_init)
    o_ref[...] = ot.T.astype(o_ref.dtype)


def _diff_frames(q, k, vt, km, vmt, lams, subln_col, lam_init, nb, seq):
    m = q.shape[0]
    hw = 2 * DA_HD
    hr = hw + ONES_ROWS
    tq = vt.shape[2]
    assert seq % tq == 0 and tq % LANES == 0
    nt = seq // tq
    return pl.pallas_call(
        functools.partial(_diff_frames_kernel, tq=tq, lam_init=lam_init),
        grid=(nb, DA_HEADS, nt),
        in_specs=[pl.BlockSpec((tq, hw), lambda b, h, i: (b * nt + i, h)),
                  pl.BlockSpec((seq, hw), lambda b, h, i: (b, h)),
                  pl.BlockSpec((nt, hr, tq), lambda b, h, i: (b, h, 0)),
                  pl.BlockSpec((N_META, hw), lambda b, h, i: (0, h)),
                  pl.BlockSpec((hr, N_META), lambda b, h, i: (h, 0)),
                  pl.BlockSpec((4, DA_HD), lambda b, h, i: (0, 0)),
                  pl.BlockSpec((hw, LANES), lambda b, h, i: (0, 0))],
        out_specs=pl.BlockSpec((tq, hw), lambda b, h, i: (b * nt + i, h)),
        out_shape=jax.ShapeDtypeStruct((m, DA_WIDTH), _MXU),
        scratch_shapes=[pltpu.VMEM((2, hr, tq), _F32),
                        pltpu.VMEM((2, tq, tq), _F32), pltpu.VMEM((2, tq, tq), _F32),
                        pltpu.VMEM((2, tq, tq), _MXU), pltpu.VMEM((2, tq, tq), _MXU)],
        compiler_params=_cparams(("parallel", "parallel", "arbitrary")),
        name="diff_frames",
    )(q, k, vt, km, vmt, lams, subln_col)


def _diff_small_kernel(*refs, has_cache, lam_init):
    if has_cache:
        q_ref, k_ref, v_ref, ck_ref, cv_ref, lam_ref, sub_ref, o_ref = refs
    else:
        q_ref, k_ref, v_ref, lam_ref, sub_ref, o_ref = refs
    q = q_ref[...]
    kn = k_ref[...]
    vn = v_ref[...]
    parts = [(kn, vn)]
    if has_cache:
        parts.append((ck_ref[0].astype(kn.dtype), cv_ref[0].astype(vn.dtype)))
    accs, ls = [], []
    for j in range(2):
        qj = q[:, j * DA_HD:(j + 1) * DA_HD]
        ss = [_dot_nt(qj, kp[:, j * DA_HD:(j + 1) * DA_HD]) for kp, _ in parts]
        m = ss[0].max(axis=-1, keepdims=True)
        for s in ss[1:]:
            m = jnp.maximum(m, s.max(axis=-1, keepdims=True))
        l = 0.0
        acc = 0.0
        for s, (_, vp) in zip(ss, parts):
            p = jnp.exp2(s - m)
            l = l + jnp.sum(p, axis=-1, keepdims=True)
            acc = acc + _dot(p.astype(vp.dtype), vp)
        accs.append(acc)
        ls.append(l)
    lam = _lam_value(lam_ref, lam_init)
    o = _diff_finish(accs[0], ls[0], accs[1], ls[1], lam, sub_ref[...], lam_init)
    o_ref[...] = o.astype(o_ref.dtype)


def _diff_small(q, k, v, cache_k, cache_v, lams, subln, lam_init, nb, nq):
    hw = 2 * DA_HD
    has_cache = cache_k is not None
    qspec = pl.BlockSpec((nq, hw), lambda b, h: (b, h))
    in_specs = [qspec, qspec, qspec]
    args = [q, k, v]
    if has_cache:
        past = cache_k.shape[1]
        cspec = pl.BlockSpec((1, past, hw), lambda b, h: (b, 0, h))
        in_specs += [cspec, cspec]
        args += [cache_k, cache_v]
    in_specs += [pl.BlockSpec((4, DA_HD), lambda b, h: (0, 0)),
                 pl.BlockSpec((1, hw), lambda b, h: (0, 0))]
    args += [lams, subln]
    return pl.pallas_call(
        functools.partial(_diff_small_kernel, has_cache=has_cache, lam_init=lam_init),
        grid=(nb, DA_HEADS),
        in_specs=in_specs,
        out_specs=qspec,
        out_shape=jax.ShapeDtypeStruct((nb * nq, DA_WIDTH), _MXU),
        compiler_params=_cparams(("parallel", "parallel")),
        name="diff_small",
    )(*args)


def _stack_heads(qg, nheads):
    n = qg.shape[0]
    lane = lax.broadcasted_iota(jnp.int32, (n, LANES), 1)
    lo = lane < SW_HD
    pieces = []
    for r in range(nheads):
        blk = qg[:, (r // 2) * LANES:(r // 2 + 1) * LANES]
        keep = lo if r % 2 == 0 else jnp.logical_not(lo)
        pieces.append(jnp.where(keep, blk, jnp.zeros_like(blk)))
    return jnp.concatenate(pieces, axis=0)


def _unstack_heads(o, n, nheads):
    lane = lax.broadcasted_iota(jnp.int32, (n, LANES), 1)
    lo = lane < SW_HD
    return [jnp.where(lo, o[(2 * p) * n:(2 * p + 1) * n], o[(2 * p + 1) * n:(2 * p + 2) * n])
            for p in range(nheads // 2)]


def _sink_softmax_pv(ss, vs, sk):
    m = sk
    for s in ss:
        m = jnp.maximum(m, jnp.max(s, axis=-1, keepdims=True))
    den = jnp.exp2(sk - m)
    acc = 0.0
    for s, v in zip(ss, vs):
        e = jnp.exp2(s - m)
        den = den + jnp.sum(e, axis=-1, keepdims=True)
        acc = acc + _dot(e.astype(v.dtype), v)
    return acc / den


def _swa_frames_kernel(q_ref, kc_ref, vc_ref, kp_ref, vp_ref, km_ref, vm_ref, sk_ref, o_ref,
                       kcat, vcat, *, tq):
    i = pl.program_id(1)
    first = i == 0
    prev_rows = WINDOW
    kcat[0:prev_rows] = jnp.where(first, km_ref[...], kp_ref[...])
    vcat[0:prev_rows] = jnp.where(first, vm_ref[...], vp_ref[...])
    kcat[prev_rows:prev_rows + tq] = kc_ref[...]
    vcat[prev_rows:prev_rows + tq] = vc_ref[...]
    nwin = WINDOW + CHUNK
    for c in range(tq // CHUNK):
        rows = slice(c * CHUNK, (c + 1) * CHUNK)
        win = slice(c * CHUNK, c * CHUNK + nwin)
        for g in range(SW_KV):
            gl = slice(g * LANES, (g + 1) * LANES)
            qst = _stack_heads(q_ref[rows, g * SW_REP * SW_HD:(g + 1) * SW_REP * SW_HD], SW_REP)
            s = _dot_nt(qst, kcat[win, gl])
            if c * CHUNK < prev_rows - N_META:
                col = lax.broadcasted_iota(jnp.int32, s.shape, 1) + c * CHUNK
                ok = jnp.logical_or(col >= prev_rows - N_META, jnp.logical_not(first))
                s = jnp.where(ok, s, NEG_INF)
            o = _sink_softmax_pv([s], [vcat[win, gl]], sk_ref[g][:, 0:1])
            for p, blk in enumerate(_unstack_heads(o, CHUNK, SW_REP)):
                lo = g * SW_REP * SW_HD + p * LANES
                o_ref[rows, lo:lo + LANES] = blk.astype(o_ref.dtype)


def _swa_frames(q, kd, vd, kmeta_blk, vmeta_blk, sink_rows, nb, seq):
    m = q.shape[0]
    tq = _row_tile(seq, 512)
    nt = seq // tq
    pw = SW_KV * LANES
    per = tq // WINDOW
    cur = lambda b, i: (b * nt + i, 0)
    prev = lambda b, i: (jnp.maximum((b * nt + i) * per - 1, 0), 0)
    fixed = lambda b, i: (0, 0)
    return pl.pallas_call(
        functools.partial(_swa_frames_kernel, tq=tq),
        grid=(nb, nt),
        in_specs=[pl.BlockSpec((tq, SW_WIDTH), cur),
                  pl.BlockSpec((tq, pw), cur), pl.BlockSpec((tq, pw), cur),
                  pl.BlockSpec((WINDOW, pw), prev), pl.BlockSpec((WINDOW, pw), prev),
                  pl.BlockSpec((WINDOW, pw), fixed), pl.BlockSpec((WINDOW, pw), fixed),
                  pl.BlockSpec((SW_KV, SW_REP * CHUNK, LANES), lambda b, i: (0, 0, 0))],
        out_specs=pl.BlockSpec((tq, SW_WIDTH), cur),
        out_shape=jax.ShapeDtypeStruct((m, SW_WIDTH), _MXU),
        scratch_shapes=[pltpu.VMEM((WINDOW + tq, pw), _MXU), pltpu.VMEM((WINDOW + tq, pw), _MXU)],
        compiler_params=_cparams(("parallel", "arbitrary")),
        name="swa_frames",
    )(q, kd, vd, kd, vd, kmeta_blk, vmeta_blk, sink_rows)


def _swa_small_kernel(*refs, has_cache, nq):
    if has_cache:
        q_ref, k_ref, v_ref, ck_ref, cv_ref, sk_ref, o_ref = refs
    else:
        q_ref, k_ref, v_ref, sk_ref, o_ref = refs
    for g in range(SW_KV):
        gl = slice(g * LANES, (g + 1) * LANES)
        qst = _stack_heads(q_ref[:, g * SW_REP * SW_HD:(g + 1) * SW_REP * SW_HD], SW_REP)
        ks, vs = [k_ref[:, gl]], [v_ref[:, gl]]
        if has_cache:
            ks.append(ck_ref[0][:, gl])
            vs.append(cv_ref[0][:, gl])
        ss = [_dot_nt(qst, kk) for kk in ks]
        o = _sink_softmax_pv(ss, vs, sk_ref[g][:, 0:1])
        for p, blk in enumerate(_unstack_heads(o, nq, SW_REP)):
            lo = g * SW_REP * SW_HD + p * LANES
            o_ref[:, lo:lo + LANES] = blk.astype(o_ref.dtype)


def _swa_small(q, kd, vd, cache_kd, cache_vd, sink_rows, nb, nq):
    pw = SW_KV * LANES
    has_cache = cache_kd is not None
    row = lambda b: (b, 0)
    in_specs = [pl.BlockSpec((nq, SW_WIDTH), row), pl.BlockSpec((nq, pw), row), pl.BlockSpec((nq, pw), row)]
    args = [q, kd, vd]
    if has_cache:
        past = cache_kd.shape[1]
        cspec = pl.BlockSpec((1, past, pw), lambda b: (b, 0, 0))
        in_specs += [cspec, cspec]
        args += [cache_kd, cache_vd]
    in_specs.append(pl.BlockSpec((SW_KV, SW_REP * nq, LANES), lambda b: (0, 0, 0)))
    args.append(sink_rows)
    return pl.pallas_call(
        functools.partial(_swa_small_kernel, has_cache=has_cache, nq=nq),
        grid=(nb,),
        in_specs=in_specs,
        out_specs=pl.BlockSpec((nq, SW_WIDTH), row),
        out_shape=jax.ShapeDtypeStruct((nb * nq, SW_WIDTH), _MXU),
        compiler_params=_cparams(("parallel",)),
        name="swa_small",
    )(*args)


def _order_key(score, valid):
    bits = lax.bitcast_convert_type(score + 0.0, jnp.int32)
    key = jnp.where(bits < 0, bits ^ jnp.int32(0x7FFFFFFF), bits)
    return jnp.where(valid, key, jnp.int32(INT_MIN))


def _stack_idx_heads(qi):
    return _stack_heads(qi, IDX_HEADS)


def _index_scores(qst, wb_of, kid, n):
    score = None
    hb = 4
    for h0 in range(0, IDX_HEADS, hb):
        sc = _dot_nt(qst[h0 * n:(h0 + hb) * n], kid)
        for h in range(h0, h0 + hb):
            part = jnp.maximum(sc[(h - h0) * n:(h - h0 + 1) * n], 0.0) * wb_of(h)
            score = part if score is None else score + part
    return score


def _index_scores_t(kid, qst, wit, n):
    score = None
    hb = 4
    for h0 in range(0, IDX_HEADS, hb):
        sc = _dot_nt(kid, qst[h0 * n:(h0 + hb) * n])
        for h in range(h0, h0 + hb):
            part = jnp.maximum(sc[:, (h - h0) * n:(h - h0 + 1) * n], 0.0) * wit[h:h + 1, :]
            score = part if score is None else score + part
    return score


def _dsa_frames_kernel(q_ref, k_ref, vt_ref, qi_ref, kid_ref, wi_ref, km_ref, vmt_ref, kidm_ref, o_ref,
                       keym_s, key_s, biasm_s, bias_s, qst_s, wit_s, qg_s, m_s, acc_s, *, tq, tk, topk):
    i = pl.program_id(1)
    n_t = (i * tq + tq + tk - 1) // tk
    qst_s[...] = _stack_idx_heads(qi_ref[...])
    wit_s[...] = wi_ref[...].T
    sc_m = _index_scores_t(kidm_ref[...], qst_s[...], wit_s[...], tq)
    row_m = lax.broadcasted_iota(jnp.int32, (LANES, tq), 0)
    keym_s[...] = _order_key(sc_m, row_m < N_META)

    q_chunk = (lax.broadcasted_iota(jnp.int32, (tk, tq), 1) + i * tq) // CHUNK

    def score_body(t, carry):
        start = pl.multiple_of(t * tk, tk)
        sc = _index_scores_t(kid_ref[pl.ds(start, tk), :], qst_s[...], wit_s[...], tq)
        k_chunk = (lax.broadcasted_iota(jnp.int32, (tk, tq), 0) + t * tk) // CHUNK
        key_s[t] = _order_key(sc, k_chunk <= q_chunk)
        return carry

    lax.fori_loop(0, n_t, score_body, 0)

    def count_ge(thr):
        def fold(x):
            return jnp.sum(x.reshape(x.shape[0] // 64, 64, tq), axis=0)

        part = fold(jnp.where(keym_s[...] >= thr, 1.0, 0.0))

        def cbody(t, acc):
            return acc + fold(jnp.where(key_s[t] >= thr, 1.0, 0.0))

        part = lax.fori_loop(0, n_t, cbody, part)
        return jnp.sum(part, axis=0, keepdims=True)

    def bit_body(it, u):
        bit = jnp.int32(31) - it
        cand = u | lax.shift_left(jnp.int32(1), bit)
        cnt = count_ge(cand ^ jnp.int32(INT_MIN))
        return jnp.where(cnt >= float(topk), cand, u)

    u = lax.fori_loop(0, 32, bit_body, jnp.zeros((1, tq), jnp.int32))
    thr = jnp.maximum(u ^ jnp.int32(INT_MIN), jnp.int32(INT_MIN + 1))

    biasm_s[...] = jnp.where(keym_s[...] >= thr, 0.0, NEG_INF)

    def bias_body(t, carry):
        bias_s[t] = jnp.where(key_s[t] >= thr, 0.0, NEG_INF)
        return carry

    lax.fori_loop(0, n_t, bias_body, 0)

    vr = DS_HD + ONES_ROWS
    groups = [(g, slice(g * DS_HD, (g + 1) * DS_HD), slice(g * vr, (g + 1) * vr)) for g in range(DS_KV)]
    for g, gl, gv in groups:
        qg_s[g] = jnp.concatenate(
            [q_ref[:, (g * DS_REP + r) * DS_HD:(g * DS_REP + r + 1) * DS_HD] for r in range(DS_REP)], axis=0)
        s = _dot_nt(km_ref[:, gl], qg_s[g]) + jnp.concatenate([biasm_s[...]] * DS_REP, axis=1)
        m0 = jnp.max(s, axis=0, keepdims=True)
        m_s[g] = m0
        acc_s[g] = _dot(vmt_ref[gv, :], jnp.exp2(s - m0).astype(_MXU))

    def att_body(t, carry):
        start = pl.multiple_of(t * tk, tk)
        bias = jnp.concatenate([bias_s[t]] * DS_REP, axis=1)
        for pair in (groups[:2], groups[2:]):
            ss = [_dot_nt(k_ref[pl.ds(start, tk), gl], qg_s[g]) for g, gl, _ in pair]
            for (g, _, gv), s in zip(pair, ss):
                s = s + bias
                m_prev = m_s[g]
                m_new = jnp.maximum(m_prev, jnp.max(s, axis=0, keepdims=True))
                alpha = jnp.exp2(m_prev - m_new)
                p = jnp.exp2(s - m_new).astype(_MXU)
                acc_s[g] = alpha * acc_s[g] + _dot(vt_ref[t, gv, :], p)
                m_s[g] = m_new
        return carry

    lax.fori_loop(0, n_t, att_body, 0)
    for g in range(DS_KV):
        acc = acc_s[g]
        ot = acc[:DS_HD] * (1.0 / acc[DS_HD:DS_HD + 1])
        for r in range(DS_REP):
            lo = (g * DS_REP + r) * DS_HD
            o_ref[:, lo:lo + DS_HD] = ot[:, r * tq:(r + 1) * tq].T.astype(o_ref.dtype)


def _dsa_frames(q, k, vt, qi, kid, wi, km_blk, vmt_blk, kidm_blk, nb, seq, topk):
    m = q.shape[0]
    tq = _row_tile(seq, 128)
    tk = vt.shape[2]
    assert seq % tk == 0 and tq == LANES
    nt = seq // tq
    nkt = seq // tk
    kvw = DS_KV * DS_HD
    kvr = DS_KV * (DS_HD + ONES_ROWS)
    cur = lambda b, i: (b * nt + i, 0)
    per_b = lambda b, i: (b, 0)
    fixed = lambda b, i: (0, 0)
    return pl.pallas_call(
        functools.partial(_dsa_frames_kernel, tq=tq, tk=tk, topk=topk),
        grid=(nb, nt),
        in_specs=[pl.BlockSpec((tq, DS_WIDTH), cur),
                  pl.BlockSpec((seq, kvw), per_b),
                  pl.BlockSpec((nkt, kvr, tk), lambda b, i: (b, 0, 0)),
                  pl.BlockSpec((tq, IDX_HEADS * IDX_HD), cur),
                  pl.BlockSpec((seq, LANES), per_b),
                  pl.BlockSpec((tq, LANES), cur),
                  pl.BlockSpec((LANES, kvw), fixed), pl.BlockSpec((kvr, LANES), fixed),
                  pl.BlockSpec((LANES, LANES), fixed)],
        out_specs=pl.BlockSpec((tq, DS_WIDTH), cur),
        out_shape=jax.ShapeDtypeStruct((m, DS_WIDTH), _MXU),
        scratch_shapes=[pltpu.VMEM((LANES, tq), jnp.int32), pltpu.VMEM((nkt, tk, tq), jnp.int32),
                        pltpu.VMEM((LANES, tq), _F32), pltpu.VMEM((nkt, tk, tq), _F32),
                        pltpu.VMEM((IDX_HEADS * tq, LANES), _MXU),
                        pltpu.VMEM((LANES, tq), _F32),
                        pltpu.VMEM((DS_KV, DS_REP * tq, DS_HD), _MXU),
                        pltpu.VMEM((DS_KV, 1, DS_REP * tq), _F32),
                        pltpu.VMEM((DS_KV, DS_HD + ONES_ROWS, DS_REP * tq), _F32)],
        compiler_params=_cparams(("parallel", "arbitrary")),
        name="dsa_frames",
    )(q, k, vt, qi, kid, wi, km_blk, vmt_blk, kidm_blk)


def _dsa_small_kernel(*refs, has_cache, nq, n_new, topk):
    if has_cache:
        q_ref, kn_ref, vn_ref, qi_ref, kidn_ref, wi_ref, ck_ref, cv_ref, ckid_ref, o_ref = refs
    else:
        q_ref, kn_ref, vn_ref, qi_ref, kidn_ref, wi_ref, o_ref = refs
    qst = _stack_idx_heads(qi_ref[...])
    wi = wi_ref[...]
    wbs = [jnp.broadcast_to(wi[:, h:h + 1], (nq, LANES)) for h in range(IDX_HEADS)]

    def wb_tile(width):
        return lambda h: jnp.concatenate([wbs[h]] * (width // LANES), axis=1)

    col_n = lax.broadcasted_iota(jnp.int32, (nq, LANES), 1)
    keys = [_order_key(_index_scores(qst, wb_tile(LANES), kidn_ref[...], nq), col_n < n_new)]
    if has_cache:
        past = ckid_ref.shape[1]
        sc = _index_scores(qst, wb_tile(past), ckid_ref[0], nq)
        keys.append(_order_key(sc, jnp.full(sc.shape, True)))

    def count_ge(thr):
        cnt = 0.0
        for kk in keys:
            cnt = cnt + jnp.sum(jnp.where(kk >= thr, 1.0, 0.0), axis=-1, keepdims=True)
        return cnt

    def bit_body(it, u):
        bit = jnp.int32(31) - it
        cand = u | lax.shift_left(jnp.int32(1), bit)
        cnt = count_ge(cand ^ jnp.int32(INT_MIN))
        return jnp.where(cnt >= float(topk), cand, u)

    u = lax.fori_loop(0, 32, bit_body, jnp.zeros((nq, 1), jnp.int32))
    thr = jnp.maximum(u ^ jnp.int32(INT_MIN), jnp.int32(INT_MIN + 1))
    biases = [jnp.where(kk >= thr, 0.0, NEG_INF) for kk in keys]

    for g in range(DS_KV):
        gl = slice(g * DS_HD, (g + 1) * DS_HD)
        qg = jnp.concatenate(
            [q_ref[:, (g * DS_REP + r) * DS_HD:(g * DS_REP + r + 1) * DS_HD] for r in range(DS_REP)], axis=0)
        ks, vs = [kn_ref[:, gl]], [vn_ref[:, gl]]
        if has_cache:
            ks.append(ck_ref[0][:, gl].astype(_MXU))
            vs.append(cv_ref[0][:, gl].astype(_MXU))
        ss = [_dot_nt(qg, kk) + jnp.concatenate([bb] * DS_REP, axis=0) for kk, bb in zip(ks, biases)]
        m = ss[0].max(axis=-1, keepdims=True)
        for s in ss[1:]:
            m = jnp.maximum(m, s.max(axis=-1, keepdims=True))
        l = 0.0
        acc = 0.0
        for s, vv in zip(ss, vs):
            p = jnp.exp2(s - m)
            l = l + jnp.sum(p, axis=-1, keepdims=True)
            acc = acc + _dot(p.astype(_MXU), vv)
        og = acc / l
        for r in range(DS_REP):
            lo = (g * DS_REP + r) * DS_HD
            o_ref[:, lo:lo + DS_HD] = og[r * nq:(r + 1) * nq].astype(o_ref.dtype)


def _dsa_small(q, kn_blk, vn_blk, qi, kidn_blk, wi, cache_k, cache_v, cache_kid, nb, nq, n_new, topk):
    kvw = DS_KV * DS_HD
    has_cache = cache_k is not None
    row = lambda b: (b, 0)
    in_specs = [pl.BlockSpec((nq, DS_WIDTH), row),
                pl.BlockSpec((LANES, kvw), row), pl.BlockSpec((LANES, kvw), row),
                pl.BlockSpec((nq, IDX_HEADS * IDX_HD), row),
                pl.BlockSpec((LANES, LANES), row),
                pl.BlockSpec((nq, LANES), row)]
    args = [q, kn_blk, vn_blk, qi, kidn_blk, wi]
    if has_cache:
        past = cache_k.shape[1]
        in_specs += [pl.BlockSpec((1, past, kvw), lambda b: (b, 0, 0)),
                     pl.BlockSpec((1, past, kvw), lambda b: (b, 0, 0)),
                     pl.BlockSpec((1, past, LANES), lambda b: (b, 0, 0))]
        args += [cache_k, cache_v, cache_kid]
    return pl.pallas_call(
        functools.partial(_dsa_small_kernel, has_cache=has_cache, nq=nq, n_new=n_new, topk=topk),
        grid=(nb,),
        in_specs=in_specs,
        out_specs=pl.BlockSpec((nq, DS_WIDTH), row),
        out_shape=jax.ShapeDtypeStruct((nb * nq, DS_WIDTH), _MXU),
        compiler_params=_cparams(("parallel",)),
        name="dsa_small",
    )(*args)


def _pad_rows_front(a, rows):
    return jnp.pad(a, ((rows - a.shape[0], 0), (0, 0)))


def _pad_rows_back(a, nb, nq, rows):
    w = a.shape[1]
    return jnp.pad(a.reshape(nb, nq, w), ((0, 0), (0, rows - nq), (0, 0))).reshape(nb * rows, w)


def _transpose_with_ones(v, group_cols):
    n = v.shape[0]
    g = v.shape[1] // group_cols
    vt = v.T.reshape(g, group_cols, n)
    pad = jnp.zeros((g, ONES_ROWS, n), v.dtype).at[:, 0, :].set(1)
    return jnp.concatenate([vt, pad], axis=1).reshape(g * (group_cols + ONES_ROWS), n)


def _dup_pairs(a, hd):
    lead = a.shape[:-1]
    g = a.shape[-1] // hd
    a = a.reshape(*lead, g, 1, hd)
    return jnp.broadcast_to(a, (*lead, g, 2, hd)).reshape(*lead, g * 2 * hd)


def kernel(x_prompt, x_sample, cache_l0_k, cache_l0_v, cache_l1_k, cache_l1_v, cache_l2_k, cache_l2_v, cache_l2_kidx, cache_l3_k, cache_l3_v, meta_tokens, l0_norm, l0_w_in, l0_w_out, l0_lam_q1, l0_lam_k1, l0_lam_q2, l0_lam_k2, l0_subln, l1_norm, l1_w_in, l1_w_out, l1_sinks, l2_norm, l2_w_in, l2_w_out, l3_norm, l3_w_in, l3_w_out, l3_lam_q1, l3_lam_k1, l3_lam_q2, l3_lam_k2, l3_subln, final_norm):
    nb, seq, d = x_prompt.shape
    db, ds, _ = x_sample.shape
    past = cache_l0_k.shape[1]
    sw_rows = cache_l1_k.shape[1]
    assert seq % CHUNK == 0 and meta_tokens.shape[0] == N_META and ds <= LANES

    xs = {"f": x_prompt.reshape(nb * seq, d), "m": meta_tokens.astype(_F32), "s": x_sample.reshape(db * ds, d)}
    pos = {"f": N_META + jnp.arange(seq), "m": jnp.arange(N_META), "s": past + jnp.arange(ds)}
    pos["s"] = jnp.tile(pos["s"], db)

    tabs128 = {n: _rope_tables(p, 128, 1.0) for n, p in pos.items()}
    tabs128_da = {n: _rope_tables(p, 128, DA_HD ** -0.5 * LOG2E) for n, p in pos.items()}
    tabs128_ds = {n: _rope_tables(p, 128, DS_HD ** -0.5 * LOG2E) for n, p in pos.items()}
    tabs64 = {n: _rope_tables(p, 64, 1.0) for n, p in pos.items()}
    tabs64_sw = {n: _rope_tables(p, 64, SW_HD ** -0.5 * LOG2E) for n, p in pos.items()}

    layers = [
        dict(norm=l0_norm, w_in=l0_w_in, w_out=l0_w_out, lam=(l0_lam_q1, l0_lam_k1, l0_lam_q2, l0_lam_k2),
             subln=l0_subln, cache=(cache_l0_k, cache_l0_v)),
        dict(norm=l1_norm, w_in=l1_w_in, w_out=l1_w_out, sinks=l1_sinks, cache=(cache_l1_k, cache_l1_v)),
        dict(norm=l2_norm, w_in=l2_w_in, w_out=l2_w_out, cache=(cache_l2_k, cache_l2_v, cache_l2_kidx)),
        dict(norm=l3_norm, w_in=l3_w_in, w_out=l3_w_out, lam=(l3_lam_q1, l3_lam_k1, l3_lam_q2, l3_lam_k2),
             subln=l3_subln, cache=(cache_l3_k, cache_l3_v)),
    ]
    norms = [lp["norm"] for lp in layers[1:]] + [final_norm]
    depth = len(layers)

    hs = {n: _rms_norm_rows(x, layers[0]["norm"]) for n, x in xs.items()}
    p_st, s_st = [], []
    ys = {}
    for li, lp in enumerate(layers):
        kind = li % N_MIXERS
        w_in = lp["w_in"].astype(_MXU)
        w_out = lp["w_out"].astype(_MXU)
        o, z = {}, {}
        if kind == 0:
            lam_init = 0.8 - 0.6 * math.exp(-0.3 * li)
            wq, wk, wv, wz = (w_in[:, i * DA_WIDTH:(i + 1) * DA_WIDTH] for i in range(4))
            lams = jnp.stack([l.astype(_F32) for l in lp["lam"]])
            subln = lp["subln"].reshape(1, 2 * DA_HD).astype(_F32)
            subln_col = jnp.broadcast_to(lp["subln"].astype(_F32)[:, None], (2 * DA_HD, LANES))
            q, k32, k16, v32, v16 = {}, {}, {}, {}, {}
            for n in xs:
                (q[n],) = _project(hs[n], wq, [_MXU], *tabs128_da[n])
                k32[n], k16[n] = _project(hs[n], wk, [_F32, _MXU], *tabs128[n])
                if n == "f":
                    v32[n], vt_f = _project(hs[n], wv, [_F32], group_cols=2 * DA_HD)
                else:
                    v32[n], v16[n] = _project(hs[n], wv, [_F32, _MXU])
                (z[n],) = _project(hs[n], wz, [_MXU])
            o["f"] = _diff_frames(q["f"], k16["f"], vt_f, k16["m"], _transpose_with_ones(v16["m"], 2 * DA_HD),
                                  lams, subln_col, lam_init, nb, seq)
            o["m"] = _diff_small(q["m"], k16["m"], v16["m"], None, None, lams, subln, lam_init, 1, N_META)
            ck, cv = lp["cache"]
            o["s"] = _diff_small(q["s"], k16["s"], v16["s"], ck.reshape(db, past, DA_WIDTH),
                                 cv.reshape(db, past, DA_WIDTH), lams, subln, lam_init, db, ds)
            hshape = (DA_HEADS, 2 * DA_HD)

            def with_meta(f, mrows):
                mm = jnp.broadcast_to(mrows.reshape(1, N_META, *hshape), (nb, N_META, *hshape))
                return jnp.concatenate([mm, f.reshape(nb, seq, *hshape)], axis=1)

            p_st.append([with_meta(k32["f"], k32["m"]), with_meta(v32["f"], v32["m"])])
            s_st.append([k32["s"].reshape(db, ds, *hshape), v32["s"].reshape(db, ds, *hshape)])
        elif kind == 1:
            kvd = SW_KV * SW_HD
            wq = w_in[:, :SW_WIDTH]
            wk = w_in[:, SW_WIDTH:SW_WIDTH + kvd]
            wv = w_in[:, SW_WIDTH + kvd:SW_WIDTH + 2 * kvd]
            wz = w_in[:, SW_WIDTH + 2 * kvd:]
            wkd, wvd = _dup_pairs(wk, SW_HD), _dup_pairs(wv, SW_HD)
            sink_rows = {}
            q, kv32, kd, vd = {}, {}, {}, {}
            for n, rows in (("f", CHUNK), ("m", N_META), ("s", ds)):
                sk = (lp["sinks"].astype(_F32) * LOG2E).reshape(SW_KV, SW_REP, 1, 1)
                sink_rows[n] = jnp.broadcast_to(sk, (SW_KV, SW_REP, rows, LANES)).reshape(SW_KV, SW_REP * rows, LANES)
                (q[n],) = _project(hs[n], wq, [_MXU], *tabs64_sw[n])
                (kd[n],) = _project(hs[n], wkd, [_MXU], *tabs64[n])
                (vd[n],) = _project(hs[n], wvd, [_MXU])
                (z[n],) = _project(hs[n], wz, [_MXU])
            kmeta_blk = _pad_rows_front(kd["m"], WINDOW)
            vmeta_blk = _pad_rows_front(vd["m"], WINDOW)
            o["f"] = _swa_frames(q["f"], kd["f"], vd["f"], kmeta_blk, vmeta_blk, sink_rows["f"], nb, seq)
            o["m"] = _swa_small(q["m"], kd["m"], vd["m"], None, None, sink_rows["m"], 1, N_META)
            ck, cv = lp["cache"]
            ckd = _dup_pairs(ck.reshape(db, sw_rows, kvd), SW_HD).astype(_MXU)
            cvd = _dup_pairs(cv.reshape(db, sw_rows, kvd), SW_HD).astype(_MXU)
            o["s"] = _swa_small(q["s"], kd["s"], vd["s"], ckd, cvd, sink_rows["s"], db, ds)
            assert seq >= sw_rows and sw_rows % 8 == 0
            h_tail = hs["f"].reshape(nb, seq, d)[:, seq - sw_rows:].reshape(nb * sw_rows, d)
            tail_pos = N_META + seq - sw_rows + jnp.arange(sw_rows)
            tail_tabs, tail_shift = _rope_tables(tail_pos, 64, 1.0)
            (kt,) = _project(h_tail, _pad_cols(wk), [_F32], tail_tabs, tail_shift)
            (vt,) = _project(h_tail, _pad_cols(wv), [_F32])
            p_st.append([kt[:, :kvd].reshape(nb, sw_rows, SW_KV, SW_HD), vt[:, :kvd].reshape(nb, sw_rows, SW_KV, SW_HD)])
            (ks,) = _project(hs["s"], _pad_cols(wk), [_F32], *tabs64["s"])
            (vs_,) = _project(hs["s"], _pad_cols(wv), [_F32])
            ks = ks[:, :kvd].reshape(db, ds, SW_KV, SW_HD)
            vs_ = vs_[:, :kvd].reshape(db, ds, SW_KV, SW_HD)
            k_all = jnp.concatenate([ck, ks], axis=1)
            v_all = jnp.concatenate([cv, vs_], axis=1)
            s_st.append([k_all[:, k_all.shape[1] - sw_rows:], v_all[:, v_all.shape[1] - sw_rows:]])
        else:
            kvd = DS_KV * DS_HD
            c0 = 0
            wq = w_in[:, c0:c0 + DS_WIDTH]; c0 += DS_WIDTH
            wk = w_in[:, c0:c0 + kvd]; c0 += kvd
            wv = w_in[:, c0:c0 + kvd]; c0 += kvd
            wz = w_in[:, c0:c0 + DS_WIDTH]; c0 += DS_WIDTH
            wqi = w_in[:, c0:c0 + IDX_HEADS * IDX_HD]; c0 += IDX_HEADS * IDX_HD
            wki = w_in[:, c0:c0 + IDX_HD]; c0 += IDX_HD
            wwi = w_in[:, c0:c0 + IDX_HEADS]
            wkid = jnp.concatenate([wki, wki], axis=1)
            wwi_p = _pad_cols(wwi)
            wi_scale = IDX_HEADS ** -0.5 * IDX_HD ** -0.5
            q, k32, k16, v32, v16, qi, kid32, kid16, wi = {}, {}, {}, {}, {}, {}, {}, {}, {}
            for n in xs:
                (q[n],) = _project(hs[n], wq, [_MXU], *tabs128_ds[n])
                k32[n], k16[n] = _project(hs[n], wk, [_F32, _MXU], *tabs128[n])
                if n == "f":
                    v32[n], vt_f = _project(hs[n], wv, [_F32], group_cols=DS_HD)
                else:
                    v32[n], v16[n] = _project(hs[n], wv, [_F32, _MXU])
                (z[n],) = _project(hs[n], wz, [_MXU])
                (qi[n],) = _project(hs[n], wqi, [_MXU], *tabs64[n])
                kid32[n], kid16[n] = _project(hs[n], wkid, [_F32, _MXU], *tabs64[n])
                (wi[n],) = _project(hs[n], wwi_p, [_F32], scale=wi_scale)
            topk_p = min(TOPK_MAX, seq // 4)
            topk_s = min(TOPK_MAX, (past + ds) // 4)
            km_blk = _pad_rows_back(k16["m"], 1, N_META, LANES)
            vm_blk = _pad_rows_back(v16["m"], 1, N_META, LANES)
            kidm_blk = _pad_rows_back(kid16["m"], 1, N_META, LANES)
            o["f"] = _dsa_frames(q["f"], k16["f"], vt_f, qi["f"], kid16["f"], wi["f"],
                                 km_blk, _transpose_with_ones(vm_blk, DS_HD), kidm_blk, nb, seq, topk_p)
            o["m"] = _dsa_small(q["m"], km_blk, vm_blk, qi["m"], kidm_blk, wi["m"], None, None, None,
                                1, N_META, N_META, topk_p)
            ck, cv, cki = lp["cache"]
            ckid = jnp.concatenate([cki, cki], axis=-1).astype(_MXU)
            o["s"] = _dsa_small(q["s"], _pad_rows_back(k16["s"], db, ds, LANES), _pad_rows_back(v16["s"], db, ds, LANES),
                                qi["s"], _pad_rows_back(kid16["s"], db, ds, LANES), wi["s"],
                                ck.reshape(db, past, kvd), cv.reshape(db, past, kvd), ckid,
                                db, ds, ds, topk_s)

            def with_meta(f, mrows, *tail):
                mm = jnp.broadcast_to(mrows.reshape(1, N_META, *tail), (nb, N_META, *tail))
                return jnp.concatenate([mm, f.reshape(nb, seq, *tail)], axis=1)

            p_st.append([with_meta(k32["f"], k32["m"], DS_KV, DS_HD), with_meta(v32["f"], v32["m"], DS_KV, DS_HD),
                         with_meta(kid32["f"][:, :IDX_HD], kid32["m"][:, :IDX_HD], IDX_HD)])
            s_st.append([k32["s"].reshape(db, ds, DS_KV, DS_HD), v32["s"].reshape(db, ds, DS_KV, DS_HD),
                         kid32["s"][:, :IDX_HD].reshape(db, ds, IDX_HD)])
        last = li == depth - 1
        for n in xs:
            res = _out_project(o[n], z[n], w_out, xs[n], norms[li], last)
            if last:
                (ys[n],) = res
            else:
                xs[n], hs[n] = res

    y_prompt = ys["f"].reshape(nb, seq, d)
    y_sample = ys["s"].reshape(db, ds, d)
    return (y_prompt, y_sample,
            p_st[0][0], p_st[0][1], s_st[0][0], s_st[0][1],
            p_st[1][0], p_st[1][1], s_st[1][0], s_st[1][1],
            p_st[2][0], p_st[2][1], p_st[2][2], s_st[2][0], s_st[2][1], s_st[2][2],
            p_st[3][0], p_st[3][1], s_st[3][0], s_st[3][1])


def _pad_cols(w):
    n = w.shape[1]
    return jnp.pad(w, ((0, 0), (0, (-n) % LANES)))
```

```python
import functools
import math

import jax
import jax.numpy as jnp
from jax import lax
from jax.experimental import pallas as pl
from jax.experimental.pallas import tpu as pltpu

CHUNK = 64
N_META = 16
ROPE_THETA = 500000.0
NORM_EPS = 1e-6
NEG_INF = -1e30
DA_HEADS = 8
DA_HD = 128
DA_WIDTH = DA_HEADS * 2 * DA_HD
SW_HEADS = 32
SW_KV = 4
SW_HD = 64
SW_WIDTH = SW_HEADS * SW_HD
SW_REP = SW_HEADS // SW_KV
WINDOW = 128
DS_HEADS = 16
DS_KV = 4
DS_HD = 128
DS_WIDTH = DS_HEADS * DS_HD
DS_REP = DS_HEADS // DS_KV
IDX_HEADS = 16
IDX_HD = 64
TOPK_MAX = 256
N_MIXERS = 3

LANES = 128
INT_MIN = -(2 ** 31)
LOG2E = math.log2(math.e)
HALF16 = 32768
ONES_ROWS = 16
VMEM_LIMIT = 56 * 1024 * 1024

_MXU = jnp.bfloat16
_F32 = jnp.float32

_NT = (((1,), (1,)), ((), ()))


def _cparams(sem):
    return pltpu.CompilerParams(dimension_semantics=sem, vmem_limit_bytes=VMEM_LIMIT)


def _row_tile(m, pref):
    t = pref
    while t > 8 and m % t:
        t //= 2
    assert m % t == 0, (m, pref)
    return t


def _dot(a, b):
    return jnp.dot(a, b, preferred_element_type=_F32)


def _dot_nt(a, b):
    return lax.dot_general(a, b, _NT, preferred_element_type=_F32)


def _norm_kernel(x_ref, g_ref, h_ref):
    x = x_ref[...]
    y = x * lax.rsqrt(jnp.mean(x * x, axis=-1, keepdims=True) + NORM_EPS)
    h_ref[...] = (y * g_ref[...]).astype(h_ref.dtype)


def _rms_norm_rows(x, g):
    m, d = x.shape
    out_dtype = _MXU
    tm = _row_tile(m, 512)
    return pl.pallas_call(
        _norm_kernel,
        grid=(m // tm,),
        in_specs=[pl.BlockSpec((tm, d), lambda i: (i, 0)),
                  pl.BlockSpec((1, d), lambda i: (0, 0))],
        out_specs=pl.BlockSpec((tm, d), lambda i: (i, 0)),
        out_shape=jax.ShapeDtypeStruct((m, d), out_dtype),
        compiler_params=_cparams(("parallel",)),
        name="rms_norm",
    )(x, g.reshape(1, d).astype(_F32))


def _proj_kernel(*refs, rope_shift, scale, n_out, tn, group_cols):
    h_ref, w_ref = refs[0], refs[1]
    pos = 2
    if rope_shift is not None:
        c_ref, a_ref, b_ref = refs[2:5]
        pos = 5
    outs = refs[pos:pos + n_out]
    ot_ref = refs[pos + n_out] if group_cols else None
    if ot_ref is not None:
        tm = ot_ref.shape[2]
        row = lax.broadcasted_iota(jnp.int32, (ONES_ROWS, tm), 0)
        ones_blk = jnp.where(row == 0, 1.0, 0.0).astype(ot_ref.dtype)
        for g in range(tn // group_cols):
            base = g * (group_cols + ONES_ROWS) + group_cols
            ot_ref[0, base:base + ONES_ROWS, :] = ones_blk
    h = h_ref[...]
    cw = 2 * LANES if tn % (2 * LANES) == 0 else LANES
    for c in range(tn // cw):
        wide = _dot(h, w_ref[:, c * cw:(c + 1) * cw])
        for sub in range(cw // LANES):
            acc = wide[:, sub * LANES:(sub + 1) * LANES]
            if rope_shift is not None:
                acc = (acc * c_ref[...]
                       + pltpu.roll(acc, rope_shift, 1) * a_ref[...]
                       + pltpu.roll(acc, LANES - rope_shift, 1) * b_ref[...])
            elif scale is not None:
                acc = acc * scale
            cols = slice(c * cw + sub * LANES, c * cw + (sub + 1) * LANES)
            for o in outs:
                o[:, cols] = acc.astype(o.dtype)
            if ot_ref is not None:
                lo = cols.start
                r0 = (lo // group_cols) * (group_cols + ONES_ROWS) + lo % group_cols
                ot_ref[0, r0:r0 + LANES, :] = acc.T.astype(ot_ref.dtype)


def _project(h, w, out_dtypes, tabs=None, rope_shift=None, scale=None, group_cols=0):
    m, d = h.shape
    n = w.shape[1]
    tm = _row_tile(m if tabs is None else math.gcd(m, tabs[0].shape[0]), 512)
    tn = n if n <= 2048 else 2048
    assert n % tn == 0 and tn % LANES == 0
    in_specs = [pl.BlockSpec((tm, d), lambda j, i: (i, 0)),
                pl.BlockSpec((d, tn), lambda j, i: (0, j))]
    args = [h, w]
    if tabs is not None:
        trows = tabs[0].shape[0]
        assert trows % tm == 0
        nrep = trows // tm
        for t in tabs:
            in_specs.append(pl.BlockSpec((tm, LANES), lambda j, i: (i % nrep, 0)))
            args.append(t)
    else:
        rope_shift = None
    out_specs = [pl.BlockSpec((tm, tn), lambda j, i: (i, j)) for _ in out_dtypes]
    out_shape = [jax.ShapeDtypeStruct((m, n), dt) for dt in out_dtypes]
    if group_cols:
        assert tn % group_cols == 0 and group_cols % LANES == 0
        grow = group_cols + ONES_ROWS
        out_specs.append(pl.BlockSpec((1, tn // group_cols * grow, tm), lambda j, i: (i, j, 0)))
        out_shape.append(jax.ShapeDtypeStruct((m // tm, n // group_cols * grow, tm), _MXU))
    outs = pl.pallas_call(
        functools.partial(_proj_kernel, rope_shift=rope_shift, scale=scale,
                          n_out=len(out_dtypes), tn=tn, group_cols=group_cols),
        grid=(n // tn, m // tm),
        in_specs=in_specs,
        out_specs=out_specs,
        out_shape=out_shape,
        compiler_params=_cparams(("parallel", "parallel")),
        name="in_proj",
    )(*args)
    return outs


def _outproj_kernel(o_ref, z_ref, w_ref, x_ref, g_ref, *outs, emit_x):
    z = z_ref[...].astype(_F32)
    a = o_ref[...].astype(_F32) * (z / (1.0 + jnp.exp(-z)))
    y = x_ref[...] + _dot(a.astype(w_ref.dtype), w_ref[...])
    if emit_x:
        outs[0][...] = y
    n = y * lax.rsqrt(jnp.mean(y * y, axis=-1, keepdims=True) + NORM_EPS) * g_ref[...]
    outs[-1][...] = n.astype(outs[-1].dtype)


def _out_project(o, z, w, x, g_next, last):
    m, d = x.shape
    wd = o.shape[1]
    tm = _row_tile(m, 512)
    row = lambda i: (i, 0)
    fixed = lambda i: (0, 0)
    if last:
        out_shape = [jax.ShapeDtypeStruct((m, d), _F32)]
    else:
        out_shape = [jax.ShapeDtypeStruct((m, d), _F32), jax.ShapeDtypeStruct((m, d), _MXU)]
    return pl.pallas_call(
        functools.partial(_outproj_kernel, emit_x=not last),
        grid=(m // tm,),
        in_specs=[pl.BlockSpec((tm, wd), row), pl.BlockSpec((tm, wd), row),
                  pl.BlockSpec((wd, d), fixed), pl.BlockSpec((tm, d), row),
                  pl.BlockSpec((1, d), fixed)],
        out_specs=[pl.BlockSpec((tm, d), row) for _ in out_shape],
        out_shape=out_shape,
        compiler_params=_cparams(("parallel",)),
        name="out_proj",
    )(o, z, w, x, g_next.reshape(1, d).astype(_F32))


def _rope_tables(pos, hd, scale=1.0):
    rd = hd // 4
    half = rd // 2
    lane = jnp.arange(LANES)
    d = lane % hd
    f = d % half
    inv_freq = ROPE_THETA ** (-(jnp.arange(half, dtype=_F32) * 2.0 / rd))
    ang = pos.astype(_F32)[:, None] * inv_freq[f][None, :]
    cos, sin = jnp.cos(ang), jnp.sin(ang)
    in_rot = (d < rd)[None, :]
    c = jnp.where(in_rot, cos, 1.0)
    a = jnp.where(((d >= half) & (d < rd))[None, :], sin, 0.0)
    b = jnp.where((d < half)[None, :], -sin, 0.0)
    s = jnp.float32(scale)
    return (c * s, a * s, b * s), half


def _lam_value(lam_ref, lam_init):
    l = lam_ref[...]
    s1 = jnp.sum(l[0:1] * l[1:2], axis=-1, keepdims=True)
    s2 = jnp.sum(l[2:3] * l[3:4], axis=-1, keepdims=True)
    return jnp.exp(s1) - jnp.exp(s2) + lam_init


def _diff_finish(acc1, l1, acc2, l2, lam, sub, lam_init):
    o = acc1 / l1 - lam * (acc2 / l2)
    o = o * lax.rsqrt(jnp.mean(o * o, axis=-1, keepdims=True) + NORM_EPS) * sub
    return o * (1.0 - lam_init)


def _diff_frames_kernel(q_ref, k_ref, vt_ref, km_ref, vmt_ref, lam_ref, subc_ref, o_ref,
                        acc_s, sa_s, sb_s, pa_s, pb_s, *, tq, lam_init):
    i = pl.program_id(2)
    hw = 2 * DA_HD

    def q_half(j):
        return q_ref[:, j * DA_HD:(j + 1) * DA_HD]

    km = km_ref[...]
    vmt = vmt_ref[...]
    ms = []
    for j in range(2):
        s = _dot_nt(km[:, j * DA_HD:(j + 1) * DA_HD], q_half(j))
        m = jnp.max(s, axis=0, keepdims=True)
        ms.append(m)
        acc_s[j] = _dot(vmt, jnp.exp2(s - m).astype(vmt.dtype))

    def scores_into(dst, t):
        start = pl.multiple_of(t * tq, tq)
        for j in range(2):
            dst[j] = _dot_nt(k_ref[pl.ds(start, tq), j * DA_HD:(j + 1) * DA_HD], q_half(j))

    def value_update(t, a, p_rd):
        vt = vt_ref[t]
        for j in range(2):
            acc_s[j] = a[j] * acc_s[j] + _dot(vt, p_rd[j])

    def tile_step(t, carry, src, dst, p_rd, p_wr, masked, has_next):
        m, a = carry
        if masked:
            kc = lax.broadcasted_iota(jnp.int32, (tq, tq), 0) // CHUNK
            qc = lax.broadcasted_iota(jnp.int32, (tq, tq), 1) // CHUNK
            ok = kc <= qc
        m_out, a_out = [], []
        for j in range(2):
            s = src[j]
            if masked:
                s = jnp.where(ok, s, NEG_INF)
            m_new = jnp.maximum(m[j], jnp.max(s, axis=0, keepdims=True))
            a_out.append(jnp.exp2(m[j] - m_new))
            m_out.append(m_new)
            p_wr[j] = jnp.exp2(s - m_new).astype(p_wr.dtype)
        value_update(jnp.maximum(t - 1, 0), a, p_rd)
        if has_next:
            scores_into(dst, t + 1)
        return tuple(m_out), tuple(a_out)

    scores_into(sa_s, 0)
    pb_s[...] = jnp.zeros_like(pb_s)
    one = jnp.ones((1, tq), _F32)
    carry = ((ms[0], ms[1]), (one, one))

    def pair_body(u, carry):
        carry = tile_step(2 * u, carry, sa_s, sb_s, pb_s, pa_s, False, True)
        return tile_step(2 * u + 1, carry, sb_s, sa_s, pa_s, pb_s, False, True)

    carry = lax.fori_loop(0, i // 2, pair_body, carry)

    @pl.when(i % 2 == 0)
    def _():
        _, a = tile_step(i, carry, sa_s, sb_s, pb_s, pa_s, True, False)
        value_update(i, a, pa_s)

    @pl.when(i % 2 == 1)
    def _():
        c = tile_step(i - 1, carry, sa_s, sb_s, pb_s, pa_s, False, True)
        _, a = tile_step(i, c, sb_s, sa_s, pa_s, pb_s, True, False)
        value_update(i, a, pb_s)

    acc0 = acc_s[0]
    acc1 = acc_s[1]
    lam = _lam_value(lam_ref, lam_init)
    inv0 = 1.0 / acc0[hw:hw + 1]
    inv1 = lam / acc1[hw:hw + 1]
    ot = acc0[:hw] * inv0 - acc1[:hw] * inv1
    sub = jnp.concatenate([subc_ref[...]] * (tq // LANES), axis=1)
    ot = ot * lax.rsqrt(jnp.mean(ot * ot, axis=0, keepdims=True) + NORM_EPS) * sub * (1.0 - lam_init)
    o_ref[...] = ot.T.astype(o_ref.dtype)


def _diff_frames(q, k, vt, km, vmt, lams, subln_col, lam_init, nb, seq):
    m = q.shape[0]
    hw = 2 * DA_HD
    hr = hw + ONES_ROWS
    tq = vt.shape[2]
    assert seq % tq == 0 and tq % LANES == 0
    nt = seq // tq
    return pl.pallas_call(
        functools.partial(_diff_frames_kernel, tq=tq, lam_init=lam_init),
        grid=(nb, DA_HEADS, nt),
        in_specs=[pl.BlockSpec((tq, hw), lambda b, h, i: (b * nt + i, h)),
                  pl.BlockSpec((seq, hw), lambda b, h, i: (b, h)),
                  pl.BlockSpec((nt, hr, tq), lambda b, h, i: (b, h, 0)),
                  pl.BlockSpec((N_META, hw), lambda b, h, i: (0, h)),
                  pl.BlockSpec((hr, N_META), lambda b, h, i: (h, 0)),
                  pl.BlockSpec((4, DA_HD), lambda b, h, i: (0, 0)),
                  pl.BlockSpec((hw, LANES), lambda b, h, i: (0, 0))],
        out_specs=pl.BlockSpec((tq, hw), lambda b, h, i: (b * nt + i, h)),
        out_shape=jax.ShapeDtypeStruct((m, DA_WIDTH), _MXU),
        scratch_shapes=[pltpu.VMEM((2, hr, tq), _F32),
                        pltpu.VMEM((2, tq, tq), _F32), pltpu.VMEM((2, tq, tq), _F32),
                        pltpu.VMEM((2, tq, tq), _MXU), pltpu.VMEM((2, tq, tq), _MXU)],
        compiler_params=_cparams(("parallel", "parallel", "arbitrary")),
        name="diff_frames",
    )(q, k, vt, km, vmt, lams, subln_col)


def _diff_small_kernel(*refs, has_cache, lam_init):
    if has_cache:
        q_ref, k_ref, v_ref, ck_ref, cv_ref, lam_ref, sub_ref, o_ref = refs
    else:
        q_ref, k_ref, v_ref, lam_ref, sub_ref, o_ref = refs
    q = q_ref[...]
    kn = k_ref[...]
    vn = v_ref[...]
    parts = [(kn, vn)]
    if has_cache:
        parts.append((ck_ref[0].astype(kn.dtype), cv_ref[0].astype(vn.dtype)))
    accs, ls = [], []
    for j in range(2):
        qj = q[:, j * DA_HD:(j + 1) * DA_HD]
        ss = [_dot_nt(qj, kp[:, j * DA_HD:(j + 1) * DA_HD]) for kp, _ in parts]
        m = ss[0].max(axis=-1, keepdims=True)
        for s in ss[1:]:
            m = jnp.maximum(m, s.max(axis=-1, keepdims=True))
        l = 0.0
        acc = 0.0
        for s, (_, vp) in zip(ss, parts):
            p = jnp.exp2(s - m)
            l = l + jnp.sum(p, axis=-1, keepdims=True)
            acc = acc + _dot(p.astype(vp.dtype), vp)
        accs.append(acc)
        ls.append(l)
    lam = _lam_value(lam_ref, lam_init)
    o = _diff_finish(accs[0], ls[0], accs[1], ls[1], lam, sub_ref[...], lam_init)
    o_ref[...] = o.astype(o_ref.dtype)


def _diff_small(q, k, v, cache_k, cache_v, lams, subln, lam_init, nb, nq):
    hw = 2 * DA_HD
    has_cache = cache_k is not None
    qspec = pl.BlockSpec((nq, hw), lambda b, h: (b, h))
    in_specs = [qspec, qspec, qspec]
    args = [q, k, v]
    if has_cache:
        past = cache_k.shape[1]
        cspec = pl.BlockSpec((1, past, hw), lambda b, h: (b, 0, h))
        in_specs += [cspec, cspec]
        args += [cache_k, cache_v]
    in_specs += [pl.BlockSpec((4, DA_HD), lambda b, h: (0, 0)),
                 pl.BlockSpec((1, hw), lambda b, h: (0, 0))]
    args += [lams, subln]
    return pl.pallas_call(
        functools.partial(_diff_small_kernel, has_cache=has_cache, lam_init=lam_init),
        grid=(nb, DA_HEADS),
        in_specs=in_specs,
        out_specs=qspec,
        out_shape=jax.ShapeDtypeStruct((nb * nq, DA_WIDTH), _MXU),
        compiler_params=_cparams(("parallel", "parallel")),
        name="diff_small",
    )(*args)


def _stack_heads(qg, nheads):
    n = qg.shape[0]
    lane = lax.broadcasted_iota(jnp.int32, (n, LANES), 1)
    lo = lane < SW_HD
    pieces = []
    for r in range(nheads):
        blk = qg[:, (r // 2) * LANES:(r // 2 + 1) * LANES]
        keep = lo if r % 2 == 0 else jnp.logical_not(lo)
        pieces.append(jnp.where(keep, blk, jnp.zeros_like(blk)))
    return jnp.concatenate(pieces, axis=0)


def _unstack_heads(o, n, nheads):
    lane = lax.broadcasted_iota(jnp.int32, (n, LANES), 1)
    lo = lane < SW_HD
    return [jnp.where(lo, o[(2 * p) * n:(2 * p + 1) * n], o[(2 * p + 1) * n:(2 * p + 2) * n])
            for p in range(nheads // 2)]


def _sink_softmax_pv(ss, vs, sk):
    m = sk
    for s in ss:
        m = jnp.maximum(m, jnp.max(s, axis=-1, keepdims=True))
    den = jnp.exp2(sk - m)
    acc = 0.0
    for s, v in zip(ss, vs):
        e = jnp.exp2(s - m)
        den = den + jnp.sum(e, axis=-1, keepdims=True)
        acc = acc + _dot(e.astype(v.dtype), v)
    return acc / den


def _swa_frames_kernel(q_ref, kc_ref, vc_ref, kp_ref, vp_ref, km_ref, vm_ref, sk_ref, o_ref,
                       kcat, vcat, *, tq):
    i = pl.program_id(1)
    first = i == 0
    prev_rows = WINDOW
    kcat[0:prev_rows] = jnp.where(first, km_ref[...], kp_ref[...])
    vcat[0:prev_rows] = jnp.where(first, vm_ref[...], vp_ref[...])
    kcat[prev_rows:prev_rows + tq] = kc_ref[...]
    vcat[prev_rows:prev_rows + tq] = vc_ref[...]
    nwin = WINDOW + CHUNK
    for c in range(tq // CHUNK):
        rows = slice(c * CHUNK, (c + 1) * CHUNK)
        win = slice(c * CHUNK, c * CHUNK + nwin)
        for g in range(SW_KV):
            gl = slice(g * LANES, (g + 1) * LANES)
            qst = _stack_heads(q_ref[rows, g * SW_REP * SW_HD:(g + 1) * SW_REP * SW_HD], SW_REP)
            s = _dot_nt(qst, kcat[win, gl])
            if c * CHUNK < prev_rows - N_META:
                col = lax.broadcasted_iota(jnp.int32, s.shape, 1) + c * CHUNK
                ok = jnp.logical_or(col >= prev_rows - N_META, jnp.logical_not(first))
                s = jnp.where(ok, s, NEG_INF)
            o = _sink_softmax_pv([s], [vcat[win, gl]], sk_ref[g][:, 0:1])
            for p, blk in enumerate(_unstack_heads(o, CHUNK, SW_REP)):
                lo = g * SW_REP * SW_HD + p * LANES
                o_ref[rows, lo:lo + LANES] = blk.astype(o_ref.dtype)


def _swa_frames(q, kd, vd, kmeta_blk, vmeta_blk, sink_rows, nb, seq):
    m = q.shape[0]
    tq = _row_tile(seq, 512)
    nt = seq // tq
    pw = SW_KV * LANES
    per = tq // WINDOW
    cur = lambda b, i: (b * nt + i, 0)
    prev = lambda b, i: (jnp.maximum((b * nt + i) * per - 1, 0), 0)
    fixed = lambda b, i: (0, 0)
    return pl.pallas_call(
        functools.partial(_swa_frames_kernel, tq=tq),
        grid=(nb, nt),
        in_specs=[pl.BlockSpec((tq, SW_WIDTH), cur),
                  pl.BlockSpec((tq, pw), cur), pl.BlockSpec((tq, pw), cur),
                  pl.BlockSpec((WINDOW, pw), prev), pl.BlockSpec((WINDOW, pw), prev),
                  pl.BlockSpec((WINDOW, pw), fixed), pl.BlockSpec((WINDOW, pw), fixed),
                  pl.BlockSpec((SW_KV, SW_REP * CHUNK, LANES), lambda b, i: (0, 0, 0))],
        out_specs=pl.BlockSpec((tq, SW_WIDTH), cur),
        out_shape=jax.ShapeDtypeStruct((m, SW_WIDTH), _MXU),
        scratch_shapes=[pltpu.VMEM((WINDOW + tq, pw), _MXU), pltpu.VMEM((WINDOW + tq, pw), _MXU)],
        compiler_params=_cparams(("parallel", "arbitrary")),
        name="swa_frames",
    )(q, kd, vd, kd, vd, kmeta_blk, vmeta_blk, sink_rows)


def _swa_small_kernel(*refs, has_cache, nq):
    if has_cache:
        q_ref, k_ref, v_ref, ck_ref, cv_ref, sk_ref, o_ref = refs
    else:
        q_ref, k_ref, v_ref, sk_ref, o_ref = refs
    for g in range(SW_KV):
        gl = slice(g * LANES, (g + 1) * LANES)
        qst = _stack_heads(q_ref[:, g * SW_REP * SW_HD:(g + 1) * SW_REP * SW_HD], SW_REP)
        ks, vs = [k_ref[:, gl]], [v_ref[:, gl]]
        if has_cache:
            ks.append(ck_ref[0][:, gl])
            vs.append(cv_ref[0][:, gl])
        ss = [_dot_nt(qst, kk) for kk in ks]
        o = _sink_softmax_pv(ss, vs, sk_ref[g][:, 0:1])
        for p, blk in enumerate(_unstack_heads(o, nq, SW_REP)):
            lo = g * SW_REP * SW_HD + p * LANES
            o_ref[:, lo:lo + LANES] = blk.astype(o_ref.dtype)


def _swa_small(q, kd, vd, cache_kd, cache_vd, sink_rows, nb, nq):
    pw = SW_KV * LANES
    has_cache = cache_kd is not None
    row = lambda b: (b, 0)
    in_specs = [pl.BlockSpec((nq, SW_WIDTH), row), pl.BlockSpec((nq, pw), row), pl.BlockSpec((nq, pw), row)]
    args = [q, kd, vd]
    if has_cache:
        past = cache_kd.shape[1]
        cspec = pl.BlockSpec((1, past, pw), lambda b: (b, 0, 0))
        in_specs += [cspec, cspec]
        args += [cache_kd, cache_vd]
    in_specs.append(pl.BlockSpec((SW_KV, SW_REP * nq, LANES), lambda b: (0, 0, 0)))
    args.append(sink_rows)
    return pl.pallas_call(
        functools.partial(_swa_small_kernel, has_cache=has_cache, nq=nq),
        grid=(nb,),
        in_specs=in_specs,
        out_specs=pl.BlockSpec((nq, SW_WIDTH), row),
        out_shape=jax.ShapeDtypeStruct((nb * nq, SW_WIDTH), _MXU),
        compiler_params=_cparams(("parallel",)),
        name="swa_small",
    )(*args)


def _order_key(score, valid):
    bits = lax.bitcast_convert_type(score + 0.0, jnp.int32)
    key = jnp.where(bits < 0, bits ^ jnp.int32(0x7FFFFFFF), bits)
    return jnp.where(valid, key, jnp.int32(INT_MIN))


def _stack_idx_heads(qi):
    return _stack_heads(qi, IDX_HEADS)


def _index_scores(qst, wb_of, kid, n):
    score = None
    hb = 4
    for h0 in range(0, IDX_HEADS, hb):
        sc = _dot_nt(qst[h0 * n:(h0 + hb) * n], kid)
        for h in range(h0, h0 + hb):
            part = jnp.maximum(sc[(h - h0) * n:(h - h0 + 1) * n], 0.0) * wb_of(h)
            score = part if score is None else score + part
    return score


def _index_scores_t(kid, qst, wit, n):
    score = None
    hb = 4
    for h0 in range(0, IDX_HEADS, hb):
        sc = _dot_nt(kid, qst[h0 * n:(h0 + hb) * n])
        for h in range(h0, h0 + hb):
            part = jnp.maximum(sc[:, (h - h0) * n:(h - h0 + 1) * n], 0.0) * wit[h:h + 1, :]
            score = part if score is None else score + part
    return score


def _dsa_frames_kernel(q_ref, k_ref, vt_ref, qi_ref, kid_ref, wi_ref, km_ref, vmt_ref, kidm_ref, o_ref,
                       keym_s, key_s, him_s, hi_s, lom_s, lo_s, biasm_s, bias_s, qst_s, wit_s, qg_s, m_s, acc_s,
                       *, tq, tk, topk):
    i = pl.program_id(1)
    n_t = (i * tq + tq + tk - 1) // tk
    qst_s[...] = _stack_idx_heads(qi_ref[...])
    wit_s[...] = wi_ref[...].T
    sc_m = _index_scores_t(kidm_ref[...], qst_s[...], wit_s[...], tq)
    row_m = lax.broadcasted_iota(jnp.int32, (LANES, tq), 0)
    keym_s[...] = _order_key(sc_m, row_m < N_META)

    q_chunk = (lax.broadcasted_iota(jnp.int32, (tk, tq), 1) + i * tq) // CHUNK

    def split16(key):
        hi = lax.shift_right_arithmetic(key, 16).astype(jnp.int16)
        lo = ((key & 0xFFFF) - HALF16).astype(jnp.int16)
        return hi, lo

    him_s[...], lom_s[...] = split16(keym_s[...])

    def score_body(t, carry):
        start = pl.multiple_of(t * tk, tk)
        sc = _index_scores_t(kid_ref[pl.ds(start, tk), :], qst_s[...], wit_s[...], tq)
        k_chunk = (lax.broadcasted_iota(jnp.int32, (tk, tq), 0) + t * tk) // CHUNK
        key = _order_key(sc, k_chunk <= q_chunk)
        key_s[t] = key
        hi_s[t], lo_s[t] = split16(key)
        return carry

    lax.fori_loop(0, n_t, score_body, 0)

    sub16 = 16

    def rows16(c32):
        return jnp.broadcast_to(c32, (sub16, tq)).astype(jnp.int16)[None]

    def count16(pm, ps, pred):
        def fold(x):
            x3 = x.reshape(x.shape[0] // sub16, sub16, tq)
            c = jnp.where(pred(x3), jnp.int16(1), jnp.int16(0)).reshape(x.shape[0] // LANES, LANES, tq)
            out = c[0]
            for r in range(1, c.shape[0]):
                out = out + c[r]
            return out

        part = fold(pm[...])
        part = lax.fori_loop(0, n_t, lambda t, acc: acc + fold(ps[t]), part)
        return jnp.sum(part.astype(jnp.int32), axis=0, keepdims=True)

    def search16(pm, ps, base):
        def bit_body(it, u):
            cand = u | lax.shift_left(jnp.int32(1), jnp.int32(15) - it)
            c16 = rows16(cand - HALF16)
            cnt = base + count16(pm, ps, lambda x: x >= c16)
            return jnp.where(cnt >= topk, cand, u)

        return lax.fori_loop(0, 16, bit_body, jnp.zeros((1, tq), jnp.int32))

    u_hi = search16(him_s, hi_s, 0)
    t_hi = rows16(u_hi - HALF16)
    above = count16(him_s, hi_s, lambda x: x > t_hi)

    def keep_boundary(hi, lo):
        hi3 = hi.reshape(hi.shape[0] // sub16, sub16, tq)
        lo3 = lo.reshape(hi3.shape)
        return jnp.where(hi3 == t_hi, lo3, jnp.int16(-HALF16)).reshape(hi.shape)

    lom_s[...] = keep_boundary(him_s[...], lom_s[...])

    def boundary_body(t, carry):
        lo_s[t] = keep_boundary(hi_s[t], lo_s[t])
        return carry

    lax.fori_loop(0, n_t, boundary_body, 0)
    u_lo = search16(lom_s, lo_s, above)
    thr = lax.shift_left(u_hi - HALF16, 16) | u_lo
    thr = jnp.maximum(thr, jnp.int32(INT_MIN + 1))

    biasm_s[...] = jnp.where(keym_s[...] >= thr, 0.0, NEG_INF)

    def bias_body(t, carry):
        bias_s[t] = jnp.where(key_s[t] >= thr, 0.0, NEG_INF)
        return carry

    lax.fori_loop(0, n_t, bias_body, 0)

    vr = DS_HD + ONES_ROWS
    groups = [(g, slice(g * DS_HD, (g + 1) * DS_HD), slice(g * vr, (g + 1) * vr)) for g in range(DS_KV)]
    for g, gl, gv in groups:
        qg_s[g] = jnp.concatenate(
            [q_ref[:, (g * DS_REP + r) * DS_HD:(g * DS_REP + r + 1) * DS_HD] for r in range(DS_REP)], axis=0)
        s = _dot_nt(km_ref[:, gl], qg_s[g]) + jnp.concatenate([biasm_s[...]] * DS_REP, axis=1)
        m0 = jnp.max(s, axis=0, keepdims=True)
        m_s[g] = m0
        acc_s[g] = _dot(vmt_ref[gv, :], jnp.exp2(s - m0).astype(_MXU))

    def att_body(t, carry):
        start = pl.multiple_of(t * tk, tk)
        bias = jnp.concatenate([bias_s[t]] * DS_REP, axis=1)
        for pair in (groups[:2], groups[2:]):
            ss = [_dot_nt(k_ref[pl.ds(start, tk), gl], qg_s[g]) for g, gl, _ in pair]
            for (g, _, gv), s in zip(pair, ss):
                s = s + bias
                m_prev = m_s[g]
                m_new = jnp.maximum(m_prev, jnp.max(s, axis=0, keepdims=True))
                alpha = jnp.exp2(m_prev - m_new)
                p = jnp.exp2(s - m_new).astype(_MXU)
                acc_s[g] = alpha * acc_s[g] + _dot(vt_ref[t, gv, :], p)
                m_s[g] = m_new
        return carry

    lax.fori_loop(0, n_t, att_body, 0)
    for g in range(DS_KV):
        acc = acc_s[g]
        ot = acc[:DS_HD] * (1.0 / acc[DS_HD:DS_HD + 1])
        for r in range(DS_REP):
            lo = (g * DS_REP + r) * DS_HD
            o_ref[:, lo:lo + DS_HD] = ot[:, r * tq:(r + 1) * tq].T.astype(o_ref.dtype)


def _dsa_frames(q, k, vt, qi, kid, wi, km_blk, vmt_blk, kidm_blk, nb, seq, topk):
    m = q.shape[0]
    tq = _row_tile(seq, 128)
    tk = vt.shape[2]
    assert seq % tk == 0 and tq == LANES
    nt = seq // tq
    nkt = seq // tk
    kvw = DS_KV * DS_HD
    kvr = DS_KV * (DS_HD + ONES_ROWS)
    cur = lambda b, i: (b * nt + i, 0)
    per_b = lambda b, i: (b, 0)
    fixed = lambda b, i: (0, 0)
    return pl.pallas_call(
        functools.partial(_dsa_frames_kernel, tq=tq, tk=tk, topk=topk),
        grid=(nb, nt),
        in_specs=[pl.BlockSpec((tq, DS_WIDTH), cur),
                  pl.BlockSpec((seq, kvw), per_b, pipeline_mode=pl.Buffered(1)),
                  pl.BlockSpec((nkt, kvr, tk), lambda b, i: (b, 0, 0), pipeline_mode=pl.Buffered(1)),
                  pl.BlockSpec((tq, IDX_HEADS * IDX_HD), cur),
                  pl.BlockSpec((seq, LANES), per_b, pipeline_mode=pl.Buffered(1)),
                  pl.BlockSpec((tq, LANES), cur),
                  pl.BlockSpec((LANES, kvw), fixed), pl.BlockSpec((kvr, LANES), fixed),
                  pl.BlockSpec((LANES, LANES), fixed)],
        out_specs=pl.BlockSpec((tq, DS_WIDTH), cur),
        out_shape=jax.ShapeDtypeStruct((m, DS_WIDTH), _MXU),
        scratch_shapes=[pltpu.VMEM((LANES, tq), jnp.int32), pltpu.VMEM((nkt, tk, tq), jnp.int32),
                        pltpu.VMEM((LANES, tq), jnp.int16), pltpu.VMEM((nkt, tk, tq), jnp.int16),
                        pltpu.VMEM((LANES, tq), jnp.int16), pltpu.VMEM((nkt, tk, tq), jnp.int16),
                        pltpu.VMEM((LANES, tq), _F32), pltpu.VMEM((nkt, tk, tq), _F32),
                        pltpu.VMEM((IDX_HEADS * tq, LANES), _MXU),
                        pltpu.VMEM((LANES, tq), _F32),
                        pltpu.VMEM((DS_KV, DS_REP * tq, DS_HD), _MXU),
                        pltpu.VMEM((DS_KV, 1, DS_REP * tq), _F32),
                        pltpu.VMEM((DS_KV, DS_HD + ONES_ROWS, DS_REP * tq), _F32)],
        compiler_params=_cparams(("parallel", "arbitrary")),
        name="dsa_frames",
    )(q, k, vt, qi, kid, wi, km_blk, vmt_blk, kidm_blk)


def _dsa_small_kernel(*refs, has_cache, nq, n_new, topk):
    if has_cache:
        q_ref, kn_ref, vn_ref, qi_ref, kidn_ref, wi_ref, ck_ref, cv_ref, ckid_ref, o_ref = refs
    else:
        q_ref, kn_ref, vn_ref, qi_ref, kidn_ref, wi_ref, o_ref = refs
    qst = _stack_idx_heads(qi_ref[...])
    wi = wi_ref[...]
    wbs = [jnp.broadcast_to(wi[:, h:h + 1], (nq, LANES)) for h in range(IDX_HEADS)]

    def wb_tile(width):
        return lambda h: jnp.concatenate([wbs[h]] * (width // LANES), axis=1)

    col_n = lax.broadcasted_iota(jnp.int32, (nq, LANES), 1)
    keys = [_order_key(_index_scores(qst, wb_tile(LANES), kidn_ref[...], nq), col_n < n_new)]
    if has_cache:
        past = ckid_ref.shape[1]
        sc = _index_scores(qst, wb_tile(past), ckid_ref[0], nq)
        keys.append(_order_key(sc, jnp.full(sc.shape, True)))

    def count_ge(thr):
        cnt = 0.0
        for kk in keys:
            cnt = cnt + jnp.sum(jnp.where(kk >= thr, 1.0, 0.0), axis=-1, keepdims=True)
        return cnt

    def bit_body(it, u):
        bit = jnp.int32(31) - it
        cand = u | lax.shift_left(jnp.int32(1), bit)
        cnt = count_ge(cand ^ jnp.int32(INT_MIN))
        return jnp.where(cnt >= float(topk), cand, u)

    u = lax.fori_loop(0, 32, bit_body, jnp.zeros((nq, 1), jnp.int32))
    thr = jnp.maximum(u ^ jnp.int32(INT_MIN), jnp.int32(INT_MIN + 1))
    biases = [jnp.where(kk >= thr, 0.0, NEG_INF) for kk in keys]

    for g in range(DS_KV):
        gl = slice(g * DS_HD, (g + 1) * DS_HD)
        qg = jnp.concatenate(
            [q_ref[:, (g * DS_REP + r) * DS_HD:(g * DS_REP + r + 1) * DS_HD] for r in range(DS_REP)], axis=0)
        ks, vs = [kn_ref[:, gl]], [vn_ref[:, gl]]
        if has_cache:
            ks.append(ck_ref[0][:, gl].astype(_MXU))
            vs.append(cv_ref[0][:, gl].astype(_MXU))
        ss = [_dot_nt(qg, kk) + jnp.concatenate([bb] * DS_REP, axis=0) for kk, bb in zip(ks, biases)]
        m = ss[0].max(axis=-1, keepdims=True)
        for s in ss[1:]:
            m = jnp.maximum(m, s.max(axis=-1, keepdims=True))
        l = 0.0
        acc = 0.0
        for s, vv in zip(ss, vs):
            p = jnp.exp2(s - m)
            l = l + jnp.sum(p, axis=-1, keepdims=True)
            acc = acc + _dot(p.astype(_MXU), vv)
        og = acc / l
        for r in range(DS_REP):
            lo = (g * DS_REP + r) * DS_HD
            o_ref[:, lo:lo + DS_HD] = og[r * nq:(r + 1) * nq].astype(o_ref.dtype)


def _dsa_small(q, kn_blk, vn_blk, qi, kidn_blk, wi, cache_k, cache_v, cache_kid, nb, nq, n_new, topk):
    kvw = DS_KV * DS_HD
    has_cache = cache_k is not None
    row = lambda b: (b, 0)
    in_specs = [pl.BlockSpec((nq, DS_WIDTH), row),
                pl.BlockSpec((LANES, kvw), row), pl.BlockSpec((LANES, kvw), row),
                pl.BlockSpec((nq, IDX_HEADS * IDX_HD), row),
                pl.BlockSpec((LANES, LANES), row),
                pl.BlockSpec((nq, LANES), row)]
    args = [q, kn_blk, vn_blk, qi, kidn_blk, wi]
    if has_cache:
        past = cache_k.shape[1]
        in_specs += [pl.BlockSpec((1, past, kvw), lambda b: (b, 0, 0)),
                     pl.BlockSpec((1, past, kvw), lambda b: (b, 0, 0)),
                     pl.BlockSpec((1, past, LANES), lambda b: (b, 0, 0))]
        args += [cache_k, cache_v, cache_kid]
    return pl.pallas_call(
        functools.partial(_dsa_small_kernel, has_cache=has_cache, nq=nq, n_new=n_new, topk=topk),
        grid=(nb,),
        in_specs=in_specs,
        out_specs=pl.BlockSpec((nq, DS_WIDTH), row),
        out_shape=jax.ShapeDtypeStruct((nb * nq, DS_WIDTH), _MXU),
        compiler_params=_cparams(("parallel",)),
        name="dsa_small",
    )(*args)


def _pad_rows_front(a, rows):
    return jnp.pad(a, ((rows - a.shape[0], 0), (0, 0)))


def _pad_rows_back(a, nb, nq, rows):
    w = a.shape[1]
    return jnp.pad(a.reshape(nb, nq, w), ((0, 0), (0, rows - nq), (0, 0))).reshape(nb * rows, w)


def _transpose_with_ones(v, group_cols):
    n = v.shape[0]
    g = v.shape[1] // group_cols
    vt = v.T.reshape(g, group_cols, n)
    pad = jnp.zeros((g, ONES_ROWS, n), v.dtype).at[:, 0, :].set(1)
    return jnp.concatenate([vt, pad], axis=1).reshape(g * (group_cols + ONES_ROWS), n)


def _dup_pairs(a, hd):
    lead = a.shape[:-1]
    g = a.shape[-1] // hd
    a = a.reshape(*lead, g, 1, hd)
    return jnp.broadcast_to(a, (*lead, g, 2, hd)).reshape(*lead, g * 2 * hd)


def kernel(x_prompt, x_sample, cache_l0_k, cache_l0_v, cache_l1_k, cache_l1_v, cache_l2_k, cache_l2_v, cache_l2_kidx, cache_l3_k, cache_l3_v, meta_tokens, l0_norm, l0_w_in, l0_w_out, l0_lam_q1, l0_lam_k1, l0_lam_q2, l0_lam_k2, l0_subln, l1_norm, l1_w_in, l1_w_out, l1_sinks, l2_norm, l2_w_in, l2_w_out, l3_norm, l3_w_in, l3_w_out, l3_lam_q1, l3_lam_k1, l3_lam_q2, l3_lam_k2, l3_subln, final_norm):
    nb, seq, d = x_prompt.shape
    db, ds, _ = x_sample.shape
    past = cache_l0_k.shape[1]
    sw_rows = cache_l1_k.shape[1]
    assert seq % CHUNK == 0 and meta_tokens.shape[0] == N_META and ds <= LANES

    xs = {"f": x_prompt.reshape(nb * seq, d), "m": meta_tokens.astype(_F32), "s": x_sample.reshape(db * ds, d)}
    pos = {"f": N_META + jnp.arange(seq), "m": jnp.arange(N_META), "s": past + jnp.arange(ds)}
    pos["s"] = jnp.tile(pos["s"], db)

    tabs128 = {n: _rope_tables(p, 128, 1.0) for n, p in pos.items()}
    tabs128_da = {n: _rope_tables(p, 128, DA_HD ** -0.5 * LOG2E) for n, p in pos.items()}
    tabs128_ds = {n: _rope_tables(p, 128, DS_HD ** -0.5 * LOG2E) for n, p in pos.items()}
    tabs64 = {n: _rope_tables(p, 64, 1.0) for n, p in pos.items()}
    tabs64_sw = {n: _rope_tables(p, 64, SW_HD ** -0.5 * LOG2E) for n, p in pos.items()}

    layers = [
        dict(norm=l0_norm, w_in=l0_w_in, w_out=l0_w_out, lam=(l0_lam_q1, l0_lam_k1, l0_lam_q2, l0_lam_k2),
             subln=l0_subln, cache=(cache_l0_k, cache_l0_v)),
        dict(norm=l1_norm, w_in=l1_w_in, w_out=l1_w_out, sinks=l1_sinks, cache=(cache_l1_k, cache_l1_v)),
        dict(norm=l2_norm, w_in=l2_w_in, w_out=l2_w_out, cache=(cache_l2_k, cache_l2_v, cache_l2_kidx)),
        dict(norm=l3_norm, w_in=l3_w_in, w_out=l3_w_out, lam=(l3_lam_q1, l3_lam_k1, l3_lam_q2, l3_lam_k2),
             subln=l3_subln, cache=(cache_l3_k, cache_l3_v)),
    ]
    norms = [lp["norm"] for lp in layers[1:]] + [final_norm]
    depth = len(layers)

    hs = {n: _rms_norm_rows(x, layers[0]["norm"]) for n, x in xs.items()}
    p_st, s_st = [], []
    ys = {}
    for li, lp in enumerate(layers):
        kind = li % N_MIXERS
        w_in = lp["w_in"].astype(_MXU)
        w_out = lp["w_out"].astype(_MXU)
        o, z = {}, {}
        if kind == 0:
            lam_init = 0.8 - 0.6 * math.exp(-0.3 * li)
            wq, wk, wv, wz = (w_in[:, i * DA_WIDTH:(i + 1) * DA_WIDTH] for i in range(4))
            lams = jnp.stack([l.astype(_F32) for l in lp["lam"]])
            subln = lp["subln"].reshape(1, 2 * DA_HD).astype(_F32)
            subln_col = jnp.broadcast_to(lp["subln"].astype(_F32)[:, None], (2 * DA_HD, LANES))
            q, k32, k16, v32, v16 = {}, {}, {}, {}, {}
            for n in xs:
                (q[n],) = _project(hs[n], wq, [_MXU], *tabs128_da[n])
                k32[n], k16[n] = _project(hs[n], wk, [_F32, _MXU], *tabs128[n])
                if n == "f":
                    v32[n], vt_f = _project(hs[n], wv, [_F32], group_cols=2 * DA_HD)
                else:
                    v32[n], v16[n] = _project(hs[n], wv, [_F32, _MXU])
                (z[n],) = _project(hs[n], wz, [_MXU])
            o["f"] = _diff_frames(q["f"], k16["f"], vt_f, k16["m"], _transpose_with_ones(v16["m"], 2 * DA_HD),
                                  lams, subln_col, lam_init, nb, seq)
            o["m"] = _diff_small(q["m"], k16["m"], v16["m"], None, None, lams, subln, lam_init, 1, N_META)
            ck, cv = lp["cache"]
            o["s"] = _diff_small(q["s"], k16["s"], v16["s"], ck.reshape(db, past, DA_WIDTH),
                                 cv.reshape(db, past, DA_WIDTH), lams, subln, lam_init, db, ds)
            hshape = (DA_HEADS, 2 * DA_HD)

            def with_meta(f, mrows):
                mm = jnp.broadcast_to(mrows.reshape(1, N_META, *hshape), (nb, N_META, *hshape))
                return jnp.concatenate([mm, f.reshape(nb, seq, *hshape)], axis=1)

            p_st.append([with_meta(k32["f"], k32["m"]), with_meta(v32["f"], v32["m"])])
            s_st.append([k32["s"].reshape(db, ds, *hshape), v32["s"].reshape(db, ds, *hshape)])
        elif kind == 1:
            kvd = SW_KV * SW_HD
            wq = w_in[:, :SW_WIDTH]
            wk = w_in[:, SW_WIDTH:SW_WIDTH + kvd]
            wv = w_in[:, SW_WIDTH + kvd:SW_WIDTH + 2 * kvd]
            wz = w_in[:, SW_WIDTH + 2 * kvd:]
            wkd, wvd = _dup_pairs(wk, SW_HD), _dup_pairs(wv, SW_HD)
            sink_rows = {}
            q, kv32, kd, vd = {}, {}, {}, {}
            for n, rows in (("f", CHUNK), ("m", N_META), ("s", ds)):
                sk = (lp["sinks"].astype(_F32) * LOG2E).reshape(SW_KV, SW_REP, 1, 1)
                sink_rows[n] = jnp.broadcast_to(sk, (SW_KV, SW_REP, rows, LANES)).reshape(SW_KV, SW_REP * rows, LANES)
                (q[n],) = _project(hs[n], wq, [_MXU], *tabs64_sw[n])
                (kd[n],) = _project(hs[n], wkd, [_MXU], *tabs64[n])
                (vd[n],) = _project(hs[n], wvd, [_MXU])
                (z[n],) = _project(hs[n], wz, [_MXU])
            kmeta_blk = _pad_rows_front(kd["m"], WINDOW)
            vmeta_blk = _pad_rows_front(vd["m"], WINDOW)
            o["f"] = _swa_frames(q["f"], kd["f"], vd["f"], kmeta_blk, vmeta_blk, sink_rows["f"], nb, seq)
            o["m"] = _swa_small(q["m"], kd["m"], vd["m"], None, None, sink_rows["m"], 1, N_META)
            ck, cv = lp["cache"]
            ckd = _dup_pairs(ck.reshape(db, sw_rows, kvd), SW_HD).astype(_MXU)
            cvd = _dup_pairs(cv.reshape(db, sw_rows, kvd), SW_HD).astype(_MXU)
            o["s"] = _swa_small(q["s"], kd["s"], vd["s"], ckd, cvd, sink_rows["s"], db, ds)
            assert seq >= sw_rows and sw_rows % 8 == 0
            h_tail = hs["f"].reshape(nb, seq, d)[:, seq - sw_rows:].reshape(nb * sw_rows, d)
            tail_pos = N_META + seq - sw_rows + jnp.arange(sw_rows)
            tail_tabs, tail_shift = _rope_tables(tail_pos, 64, 1.0)
            (kt,) = _project(h_tail, _pad_cols(wk), [_F32], tail_tabs, tail_shift)
            (vt,) = _project(h_tail, _pad_cols(wv), [_F32])
            p_st.append([kt[:, :kvd].reshape(nb, sw_rows, SW_KV, SW_HD), vt[:, :kvd].reshape(nb, sw_rows, SW_KV, SW_HD)])
            (ks,) = _project(hs["s"], _pad_cols(wk), [_F32], *tabs64["s"])
            (vs_,) = _project(hs["s"], _pad_cols(wv), [_F32])
            ks = ks[:, :kvd].reshape(db, ds, SW_KV, SW_HD)
            vs_ = vs_[:, :kvd].reshape(db, ds, SW_KV, SW_HD)
            k_all = jnp.concatenate([ck, ks], axis=1)
            v_all = jnp.concatenate([cv, vs_], axis=1)
            s_st.append([k_all[:, k_all.shape[1] - sw_rows:], v_all[:, v_all.shape[1] - sw_rows:]])
        else:
            kvd = DS_KV * DS_HD
            c0 = 0
            wq = w_in[:, c0:c0 + DS_WIDTH]; c0 += DS_WIDTH
            wk = w_in[:, c0:c0 + kvd]; c0 += kvd
            wv = w_in[:, c0:c0 + kvd]; c0 += kvd
            wz = w_in[:, c0:c0 + DS_WIDTH]; c0 += DS_WIDTH
            wqi = w_in[:, c0:c0 + IDX_HEADS * IDX_HD]; c0 += IDX_HEADS * IDX_HD
            wki = w_in[:, c0:c0 + IDX_HD]; c0 += IDX_HD
            wwi = w_in[:, c0:c0 + IDX_HEADS]
            wkid = jnp.concatenate([wki, wki], axis=1)
            wwi_p = _pad_cols(wwi)
            wi_scale = IDX_HEADS ** -0.5 * IDX_HD ** -0.5
            q, k32, k16, v32, v16, qi, kid32, kid16, wi = {}, {}, {}, {}, {}, {}, {}, {}, {}
            for n in xs:
                (q[n],) = _project(hs[n], wq, [_MXU], *tabs128_ds[n])
                k32[n], k16[n] = _project(hs[n], wk, [_F32, _MXU], *tabs128[n])
                if n == "f":
                    v32[n], vt_f = _project(hs[n], wv, [_F32], group_cols=DS_HD)
                else:
                    v32[n], v16[n] = _project(hs[n], wv, [_F32, _MXU])
                (z[n],) = _project(hs[n], wz, [_MXU])
                (qi[n],) = _project(hs[n], wqi, [_MXU], *tabs64[n])
                kid32[n], kid16[n] = _project(hs[n], wkid, [_F32, _MXU], *tabs64[n])
                (wi[n],) = _project(hs[n], wwi_p, [_F32], scale=wi_scale)
            topk_p = min(TOPK_MAX, seq // 4)
            topk_s = min(TOPK_MAX, (past + ds) // 4)
            km_blk = _pad_rows_back(k16["m"], 1, N_META, LANES)
            vm_blk = _pad_rows_back(v16["m"], 1, N_META, LANES)
            kidm_blk = _pad_rows_back(kid16["m"], 1, N_META, LANES)
            o["f"] = _dsa_frames(q["f"], k16["f"], vt_f, qi["f"], kid16["f"], wi["f"],
                                 km_blk, _transpose_with_ones(vm_blk, DS_HD), kidm_blk, nb, seq, topk_p)
            o["m"] = _dsa_small(q["m"], km_blk, vm_blk, qi["m"], kidm_blk, wi["m"], None, None, None,
                                1, N_META, N_META, topk_p)
            ck, cv, cki = lp["cache"]
            ckid = jnp.concatenate([cki, cki], axis=-1).astype(_MXU)
            o["s"] = _dsa_small(q["s"], _pad_rows_back(k16["s"], db, ds, LANES), _pad_rows_back(v16["s"], db, ds, LANES),
                                qi["s"], _pad_rows_back(kid16["s"], db, ds, LANES), wi["s"],
                                ck.reshape(db, past, kvd), cv.reshape(db, past, kvd), ckid,
                                db, ds, ds, topk_s)

            def with_meta(f, mrows, *tail):
                mm = jnp.broadcast_to(mrows.reshape(1, N_META, *tail), (nb, N_META, *tail))
                return jnp.concatenate([mm, f.reshape(nb, seq, *tail)], axis=1)

            p_st.append([with_meta(k32["f"], k32["m"], DS_KV, DS_HD), with_meta(v32["f"], v32["m"], DS_KV, DS_HD),
                         with_meta(kid32["f"][:, :IDX_HD], kid32["m"][:, :IDX_HD], IDX_HD)])
            s_st.append([k32["s"].reshape(db, ds, DS_KV, DS_HD), v32["s"].reshape(db, ds, DS_KV, DS_HD),
                         kid32["s"][:, :IDX_HD].reshape(db, ds, IDX_HD)])
        last = li == depth - 1
        for n in xs:
            res = _out_project(o[n], z[n], w_out, xs[n], norms[li], last)
            if last:
                (ys[n],) = res
            else:
                xs[n], hs[n] = res

    y_prompt = ys["f"].reshape(nb, seq, d)
    y_sample = ys["s"].reshape(db, ds, d)
    return (y_prompt, y_sample,
            p_st[0][0], p_st[0][1], s_st[0][0], s_st[0][1],
            p_st[1][0], p_st[1][1], s_st[1][0], s_st[1][1],
            p_st[2][0], p_st[2][1], p_st[2][2], s_st[2][0], s_st[2][1], s_st[2][2],
            p_st[3][0], p_st[3][1], s_st[3][0], s_st[3][1])


def _pad_cols(w):
    n = w.shape[1]
    return jnp.pad(w, ((0, 0), (0, (-n) % LANES)))
```

```python
import functools
import math

import jax
import jax.numpy as jnp
from jax import lax
from jax.experimental import pallas as pl
from jax.experimental.pallas import tpu as pltpu

CHUNK = 64
N_META = 16
ROPE_THETA = 500000.0
NORM_EPS = 1e-6
NEG_INF = -1e30
DA_HEADS = 8
DA_HD = 128
DA_WIDTH = DA_HEADS * 2 * DA_HD
SW_HEADS = 32
SW_KV = 4
SW_HD = 64
SW_WIDTH = SW_HEADS * SW_HD
SW_REP = SW_HEADS // SW_KV
WINDOW = 128
DS_HEADS = 16
DS_KV = 4
DS_HD = 128
DS_WIDTH = DS_HEADS * DS_HD
DS_REP = DS_HEADS // DS_KV
IDX_HEADS = 16
IDX_HD = 64
TOPK_MAX = 256
N_MIXERS = 3

LANES = 128
INT_MIN = -(2 ** 31)
LOG2E = math.log2(math.e)
ONES_ROWS = 16
VMEM_LIMIT = 56 * 1024 * 1024

_MXU = jnp.bfloat16
_F32 = jnp.float32

_NT = (((1,), (1,)), ((), ()))


def _cparams(sem):
    return pltpu.CompilerParams(dimension_semantics=sem, vmem_limit_bytes=VMEM_LIMIT)


def _row_tile(m, pref):
    t = pref
    while t > 8 and m % t:
        t //= 2
    assert m % t == 0, (m, pref)
    return t


def _dot(a, b):
    return jnp.dot(a, b, preferred_element_type=_F32)


def _dot_nt(a, b):
    return lax.dot_general(a, b, _NT, preferred_element_type=_F32)


def _norm_kernel(x_ref, g_ref, h_ref):
    x = x_ref[...]
    y = x * lax.rsqrt(jnp.mean(x * x, axis=-1, keepdims=True) + NORM_EPS)
    h_ref[...] = (y * g_ref[...]).astype(h_ref.dtype)


def _rms_norm_rows(x, g):
    m, d = x.shape
    out_dtype = _MXU
    tm = _row_tile(m, 512)
    return pl.pallas_call(
        _norm_kernel,
        grid=(m // tm,),
        in_specs=[pl.BlockSpec((tm, d), lambda i: (i, 0)),
                  pl.BlockSpec((1, d), lambda i: (0, 0))],
        out_specs=pl.BlockSpec((tm, d), lambda i: (i, 0)),
        out_shape=jax.ShapeDtypeStruct((m, d), out_dtype),
        compiler_params=_cparams(("parallel",)),
        name="rms_norm",
    )(x, g.reshape(1, d).astype(_F32))


def _proj_kernel(*refs, rope_shift, scale, n_out, tn, group_cols):
    h_ref, w_ref = refs[0], refs[1]
    pos = 2
    if rope_shift is not None:
        c_ref, a_ref, b_ref = refs[2:5]
        pos = 5
    outs = refs[pos:pos + n_out]
    ot_ref = refs[pos + n_out] if group_cols else None
    if ot_ref is not None:
        tm = ot_ref.shape[2]
        row = lax.broadcasted_iota(jnp.int32, (ONES_ROWS, tm), 0)
        ones_blk = jnp.where(row == 0, 1.0, 0.0).astype(ot_ref.dtype)
        for g in range(tn // group_cols):
            base = g * (group_cols + ONES_ROWS) + group_cols
            ot_ref[0, base:base + ONES_ROWS, :] = ones_blk
    h = h_ref[...]
    cw = 2 * LANES if tn % (2 * LANES) == 0 else LANES
    for c in range(tn // cw):
        wide = _dot(h, w_ref[:, c * cw:(c + 1) * cw])
        for sub in range(cw // LANES):
            acc = wide[:, sub * LANES:(sub + 1) * LANES]
            if rope_shift is not None:
                acc = (acc * c_ref[...]
                       + pltpu.roll(acc, rope_shift, 1) * a_ref[...]
                       + pltpu.roll(acc, LANES - rope_shift, 1) * b_ref[...])
            elif scale is not None:
                acc = acc * scale
            cols = slice(c * cw + sub * LANES, c * cw + (sub + 1) * LANES)
            for o in outs:
                if len(o.shape) == 3:
                    hw = o.shape[2]
                    o[:, cols.start // hw, cols.start % hw:cols.start % hw + LANES] = acc.astype(o.dtype)
                else:
                    o[:, cols] = acc.astype(o.dtype)
            if ot_ref is not None:
                lo = cols.start
                r0 = (lo // group_cols) * (group_cols + ONES_ROWS) + lo % group_cols
                ot_ref[0, r0:r0 + LANES, :] = acc.T.astype(ot_ref.dtype)


def _project(h, w, out_dtypes, tabs=None, rope_shift=None, scale=None, group_cols=0, head_shape=None):
    m, d = h.shape
    n = w.shape[1]
    tm = _row_tile(m if tabs is None else math.gcd(m, tabs[0].shape[0]), 512)
    tn = n if n <= 2048 else 2048
    assert n % tn == 0 and tn % LANES == 0
    in_specs = [pl.BlockSpec((tm, d), lambda j, i: (i, 0)),
                pl.BlockSpec((d, tn), lambda j, i: (0, j))]
    args = [h, w]
    if tabs is not None:
        trows = tabs[0].shape[0]
        assert trows % tm == 0
        nrep = trows // tm
        for t in tabs:
            in_specs.append(pl.BlockSpec((tm, LANES), lambda j, i: (i % nrep, 0)))
            args.append(t)
    else:
        rope_shift = None
    out_specs = [pl.BlockSpec((tm, tn), lambda j, i: (i, j)) for _ in out_dtypes]
    out_shape = [jax.ShapeDtypeStruct((m, n), dt) for dt in out_dtypes]
    if head_shape is not None:
        nh, hw = head_shape
        assert nh * hw == n == tn and hw % LANES == 0
        out_specs[0] = pl.BlockSpec((tm, nh, hw), lambda j, i: (i, 0, 0))
        out_shape[0] = jax.ShapeDtypeStruct((m, nh, hw), out_dtypes[0])
    if group_cols:
        assert tn % group_cols == 0 and group_cols % LANES == 0
        grow = group_cols + ONES_ROWS
        out_specs.append(pl.BlockSpec((1, tn // group_cols * grow, tm), lambda j, i: (i, j, 0)))
        out_shape.append(jax.ShapeDtypeStruct((m // tm, n // group_cols * grow, tm), _MXU))
    outs = pl.pallas_call(
        functools.partial(_proj_kernel, rope_shift=rope_shift, scale=scale,
                          n_out=len(out_dtypes), tn=tn, group_cols=group_cols),
        grid=(n // tn, m // tm),
        in_specs=in_specs,
        out_specs=out_specs,
        out_shape=out_shape,
        compiler_params=_cparams(("parallel", "parallel")),
        name="in_proj",
    )(*args)
    return outs


def _outproj_kernel(o_ref, z_ref, w_ref, x_ref, g_ref, *outs, emit_x):
    z = z_ref[...].astype(_F32)
    a = o_ref[...].astype(_F32) * (z / (1.0 + jnp.exp(-z)))
    y = x_ref[...] + _dot(a.astype(w_ref.dtype), w_ref[...])
    if emit_x:
        outs[0][...] = y
    n = y * lax.rsqrt(jnp.mean(y * y, axis=-1, keepdims=True) + NORM_EPS) * g_ref[...]
    outs[-1][...] = n.astype(outs[-1].dtype)


def _out_project(o, z, w, x, g_next, last):
    m, d = x.shape
    wd = o.shape[1]
    tm = _row_tile(m, 512)
    row = lambda i: (i, 0)
    fixed = lambda i: (0, 0)
    if last:
        out_shape = [jax.ShapeDtypeStruct((m, d), _F32)]
    else:
        out_shape = [jax.ShapeDtypeStruct((m, d), _F32), jax.ShapeDtypeStruct((m, d), _MXU)]
    return pl.pallas_call(
        functools.partial(_outproj_kernel, emit_x=not last),
        grid=(m // tm,),
        in_specs=[pl.BlockSpec((tm, wd), row), pl.BlockSpec((tm, wd), row),
                  pl.BlockSpec((wd, d), fixed), pl.BlockSpec((tm, d), row),
                  pl.BlockSpec((1, d), fixed)],
        out_specs=[pl.BlockSpec((tm, d), row) for _ in out_shape],
        out_shape=out_shape,
        compiler_params=_cparams(("parallel",)),
        name="out_proj",
    )(o, z, w, x, g_next.reshape(1, d).astype(_F32))


def _rope_tables(pos, hd, scale=1.0):
    rd = hd // 4
    half = rd // 2
    lane = jnp.arange(LANES)
    d = lane % hd
    f = d % half
    inv_freq = ROPE_THETA ** (-(jnp.arange(half, dtype=_F32) * 2.0 / rd))
    ang = pos.astype(_F32)[:, None] * inv_freq[f][None, :]
    cos, sin = jnp.cos(ang), jnp.sin(ang)
    in_rot = (d < rd)[None, :]
    c = jnp.where(in_rot, cos, 1.0)
    a = jnp.where(((d >= half) & (d < rd))[None, :], sin, 0.0)
    b = jnp.where((d < half)[None, :], -sin, 0.0)
    s = jnp.float32(scale)
    return (c * s, a * s, b * s), half


def _lam_value(lam_ref, lam_init):
    l = lam_ref[...]
    s1 = jnp.sum(l[0:1] * l[1:2], axis=-1, keepdims=True)
    s2 = jnp.sum(l[2:3] * l[3:4], axis=-1, keepdims=True)
    return jnp.exp(s1) - jnp.exp(s2) + lam_init


def _diff_finish(acc1, l1, acc2, l2, lam, sub, lam_init):
    o = acc1 / l1 - lam * (acc2 / l2)
    o = o * lax.rsqrt(jnp.mean(o * o, axis=-1, keepdims=True) + NORM_EPS) * sub
    return o * (1.0 - lam_init)


def _diff_frames_kernel(q_ref, k_ref, vt_ref, km_ref, vmt_ref, lam_ref, subc_ref, o_ref,
                        acc_s, sa_s, sb_s, pa_s, pb_s, *, tq, lam_init):
    i = pl.program_id(2)
    hw = 2 * DA_HD

    def q_half(j):
        return q_ref[:, j * DA_HD:(j + 1) * DA_HD]

    km = km_ref[...]
    vmt = vmt_ref[...]
    ms = []
    for j in range(2):
        s = _dot_nt(km[:, j * DA_HD:(j + 1) * DA_HD], q_half(j))
        m = jnp.max(s, axis=0, keepdims=True)
        ms.append(m)
        acc_s[j] = _dot(vmt, jnp.exp2(s - m).astype(vmt.dtype))

    def scores_into(dst, t):
        start = pl.multiple_of(t * tq, tq)
        for j in range(2):
            dst[j] = _dot_nt(k_ref[pl.ds(start, tq), j * DA_HD:(j + 1) * DA_HD], q_half(j))

    def value_update(t, a, p_rd):
        vt = vt_ref[t]
        for j in range(2):
            acc_s[j] = a[j] * acc_s[j] + _dot(vt, p_rd[j])

    def tile_step(t, carry, src, dst, p_rd, p_wr, masked, has_next):
        m, a = carry
        if masked:
            kc = lax.broadcasted_iota(jnp.int32, (tq, tq), 0) // CHUNK
            qc = lax.broadcasted_iota(jnp.int32, (tq, tq), 1) // CHUNK
            ok = kc <= qc
        m_out, a_out = [], []
        for j in range(2):
            s = src[j]
            if masked:
                s = jnp.where(ok, s, NEG_INF)
            m_new = jnp.maximum(m[j], jnp.max(s, axis=0, keepdims=True))
            a_out.append(jnp.exp2(m[j] - m_new))
            m_out.append(m_new)
            p_wr[j] = jnp.exp2(s - m_new).astype(p_wr.dtype)
        value_update(jnp.maximum(t - 1, 0), a, p_rd)
        if has_next:
            scores_into(dst, t + 1)
        return tuple(m_out), tuple(a_out)

    scores_into(sa_s, 0)
    pb_s[...] = jnp.zeros_like(pb_s)
    one = jnp.ones((1, tq), _F32)
    carry = ((ms[0], ms[1]), (one, one))

    def pair_body(u, carry):
        carry = tile_step(2 * u, carry, sa_s, sb_s, pb_s, pa_s, False, True)
        return tile_step(2 * u + 1, carry, sb_s, sa_s, pa_s, pb_s, False, True)

    carry = lax.fori_loop(0, i // 2, pair_body, carry)

    @pl.when(i % 2 == 0)
    def _():
        _, a = tile_step(i, carry, sa_s, sb_s, pb_s, pa_s, True, False)
        value_update(i, a, pa_s)

    @pl.when(i % 2 == 1)
    def _():
        c = tile_step(i - 1, carry, sa_s, sb_s, pb_s, pa_s, False, True)
        _, a = tile_step(i, c, sb_s, sa_s, pa_s, pb_s, True, False)
        value_update(i, a, pb_s)

    acc0 = acc_s[0]
    acc1 = acc_s[1]
    lam = _lam_value(lam_ref, lam_init)
    inv0 = 1.0 / acc0[hw:hw + 1]
    inv1 = lam / acc1[hw:hw + 1]
    ot = acc0[:hw] * inv0 - acc1[:hw] * inv1
    sub = jnp.concatenate([subc_ref[...]] * (tq // LANES), axis=1)
    ot = ot * lax.rsqrt(jnp.mean(ot * ot, axis=0, keepdims=True) + NORM_EPS) * sub * (1.0 - lam_init)
    o_ref[...] = ot.T.astype(o_ref.dtype)


def _diff_frames(q, k, vt, km, vmt, lams, subln_col, lam_init, nb, seq):
    m = q.shape[0]
    hw = 2 * DA_HD
    hr = hw + ONES_ROWS
    tq = vt.shape[2]
    assert seq % tq == 0 and tq % LANES == 0
    nt = seq // tq
    return pl.pallas_call(
        functools.partial(_diff_frames_kernel, tq=tq, lam_init=lam_init),
        grid=(nb, DA_HEADS, nt),
        in_specs=[pl.BlockSpec((tq, hw), lambda b, h, i: (b * nt + i, h)),
                  pl.BlockSpec((seq, hw), lambda b, h, i: (b, h)),
                  pl.BlockSpec((nt, hr, tq), lambda b, h, i: (b, h, 0)),
                  pl.BlockSpec((N_META, hw), lambda b, h, i: (0, h)),
                  pl.BlockSpec((hr, N_META), lambda b, h, i: (h, 0)),
                  pl.BlockSpec((4, DA_HD), lambda b, h, i: (0, 0)),
                  pl.BlockSpec((hw, LANES), lambda b, h, i: (0, 0))],
        out_specs=pl.BlockSpec((tq, hw), lambda b, h, i: (b * nt + i, h)),
        out_shape=jax.ShapeDtypeStruct((m, DA_WIDTH), _MXU),
        scratch_shapes=[pltpu.VMEM((2, hr, tq), _F32),
                        pltpu.VMEM((2, tq, tq), _F32), pltpu.VMEM((2, tq, tq), _F32),
                        pltpu.VMEM((2, tq, tq), _MXU), pltpu.VMEM((2, tq, tq), _MXU)],
        compiler_params=_cparams(("parallel", "parallel", "arbitrary")),
        name="diff_frames",
    )(q, k, vt, km, vmt, lams, subln_col)


def _diff_small_kernel(*refs, has_cache, lam_init):
    if has_cache:
        q_ref, k_ref, v_ref, ck_ref, cv_ref, lam_ref, sub_ref, o_ref = refs
    else:
        q_ref, k_ref, v_ref, lam_ref, sub_ref, o_ref = refs
    q = q_ref[...]
    kn = k_ref[...]
    vn = v_ref[...]
    parts = [(kn, vn)]
    if has_cache:
        parts.append((ck_ref[0].astype(kn.dtype), cv_ref[0].astype(vn.dtype)))
    accs, ls = [], []
    for j in range(2):
        qj = q[:, j * DA_HD:(j + 1) * DA_HD]
        ss = [_dot_nt(qj, kp[:, j * DA_HD:(j + 1) * DA_HD]) for kp, _ in parts]
        m = ss[0].max(axis=-1, keepdims=True)
        for s in ss[1:]:
            m = jnp.maximum(m, s.max(axis=-1, keepdims=True))
        l = 0.0
        acc = 0.0
        for s, (_, vp) in zip(ss, parts):
            p = jnp.exp2(s - m)
            l = l + jnp.sum(p, axis=-1, keepdims=True)
            acc = acc + _dot(p.astype(vp.dtype), vp)
        accs.append(acc)
        ls.append(l)
    lam = _lam_value(lam_ref, lam_init)
    o = _diff_finish(accs[0], ls[0], accs[1], ls[1], lam, sub_ref[...], lam_init)
    o_ref[...] = o.astype(o_ref.dtype)


def _diff_small(q, k, v, cache_k, cache_v, lams, subln, lam_init, nb, nq):
    hw = 2 * DA_HD
    has_cache = cache_k is not None
    qspec = pl.BlockSpec((nq, hw), lambda b, h: (b, h))
    in_specs = [qspec, qspec, qspec]
    args = [q, k, v]
    if has_cache:
        past = cache_k.shape[1]
        cspec = pl.BlockSpec((1, past, hw), lambda b, h: (b, 0, h))
        in_specs += [cspec, cspec]
        args += [cache_k, cache_v]
    in_specs += [pl.BlockSpec((4, DA_HD), lambda b, h: (0, 0)),
                 pl.BlockSpec((1, hw), lambda b, h: (0, 0))]
    args += [lams, subln]
    return pl.pallas_call(
        functools.partial(_diff_small_kernel, has_cache=has_cache, lam_init=lam_init),
        grid=(nb, DA_HEADS),
        in_specs=in_specs,
        out_specs=qspec,
        out_shape=jax.ShapeDtypeStruct((nb * nq, DA_WIDTH), _MXU),
        compiler_params=_cparams(("parallel", "parallel")),
        name="diff_small",
    )(*args)


def _stack_heads(qg, nheads):
    n = qg.shape[0]
    lane = lax.broadcasted_iota(jnp.int32, (n, LANES), 1)
    lo = lane < SW_HD
    pieces = []
    for r in range(nheads):
        blk = qg[:, (r // 2) * LANES:(r // 2 + 1) * LANES]
        keep = lo if r % 2 == 0 else jnp.logical_not(lo)
        pieces.append(jnp.where(keep, blk, jnp.zeros_like(blk)))
    return jnp.concatenate(pieces, axis=0)


def _unstack_heads(o, n, nheads):
    lane = lax.broadcasted_iota(jnp.int32, (n, LANES), 1)
    lo = lane < SW_HD
    return [jnp.where(lo, o[(2 * p) * n:(2 * p + 1) * n], o[(2 * p + 1) * n:(2 * p + 2) * n])
            for p in range(nheads // 2)]


def _sink_softmax_pv(ss, vs, sk):
    m = sk
    for s in ss:
        m = jnp.maximum(m, jnp.max(s, axis=-1, keepdims=True))
    den = jnp.exp2(sk - m)
    acc = 0.0
    for s, v in zip(ss, vs):
        e = jnp.exp2(s - m)
        den = den + jnp.sum(e, axis=-1, keepdims=True)
        acc = acc + _dot(e.astype(v.dtype), v)
    return acc / den


def _swa_frames_kernel(q_ref, kc_ref, vc_ref, kp_ref, vp_ref, km_ref, vm_ref, sk_ref, o_ref,
                       kcat, vcat, *, tq):
    i = pl.program_id(1)
    first = i == 0
    prev_rows = WINDOW
    kcat[0:prev_rows] = jnp.where(first, km_ref[...], kp_ref[...])
    vcat[0:prev_rows] = jnp.where(first, vm_ref[...], vp_ref[...])
    kcat[prev_rows:prev_rows + tq] = kc_ref[...]
    vcat[prev_rows:prev_rows + tq] = vc_ref[...]
    nwin = WINDOW + CHUNK
    for c in range(tq // CHUNK):
        rows = slice(c * CHUNK, (c + 1) * CHUNK)
        win = slice(c * CHUNK, c * CHUNK + nwin)
        for g in range(SW_KV):
            gl = slice(g * LANES, (g + 1) * LANES)
            qst = _stack_heads(q_ref[rows, g * SW_REP * SW_HD:(g + 1) * SW_REP * SW_HD], SW_REP)
            s = _dot_nt(qst, kcat[win, gl])
            if c * CHUNK < prev_rows - N_META:
                col = lax.broadcasted_iota(jnp.int32, s.shape, 1) + c * CHUNK
                ok = jnp.logical_or(col >= prev_rows - N_META, jnp.logical_not(first))
                s = jnp.where(ok, s, NEG_INF)
            o = _sink_softmax_pv([s], [vcat[win, gl]], sk_ref[g][:, 0:1])
            for p, blk in enumerate(_unstack_heads(o, CHUNK, SW_REP)):
                lo = g * SW_REP * SW_HD + p * LANES
                o_ref[rows, lo:lo + LANES] = blk.astype(o_ref.dtype)


def _swa_frames(q, kd, vd, kmeta_blk, vmeta_blk, sink_rows, nb, seq):
    m = q.shape[0]
    tq = _row_tile(seq, 512)
    nt = seq // tq
    pw = SW_KV * LANES
    per = tq // WINDOW
    cur = lambda b, i: (b * nt + i, 0)
    prev = lambda b, i: (jnp.maximum((b * nt + i) * per - 1, 0), 0)
    fixed = lambda b, i: (0, 0)
    return pl.pallas_call(
        functools.partial(_swa_frames_kernel, tq=tq),
        grid=(nb, nt),
        in_specs=[pl.BlockSpec((tq, SW_WIDTH), cur),
                  pl.BlockSpec((tq, pw), cur), pl.BlockSpec((tq, pw), cur),
                  pl.BlockSpec((WINDOW, pw), prev), pl.BlockSpec((WINDOW, pw), prev),
                  pl.BlockSpec((WINDOW, pw), fixed), pl.BlockSpec((WINDOW, pw), fixed),
                  pl.BlockSpec((SW_KV, SW_REP * CHUNK, LANES), lambda b, i: (0, 0, 0))],
        out_specs=pl.BlockSpec((tq, SW_WIDTH), cur),
        out_shape=jax.ShapeDtypeStruct((m, SW_WIDTH), _MXU),
        scratch_shapes=[pltpu.VMEM((WINDOW + tq, pw), _MXU), pltpu.VMEM((WINDOW + tq, pw), _MXU)],
        compiler_params=_cparams(("parallel", "arbitrary")),
        name="swa_frames",
    )(q, kd, vd, kd, vd, kmeta_blk, vmeta_blk, sink_rows)


def _swa_small_kernel(*refs, has_cache, nq):
    if has_cache:
        q_ref, k_ref, v_ref, ck_ref, cv_ref, sk_ref, o_ref = refs
    else:
        q_ref, k_ref, v_ref, sk_ref, o_ref = refs
    for g in range(SW_KV):
        gl = slice(g * LANES, (g + 1) * LANES)
        qst = _stack_heads(q_ref[:, g * SW_REP * SW_HD:(g + 1) * SW_REP * SW_HD], SW_REP)
        ks, vs = [k_ref[:, gl]], [v_ref[:, gl]]
        if has_cache:
            ks.append(ck_ref[0][:, gl])
            vs.append(cv_ref[0][:, gl])
        ss = [_dot_nt(qst, kk) for kk in ks]
        o = _sink_softmax_pv(ss, vs, sk_ref[g][:, 0:1])
        for p, blk in enumerate(_unstack_heads(o, nq, SW_REP)):
            lo = g * SW_REP * SW_HD + p * LANES
            o_ref[:, lo:lo + LANES] = blk.astype(o_ref.dtype)


def _swa_small(q, kd, vd, cache_kd, cache_vd, sink_rows, nb, nq):
    pw = SW_KV * LANES
    has_cache = cache_kd is not None
    row = lambda b: (b, 0)
    in_specs = [pl.BlockSpec((nq, SW_WIDTH), row), pl.BlockSpec((nq, pw), row), pl.BlockSpec((nq, pw), row)]
    args = [q, kd, vd]
    if has_cache:
        past = cache_kd.shape[1]
        cspec = pl.BlockSpec((1, past, pw), lambda b: (b, 0, 0))
        in_specs += [cspec, cspec]
        args += [cache_kd, cache_vd]
    in_specs.append(pl.BlockSpec((SW_KV, SW_REP * nq, LANES), lambda b: (0, 0, 0)))
    args.append(sink_rows)
    return pl.pallas_call(
        functools.partial(_swa_small_kernel, has_cache=has_cache, nq=nq),
        grid=(nb,),
        in_specs=in_specs,
        out_specs=pl.BlockSpec((nq, SW_WIDTH), row),
        out_shape=jax.ShapeDtypeStruct((nb * nq, SW_WIDTH), _MXU),
        compiler_params=_cparams(("parallel",)),
        name="swa_small",
    )(*args)


def _order_key(score, valid):
    bits = lax.bitcast_convert_type(score + 0.0, jnp.int32)
    key = jnp.where(bits < 0, bits ^ jnp.int32(0x7FFFFFFF), bits)
    return jnp.where(valid, key, jnp.int32(INT_MIN))


def _stack_idx_heads(qi):
    return _stack_heads(qi, IDX_HEADS)


def _index_scores(qst, wb_of, kid, n):
    score = None
    hb = 4
    for h0 in range(0, IDX_HEADS, hb):
        sc = _dot_nt(qst[h0 * n:(h0 + hb) * n], kid)
        for h in range(h0, h0 + hb):
            part = jnp.maximum(sc[(h - h0) * n:(h - h0 + 1) * n], 0.0) * wb_of(h)
            score = part if score is None else score + part
    return score


def _index_scores_t(kid, qst, wit, n):
    score = None
    hb = 4
    for h0 in range(0, IDX_HEADS, hb):
        sc = _dot_nt(kid, qst[h0 * n:(h0 + hb) * n])
        for h in range(h0, h0 + hb):
            part = jnp.maximum(sc[:, (h - h0) * n:(h - h0 + 1) * n], 0.0) * wit[h:h + 1, :]
            score = part if score is None else score + part
    return score


def _dsa_frames_kernel(q_ref, k_ref, vt_ref, qi_ref, kid_ref, wi_ref, km_ref, vmt_ref, kidm_ref, o_ref,
                       keym_s, key_s, biasm_s, bias_s, qst_s, wit_s, qg_s, m_s, acc_s, *, tq, tk, topk):
    i = pl.program_id(1)
    n_t = (i * tq + tq + tk - 1) // tk
    qst_s[...] = _stack_idx_heads(qi_ref[...])
    wit_s[...] = wi_ref[...].T
    sc_m = _index_scores_t(kidm_ref[...], qst_s[...], wit_s[...], tq)
    row_m = lax.broadcasted_iota(jnp.int32, (LANES, tq), 0)
    keym_s[...] = _order_key(sc_m, row_m < N_META)

    q_chunk = (lax.broadcasted_iota(jnp.int32, (tk, tq), 1) + i * tq) // CHUNK

    def score_body(t, carry):
        start = pl.multiple_of(t * tk, tk)
        sc = _index_scores_t(kid_ref[pl.ds(start, tk), :], qst_s[...], wit_s[...], tq)
        k_chunk = (lax.broadcasted_iota(jnp.int32, (tk, tq), 0) + t * tk) // CHUNK
        key_s[t] = _order_key(sc, k_chunk <= q_chunk)
        return carry

    lax.fori_loop(0, n_t, score_body, 0)

    def count_ge(thr):
        def fold(x):
            return jnp.sum(x.reshape(x.shape[0] // 64, 64, tq), axis=0)

        part = fold(jnp.where(keym_s[...] >= thr, 1.0, 0.0))

        def cbody(t, acc):
            return acc + fold(jnp.where(key_s[t] >= thr, 1.0, 0.0))

        part = lax.fori_loop(0, n_t, cbody, part)
        return jnp.sum(part, axis=0, keepdims=True)

    def bit_body(it, u):
        bit = jnp.int32(31) - it
        cand = u | lax.shift_left(jnp.int32(1), bit)
        cnt = count_ge(cand ^ jnp.int32(INT_MIN))
        return jnp.where(cnt >= float(topk), cand, u)

    u = lax.fori_loop(0, 32, bit_body, jnp.zeros((1, tq), jnp.int32))
    thr = jnp.maximum(u ^ jnp.int32(INT_MIN), jnp.int32(INT_MIN + 1))

    biasm_s[...] = jnp.where(keym_s[...] >= thr, 0.0, NEG_INF)

    def bias_body(t, carry):
        bias_s[t] = jnp.where(key_s[t] >= thr, 0.0, NEG_INF)
        return carry

    lax.fori_loop(0, n_t, bias_body, 0)

    vr = DS_HD + ONES_ROWS
    groups = [(g, slice(g * DS_HD, (g + 1) * DS_HD), slice(g * vr, (g + 1) * vr)) for g in range(DS_KV)]
    for g, gl, gv in groups:
        qg_s[g] = jnp.concatenate(
            [q_ref[:, (g * DS_REP + r) * DS_HD:(g * DS_REP + r + 1) * DS_HD] for r in range(DS_REP)], axis=0)
        s = _dot_nt(km_ref[:, gl], qg_s[g]) + jnp.concatenate([biasm_s[...]] * DS_REP, axis=1)
        m0 = jnp.max(s, axis=0, keepdims=True)
        m_s[g] = m0
        acc_s[g] = _dot(vmt_ref[gv, :], jnp.exp2(s - m0).astype(_MXU))

    def att_body(t, carry):
        start = pl.multiple_of(t * tk, tk)
        bias = jnp.concatenate([bias_s[t]] * DS_REP, axis=1)
        for pair in (groups[:2], groups[2:]):
            ss = [_dot_nt(k_ref[pl.ds(start, tk), gl], qg_s[g]) for g, gl, _ in pair]
            for (g, _, gv), s in zip(pair, ss):
                s = s + bias
                m_prev = m_s[g]
                m_new = jnp.maximum(m_prev, jnp.max(s, axis=0, keepdims=True))
                alpha = jnp.exp2(m_prev - m_new)
                p = jnp.exp2(s - m_new).astype(_MXU)
                acc_s[g] = alpha * acc_s[g] + _dot(vt_ref[t, gv, :], p)
                m_s[g] = m_new
        return carry

    lax.fori_loop(0, n_t, att_body, 0)
    for g in range(DS_KV):
        acc = acc_s[g]
        ot = acc[:DS_HD] * (1.0 / acc[DS_HD:DS_HD + 1])
        for r in range(DS_REP):
            lo = (g * DS_REP + r) * DS_HD
            o_ref[:, lo:lo + DS_HD] = ot[:, r * tq:(r + 1) * tq].T.astype(o_ref.dtype)


def _dsa_frames(q, k, vt, qi, kid, wi, km_blk, vmt_blk, kidm_blk, nb, seq, topk):
    m = q.shape[0]
    tq = _row_tile(seq, 128)
    tk = vt.shape[2]
    assert seq % tk == 0 and tq == LANES
    nt = seq // tq
    nkt = seq // tk
    kvw = DS_KV * DS_HD
    kvr = DS_KV * (DS_HD + ONES_ROWS)
    cur = lambda b, i: (b * nt + i, 0)
    per_b = lambda b, i: (b, 0)
    fixed = lambda b, i: (0, 0)
    return pl.pallas_call(
        functools.partial(_dsa_frames_kernel, tq=tq, tk=tk, topk=topk),
        grid=(nb, nt),
        in_specs=[pl.BlockSpec((tq, DS_WIDTH), cur),
                  pl.BlockSpec((seq, kvw), per_b),
                  pl.BlockSpec((nkt, kvr, tk), lambda b, i: (b, 0, 0)),
                  pl.BlockSpec((tq, IDX_HEADS * IDX_HD), cur),
                  pl.BlockSpec((seq, LANES), per_b),
                  pl.BlockSpec((tq, LANES), cur),
                  pl.BlockSpec((LANES, kvw), fixed), pl.BlockSpec((kvr, LANES), fixed),
                  pl.BlockSpec((LANES, LANES), fixed)],
        out_specs=pl.BlockSpec((tq, DS_WIDTH), cur),
        out_shape=jax.ShapeDtypeStruct((m, DS_WIDTH), _MXU),
        scratch_shapes=[pltpu.VMEM((LANES, tq), jnp.int32), pltpu.VMEM((nkt, tk, tq), jnp.int32),
                        pltpu.VMEM((LANES, tq), _F32), pltpu.VMEM((nkt, tk, tq), _F32),
                        pltpu.VMEM((IDX_HEADS * tq, LANES), _MXU),
                        pltpu.VMEM((LANES, tq), _F32),
                        pltpu.VMEM((DS_KV, DS_REP * tq, DS_HD), _MXU),
                        pltpu.VMEM((DS_KV, 1, DS_REP * tq), _F32),
                        pltpu.VMEM((DS_KV, DS_HD + ONES_ROWS, DS_REP * tq), _F32)],
        compiler_params=_cparams(("parallel", "arbitrary")),
        name="dsa_frames",
    )(q, k, vt, qi, kid, wi, km_blk, vmt_blk, kidm_blk)


def _dsa_small_kernel(*refs, has_cache, nq, n_new, topk):
    if has_cache:
        q_ref, kn_ref, vn_ref, qi_ref, kidn_ref, wi_ref, ck_ref, cv_ref, ckid_ref, o_ref = refs
    else:
        q_ref, kn_ref, vn_ref, qi_ref, kidn_ref, wi_ref, o_ref = refs
    qst = _stack_idx_heads(qi_ref[...])
    wi = wi_ref[...]
    wbs = [jnp.broadcast_to(wi[:, h:h + 1], (nq, LANES)) for h in range(IDX_HEADS)]

    def wb_tile(width):
        return lambda h: jnp.concatenate([wbs[h]] * (width // LANES), axis=1)

    col_n = lax.broadcasted_iota(jnp.int32, (nq, LANES), 1)
    keys = [_order_key(_index_scores(qst, wb_tile(LANES), kidn_ref[...], nq), col_n < n_new)]
    if has_cache:
        past = ckid_ref.shape[1]
        sc = _index_scores(qst, wb_tile(past), ckid_ref[0], nq)
        keys.append(_order_key(sc, jnp.full(sc.shape, True)))

    def count_ge(thr):
        cnt = 0.0
        for kk in keys:
            cnt = cnt + jnp.sum(jnp.where(kk >= thr, 1.0, 0.0), axis=-1, keepdims=True)
        return cnt

    def bit_body(it, u):
        bit = jnp.int32(31) - it
        cand = u | lax.shift_left(jnp.int32(1), bit)
        cnt = count_ge(cand ^ jnp.int32(INT_MIN))
        return jnp.where(cnt >= float(topk), cand, u)

    u = lax.fori_loop(0, 32, bit_body, jnp.zeros((nq, 1), jnp.int32))
    thr = jnp.maximum(u ^ jnp.int32(INT_MIN), jnp.int32(INT_MIN + 1))
    biases = [jnp.where(kk >= thr, 0.0, NEG_INF) for kk in keys]

    for g in range(DS_KV):
        gl = slice(g * DS_HD, (g + 1) * DS_HD)
        qg = jnp.concatenate(
            [q_ref[:, (g * DS_REP + r) * DS_HD:(g * DS_REP + r + 1) * DS_HD] for r in range(DS_REP)], axis=0)
        ks, vs = [kn_ref[:, gl]], [vn_ref[:, gl]]
        if has_cache:
            ks.append(ck_ref[0][:, gl].astype(_MXU))
            vs.append(cv_ref[0][:, gl].astype(_MXU))
        ss = [_dot_nt(qg, kk) + jnp.concatenate([bb] * DS_REP, axis=0) for kk, bb in zip(ks, biases)]
        m = ss[0].max(axis=-1, keepdims=True)
        for s in ss[1:]:
            m = jnp.maximum(m, s.max(axis=-1, keepdims=True))
        l = 0.0
        acc = 0.0
        for s, vv in zip(ss, vs):
            p = jnp.exp2(s - m)
            l = l + jnp.sum(p, axis=-1, keepdims=True)
            acc = acc + _dot(p.astype(_MXU), vv)
        og = acc / l
        for r in range(DS_REP):
            lo = (g * DS_REP + r) * DS_HD
            o_ref[:, lo:lo + DS_HD] = og[r * nq:(r + 1) * nq].astype(o_ref.dtype)


def _dsa_small(q, kn_blk, vn_blk, qi, kidn_blk, wi, cache_k, cache_v, cache_kid, nb, nq, n_new, topk):
    kvw = DS_KV * DS_HD
    has_cache = cache_k is not None
    row = lambda b: (b, 0)
    in_specs = [pl.BlockSpec((nq, DS_WIDTH), row),
                pl.BlockSpec((LANES, kvw), row), pl.BlockSpec((LANES, kvw), row),
                pl.BlockSpec((nq, IDX_HEADS * IDX_HD), row),
                pl.BlockSpec((LANES, LANES), row),
                pl.BlockSpec((nq, LANES), row)]
    args = [q, kn_blk, vn_blk, qi, kidn_blk, wi]
    if has_cache:
        past = cache_k.shape[1]
        in_specs += [pl.BlockSpec((1, past, kvw), lambda b: (b, 0, 0)),
                     pl.BlockSpec((1, past, kvw), lambda b: (b, 0, 0)),
                     pl.BlockSpec((1, past, LANES), lambda b: (b, 0, 0))]
        args += [cache_k, cache_v, cache_kid]
    return pl.pallas_call(
        functools.partial(_dsa_small_kernel, has_cache=has_cache, nq=nq, n_new=n_new, topk=topk),
        grid=(nb,),
        in_specs=in_specs,
        out_specs=pl.BlockSpec((nq, DS_WIDTH), row),
        out_shape=jax.ShapeDtypeStruct((nb * nq, DS_WIDTH), _MXU),
        compiler_params=_cparams(("parallel",)),
        name="dsa_small",
    )(*args)


def _pad_rows_front(a, rows):
    return jnp.pad(a, ((rows - a.shape[0], 0), (0, 0)))


def _pad_rows_back(a, nb, nq, rows):
    w = a.shape[1]
    return jnp.pad(a.reshape(nb, nq, w), ((0, 0), (0, rows - nq), (0, 0))).reshape(nb * rows, w)


def _transpose_with_ones(v, group_cols):
    n = v.shape[0]
    g = v.shape[1] // group_cols
    vt = v.T.reshape(g, group_cols, n)
    pad = jnp.zeros((g, ONES_ROWS, n), v.dtype).at[:, 0, :].set(1)
    return jnp.concatenate([vt, pad], axis=1).reshape(g * (group_cols + ONES_ROWS), n)


def _dup_pairs(a, hd):
    lead = a.shape[:-1]
    g = a.shape[-1] // hd
    a = a.reshape(*lead, g, 1, hd)
    return jnp.broadcast_to(a, (*lead, g, 2, hd)).reshape(*lead, g * 2 * hd)


def kernel(x_prompt, x_sample, cache_l0_k, cache_l0_v, cache_l1_k, cache_l1_v, cache_l2_k, cache_l2_v, cache_l2_kidx, cache_l3_k, cache_l3_v, meta_tokens, l0_norm, l0_w_in, l0_w_out, l0_lam_q1, l0_lam_k1, l0_lam_q2, l0_lam_k2, l0_subln, l1_norm, l1_w_in, l1_w_out, l1_sinks, l2_norm, l2_w_in, l2_w_out, l3_norm, l3_w_in, l3_w_out, l3_lam_q1, l3_lam_k1, l3_lam_q2, l3_lam_k2, l3_subln, final_norm):
    nb, seq, d = x_prompt.shape
    db, ds, _ = x_sample.shape
    past = cache_l0_k.shape[1]
    sw_rows = cache_l1_k.shape[1]
    assert seq % CHUNK == 0 and meta_tokens.shape[0] == N_META and ds <= LANES

    xs = {"f": x_prompt.reshape(nb * seq, d), "m": meta_tokens.astype(_F32), "s": x_sample.reshape(db * ds, d)}
    pos = {"f": N_META + jnp.arange(seq), "m": jnp.arange(N_META), "s": past + jnp.arange(ds)}
    pos["s"] = jnp.tile(pos["s"], db)

    tabs128 = {n: _rope_tables(p, 128, 1.0) for n, p in pos.items()}
    tabs128_da = {n: _rope_tables(p, 128, DA_HD ** -0.5 * LOG2E) for n, p in pos.items()}
    tabs128_ds = {n: _rope_tables(p, 128, DS_HD ** -0.5 * LOG2E) for n, p in pos.items()}
    tabs64 = {n: _rope_tables(p, 64, 1.0) for n, p in pos.items()}
    tabs64_sw = {n: _rope_tables(p, 64, SW_HD ** -0.5 * LOG2E) for n, p in pos.items()}

    layers = [
        dict(norm=l0_norm, w_in=l0_w_in, w_out=l0_w_out, lam=(l0_lam_q1, l0_lam_k1, l0_lam_q2, l0_lam_k2),
             subln=l0_subln, cache=(cache_l0_k, cache_l0_v)),
        dict(norm=l1_norm, w_in=l1_w_in, w_out=l1_w_out, sinks=l1_sinks, cache=(cache_l1_k, cache_l1_v)),
        dict(norm=l2_norm, w_in=l2_w_in, w_out=l2_w_out, cache=(cache_l2_k, cache_l2_v, cache_l2_kidx)),
        dict(norm=l3_norm, w_in=l3_w_in, w_out=l3_w_out, lam=(l3_lam_q1, l3_lam_k1, l3_lam_q2, l3_lam_k2),
             subln=l3_subln, cache=(cache_l3_k, cache_l3_v)),
    ]
    norms = [lp["norm"] for lp in layers[1:]] + [final_norm]
    depth = len(layers)

    hs = {n: _rms_norm_rows(x, layers[0]["norm"]) for n, x in xs.items()}
    p_st, s_st = [], []
    ys = {}
    for li, lp in enumerate(layers):
        kind = li % N_MIXERS
        w_in = lp["w_in"].astype(_MXU)
        w_out = lp["w_out"].astype(_MXU)
        o, z = {}, {}
        if kind == 0:
            lam_init = 0.8 - 0.6 * math.exp(-0.3 * li)
            wq, wk, wv, wz = (w_in[:, i * DA_WIDTH:(i + 1) * DA_WIDTH] for i in range(4))
            lams = jnp.stack([l.astype(_F32) for l in lp["lam"]])
            subln = lp["subln"].reshape(1, 2 * DA_HD).astype(_F32)
            subln_col = jnp.broadcast_to(lp["subln"].astype(_F32)[:, None], (2 * DA_HD, LANES))
            hshape = (DA_HEADS, 2 * DA_HD)
            q, k32, k16, v32, v16 = {}, {}, {}, {}, {}
            for n in xs:
                (q[n],) = _project(hs[n], wq, [_MXU], *tabs128_da[n])
                k32[n], k16[n] = _project(hs[n], wk, [_F32, _MXU], *tabs128[n], head_shape=hshape)
                if n == "f":
                    v32[n], vt_f = _project(hs[n], wv, [_F32], group_cols=2 * DA_HD, head_shape=hshape)
                else:
                    v32[n], v16[n] = _project(hs[n], wv, [_F32, _MXU], head_shape=hshape)
                (z[n],) = _project(hs[n], wz, [_MXU])
            o["f"] = _diff_frames(q["f"], k16["f"], vt_f, k16["m"], _transpose_with_ones(v16["m"], 2 * DA_HD),
                                  lams, subln_col, lam_init, nb, seq)
            o["m"] = _diff_small(q["m"], k16["m"], v16["m"], None, None, lams, subln, lam_init, 1, N_META)
            ck, cv = lp["cache"]
            o["s"] = _diff_small(q["s"], k16["s"], v16["s"], ck.reshape(db, past, DA_WIDTH),
                                 cv.reshape(db, past, DA_WIDTH), lams, subln, lam_init, db, ds)

            def with_meta(f, mrows):
                mm = jnp.broadcast_to(mrows.reshape(1, N_META, *hshape), (nb, N_META, *hshape))
                return jnp.concatenate([mm, f.reshape(nb, seq, *hshape)], axis=1)

            p_st.append([with_meta(k32["f"], k32["m"]), with_meta(v32["f"], v32["m"])])
            s_st.append([k32["s"].reshape(db, ds, *hshape), v32["s"].reshape(db, ds, *hshape)])
        elif kind == 1:
            kvd = SW_KV * SW_HD
            wq = w_in[:, :SW_WIDTH]
            wk = w_in[:, SW_WIDTH:SW_WIDTH + kvd]
            wv = w_in[:, SW_WIDTH + kvd:SW_WIDTH + 2 * kvd]
            wz = w_in[:, SW_WIDTH + 2 * kvd:]
            wkd, wvd = _dup_pairs(wk, SW_HD), _dup_pairs(wv, SW_HD)
            sink_rows = {}
            q, kv32, kd, vd = {}, {}, {}, {}
            for n, rows in (("f", CHUNK), ("m", N_META), ("s", ds)):
                sk = (lp["sinks"].astype(_F32) * LOG2E).reshape(SW_KV, SW_REP, 1, 1)
                sink_rows[n] = jnp.broadcast_to(sk, (SW_KV, SW_REP, rows, LANES)).reshape(SW_KV, SW_REP * rows, LANES)
                (q[n],) = _project(hs[n], wq, [_MXU], *tabs64_sw[n])
                (kd[n],) = _project(hs[n], wkd, [_MXU], *tabs64[n])
                (vd[n],) = _project(hs[n], wvd, [_MXU])
                (z[n],) = _project(hs[n], wz, [_MXU])
            kmeta_blk = _pad_rows_front(kd["m"], WINDOW)
            vmeta_blk = _pad_rows_front(vd["m"], WINDOW)
            o["f"] = _swa_frames(q["f"], kd["f"], vd["f"], kmeta_blk, vmeta_blk, sink_rows["f"], nb, seq)
            o["m"] = _swa_small(q["m"], kd["m"], vd["m"], None, None, sink_rows["m"], 1, N_META)
            ck, cv = lp["cache"]
            ckd = _dup_pairs(ck.reshape(db, sw_rows, kvd), SW_HD).astype(_MXU)
            cvd = _dup_pairs(cv.reshape(db, sw_rows, kvd), SW_HD).astype(_MXU)
            o["s"] = _swa_small(q["s"], kd["s"], vd["s"], ckd, cvd, sink_rows["s"], db, ds)
            assert seq >= sw_rows and sw_rows % 8 == 0
            h_tail = hs["f"].reshape(nb, seq, d)[:, seq - sw_rows:].reshape(nb * sw_rows, d)
            tail_pos = N_META + seq - sw_rows + jnp.arange(sw_rows)
            tail_tabs, tail_shift = _rope_tables(tail_pos, 64, 1.0)
            (kt,) = _project(h_tail, _pad_cols(wk), [_F32], tail_tabs, tail_shift)
            (vt,) = _project(h_tail, _pad_cols(wv), [_F32])
            p_st.append([kt[:, :kvd].reshape(nb, sw_rows, SW_KV, SW_HD), vt[:, :kvd].reshape(nb, sw_rows, SW_KV, SW_HD)])
            (ks,) = _project(hs["s"], _pad_cols(wk), [_F32], *tabs64["s"])
            (vs_,) = _project(hs["s"], _pad_cols(wv), [_F32])
            ks = ks[:, :kvd].reshape(db, ds, SW_KV, SW_HD)
            vs_ = vs_[:, :kvd].reshape(db, ds, SW_KV, SW_HD)
            k_all = jnp.concatenate([ck, ks], axis=1)
            v_all = jnp.concatenate([cv, vs_], axis=1)
            s_st.append([k_all[:, k_all.shape[1] - sw_rows:], v_all[:, v_all.shape[1] - sw_rows:]])
        else:
            kvd = DS_KV * DS_HD
            c0 = 0
            wq = w_in[:, c0:c0 + DS_WIDTH]; c0 += DS_WIDTH
            wk = w_in[:, c0:c0 + kvd]; c0 += kvd
            wv = w_in[:, c0:c0 + kvd]; c0 += kvd
            wz = w_in[:, c0:c0 + DS_WIDTH]; c0 += DS_WIDTH
            wqi = w_in[:, c0:c0 + IDX_HEADS * IDX_HD]; c0 += IDX_HEADS * IDX_HD
            wki = w_in[:, c0:c0 + IDX_HD]; c0 += IDX_HD
            wwi = w_in[:, c0:c0 + IDX_HEADS]
            wkid = jnp.concatenate([wki, wki], axis=1)
            wwi_p = _pad_cols(wwi)
            wi_scale = IDX_HEADS ** -0.5 * IDX_HD ** -0.5
            q, k32, k16, v32, v16, qi, kid32, kid16, wi = {}, {}, {}, {}, {}, {}, {}, {}, {}
            for n in xs:
                (q[n],) = _project(hs[n], wq, [_MXU], *tabs128_ds[n])
                k32[n], k16[n] = _project(hs[n], wk, [_F32, _MXU], *tabs128[n])
                if n == "f":
                    v32[n], vt_f = _project(hs[n], wv, [_F32], group_cols=DS_HD)
                else:
                    v32[n], v16[n] = _project(hs[n], wv, [_F32, _MXU])
                (z[n],) = _project(hs[n], wz, [_MXU])
                (qi[n],) = _project(hs[n], wqi, [_MXU], *tabs64[n])
                kid32[n], kid16[n] = _project(hs[n], wkid, [_F32, _MXU], *tabs64[n])
                (wi[n],) = _project(hs[n], wwi_p, [_F32], scale=wi_scale)
            topk_p = min(TOPK_MAX, seq // 4)
            topk_s = min(TOPK_MAX, (past + ds) // 4)
            km_blk = _pad_rows_back(k16["m"], 1, N_META, LANES)
            vm_blk = _pad_rows_back(v16["m"], 1, N_META, LANES)
            kidm_blk = _pad_rows_back(kid16["m"], 1, N_META, LANES)
            o["f"] = _dsa_frames(q["f"], k16["f"], vt_f, qi["f"], kid16["f"], wi["f"],
                                 km_blk, _transpose_with_ones(vm_blk, DS_HD), kidm_blk, nb, seq, topk_p)
            o["m"] = _dsa_small(q["m"], km_blk, vm_blk, qi["m"], kidm_blk, wi["m"], None, None, None,
                                1, N_META, N_META, topk_p)
            ck, cv, cki = lp["cache"]
            ckid = jnp.concatenate([cki, cki], axis=-1).astype(_MXU)
            o["s"] = _dsa_small(q["s"], _pad_rows_back(k16["s"], db, ds, LANES), _pad_rows_back(v16["s"], db, ds, LANES),
                                qi["s"], _pad_rows_back(kid16["s"], db, ds, LANES), wi["s"],
                                ck.reshape(db, past, kvd), cv.reshape(db, past, kvd), ckid,
                                db, ds, ds, topk_s)

            def with_meta(f, mrows, *tail):
                mm = jnp.broadcast_to(mrows.reshape(1, N_META, *tail), (nb, N_META, *tail))
                return jnp.concatenate([mm, f.reshape(nb, seq, *tail)], axis=1)

            p_st.append([with_meta(k32["f"], k32["m"], DS_KV, DS_HD), with_meta(v32["f"], v32["m"], DS_KV, DS_HD),
                         with_meta(kid32["f"][:, :IDX_HD], kid32["m"][:, :IDX_HD], IDX_HD)])
            s_st.append([k32["s"].reshape(db, ds, DS_KV, DS_HD), v32["s"].reshape(db, ds, DS_KV, DS_HD),
                         kid32["s"][:, :IDX_HD].reshape(db, ds, IDX_HD)])
        last = li == depth - 1
        for n in xs:
            res = _out_project(o[n], z[n], w_out, xs[n], norms[li], last)
            if last:
                (ys[n],) = res
            else:
                xs[n], hs[n] = res

    y_prompt = ys["f"].reshape(nb, seq, d)
    y_sample = ys["s"].reshape(db, ds, d)
    return (y_prompt, y_sample,
            p_st[0][0], p_st[0][1], s_st[0][0], s_st[0][1],
            p_st[1][0], p_st[1][1], s_st[1][0], s_st[1][1],
            p_st[2][0], p_st[2][1], p_st[2][2], s_st[2][0], s_st[2][1], s_st[2][2],
            p_st[3][0], p_st[3][1], s_st[3][0], s_st[3][1])


def _pad_cols(w):
    n = w.shape[1]
    return jnp.pad(w, ((0, 0), (0, (-n) % LANES)))
```

```python
import functools
import math

import jax
import jax.numpy as jnp
from jax import lax
from jax.experimental import pallas as pl
from jax.experimental.pallas import tpu as pltpu

CHUNK = 64
N_META = 16
ROPE_THETA = 500000.0
NORM_EPS = 1e-6
NEG_INF = -1e30
DA_HEADS = 8
DA_HD = 128
DA_WIDTH = DA_HEADS * 2 * DA_HD
SW_HEADS = 32
SW_KV = 4
SW_HD = 64
SW_WIDTH = SW_HEADS * SW_HD
SW_REP = SW_HEADS // SW_KV
WINDOW = 128
DS_HEADS = 16
DS_KV = 4
DS_HD = 128
DS_WIDTH = DS_HEADS * DS_HD
DS_REP = DS_HEADS // DS_KV
IDX_HEADS = 16
IDX_HD = 64
TOPK_MAX = 256
N_MIXERS = 3

LANES = 128
INT_MIN = -(2 ** 31)
LOG2E = math.log2(math.e)
ONES_ROWS = 16
VMEM_LIMIT = 56 * 1024 * 1024

_MXU = jnp.bfloat16
_F32 = jnp.float32

_NT = (((1,), (1,)), ((), ()))


def _cparams(sem):
    return pltpu.CompilerParams(dimension_semantics=sem, vmem_limit_bytes=VMEM_LIMIT)


def _row_tile(m, pref):
    t = pref
    while t > 8 and m % t:
        t //= 2
    assert m % t == 0, (m, pref)
    return t


def _dot(a, b):
    return jnp.dot(a, b, preferred_element_type=_F32)


def _dot_nt(a, b):
    return lax.dot_general(a, b, _NT, preferred_element_type=_F32)


def _norm_kernel(x_ref, g_ref, h_ref):
    x = x_ref[...]
    y = x * lax.rsqrt(jnp.mean(x * x, axis=-1, keepdims=True) + NORM_EPS)
    h_ref[...] = (y * g_ref[...]).astype(h_ref.dtype)


def _rms_norm_rows(x, g):
    m, d = x.shape
    out_dtype = _MXU
    tm = _row_tile(m, 512)
    return pl.pallas_call(
        _norm_kernel,
        grid=(m // tm,),
        in_specs=[pl.BlockSpec((tm, d), lambda i: (i, 0)),
                  pl.BlockSpec((1, d), lambda i: (0, 0))],
        out_specs=pl.BlockSpec((tm, d), lambda i: (i, 0)),
        out_shape=jax.ShapeDtypeStruct((m, d), out_dtype),
        compiler_params=_cparams(("parallel",)),
        name="rms_norm",
    )(x, g.reshape(1, d).astype(_F32))


def _proj_kernel(*refs, rope_shift, scale, n_out, tn, group_cols):
    h_ref, w_ref = refs[0], refs[1]
    pos = 2
    if rope_shift is not None:
        c_ref, a_ref, b_ref = refs[2:5]
        pos = 5
    outs = refs[pos:pos + n_out]
    ot_ref = refs[pos + n_out] if group_cols else None
    if ot_ref is not None:
        tm = ot_ref.shape[2]
        row = lax.broadcasted_iota(jnp.int32, (ONES_ROWS, tm), 0)
        ones_blk = jnp.where(row == 0, 1.0, 0.0).astype(ot_ref.dtype)
        for g in range(tn // group_cols):
            base = g * (group_cols + ONES_ROWS) + group_cols
            ot_ref[0, base:base + ONES_ROWS, :] = ones_blk
    h = h_ref[...]
    cw = 2 * LANES if tn % (2 * LANES) == 0 else LANES
    for c in range(tn // cw):
        wide = _dot(h, w_ref[:, c * cw:(c + 1) * cw])
        for sub in range(cw // LANES):
            acc = wide[:, sub * LANES:(sub + 1) * LANES]
            if rope_shift is not None:
                acc = (acc * c_ref[...]
                       + pltpu.roll(acc, rope_shift, 1) * a_ref[...]
                       + pltpu.roll(acc, LANES - rope_shift, 1) * b_ref[...])
            elif scale is not None:
                acc = acc * scale
            cols = slice(c * cw + sub * LANES, c * cw + (sub + 1) * LANES)
            for o in outs:
                o[:, cols] = acc.astype(o.dtype)
            if ot_ref is not None:
                lo = cols.start
                r0 = (lo // group_cols) * (group_cols + ONES_ROWS) + lo % group_cols
                ot_ref[0, r0:r0 + LANES, :] = acc.T.astype(ot_ref.dtype)


def _project(h, w, out_dtypes, tabs=None, rope_shift=None, scale=None, group_cols=0):
    m, d = h.shape
    n = w.shape[1]
    tm = _row_tile(m if tabs is None else math.gcd(m, tabs[0].shape[0]), 512)
    tn = n if n <= 2048 else 2048
    assert n % tn == 0 and tn % LANES == 0
    in_specs = [pl.BlockSpec((tm, d), lambda j, i: (i, 0)),
                pl.BlockSpec((d, tn), lambda j, i: (0, j))]
    args = [h, w]
    if tabs is not None:
        trows = tabs[0].shape[0]
        assert trows % tm == 0
        nrep = trows // tm
        for t in tabs:
            in_specs.append(pl.BlockSpec((tm, LANES), lambda j, i: (i % nrep, 0)))
            args.append(t)
    else:
        rope_shift = None
    out_specs = [pl.BlockSpec((tm, tn), lambda j, i: (i, j)) for _ in out_dtypes]
    out_shape = [jax.ShapeDtypeStruct((m, n), dt) for dt in out_dtypes]
    if group_cols:
        assert tn % group_cols == 0 and group_cols % LANES == 0
        grow = group_cols + ONES_ROWS
        out_specs.append(pl.BlockSpec((1, tn // group_cols * grow, tm), lambda j, i: (i, j, 0)))
        out_shape.append(jax.ShapeDtypeStruct((m // tm, n // group_cols * grow, tm), _MXU))
    outs = pl.pallas_call(
        functools.partial(_proj_kernel, rope_shift=rope_shift, scale=scale,
                          n_out=len(out_dtypes), tn=tn, group_cols=group_cols),
        grid=(n // tn, m // tm),
        in_specs=in_specs,
        out_specs=out_specs,
        out_shape=out_shape,
        compiler_params=_cparams(("parallel", "parallel")),
        name="in_proj",
    )(*args)
    return outs


def _outproj_kernel(o_ref, z_ref, w_ref, x_ref, g_ref, *outs, emit_x):
    z = z_ref[...].astype(_F32)
    a = o_ref[...].astype(_F32) * (z / (1.0 + jnp.exp(-z)))
    y = x_ref[...] + _dot(a.astype(w_ref.dtype), w_ref[...])
    if emit_x:
        outs[0][...] = y
    n = y * lax.rsqrt(jnp.mean(y * y, axis=-1, keepdims=True) + NORM_EPS) * g_ref[...]
    outs[-1][...] = n.astype(outs[-1].dtype)


def _out_project(o, z, w, x, g_next, last):
    m, d = x.shape
    wd = o.shape[1]
    tm = _row_tile(m, 512)
    row = lambda i: (i, 0)
    fixed = lambda i: (0, 0)
    if last:
        out_shape = [jax.ShapeDtypeStruct((m, d), _F32)]
    else:
        out_shape = [jax.ShapeDtypeStruct((m, d), _F32), jax.ShapeDtypeStruct((m, d), _MXU)]
    return pl.pallas_call(
        functools.partial(_outproj_kernel, emit_x=not last),
        grid=(m // tm,),
        in_specs=[pl.BlockSpec((tm, wd), row), pl.BlockSpec((tm, wd), row),
                  pl.BlockSpec((wd, d), fixed), pl.BlockSpec((tm, d), row),
                  pl.BlockSpec((1, d), fixed)],
        out_specs=[pl.BlockSpec((tm, d), row) for _ in out_shape],
        out_shape=out_shape,
        compiler_params=_cparams(("parallel",)),
        name="out_proj",
    )(o, z, w, x, g_next.reshape(1, d).astype(_F32))


def _rope_tables(pos, hd, scale=1.0):
    rd = hd // 4
    half = rd // 2
    lane = jnp.arange(LANES)
    d = lane % hd
    f = d % half
    inv_freq = ROPE_THETA ** (-(jnp.arange(half, dtype=_F32) * 2.0 / rd))
    ang = pos.astype(_F32)[:, None] * inv_freq[f][None, :]
    cos, sin = jnp.cos(ang), jnp.sin(ang)
    in_rot = (d < rd)[None, :]
    c = jnp.where(in_rot, cos, 1.0)
    a = jnp.where(((d >= half) & (d < rd))[None, :], sin, 0.0)
    b = jnp.where((d < half)[None, :], -sin, 0.0)
    s = jnp.float32(scale)
    return (c * s, a * s, b * s), half


def _lam_value(lam_ref, lam_init):
    l = lam_ref[...]
    s1 = jnp.sum(l[0:1] * l[1:2], axis=-1, keepdims=True)
    s2 = jnp.sum(l[2:3] * l[3:4], axis=-1, keepdims=True)
    return jnp.exp(s1) - jnp.exp(s2) + lam_init


def _diff_finish(acc1, l1, acc2, l2, lam, sub, lam_init):
    o = acc1 / l1 - lam * (acc2 / l2)
    o = o * lax.rsqrt(jnp.mean(o * o, axis=-1, keepdims=True) + NORM_EPS) * sub
    return o * (1.0 - lam_init)


def _diff_frames_kernel(q_ref, k_ref, vt_ref, km_ref, vmt_ref, lam_ref, subc_ref, o_ref,
                        acc_s, sa_s, sb_s, pa_s, pb_s, *, tq, lam_init):
    i = pl.program_id(2)
    hw = 2 * DA_HD

    def q_half(j):
        return q_ref[:, j * DA_HD:(j + 1) * DA_HD]

    km = km_ref[...]
    vmt = vmt_ref[...]
    ms = []
    for j in range(2):
        s = _dot_nt(km[:, j * DA_HD:(j + 1) * DA_HD], q_half(j))
        m = jnp.max(s, axis=0, keepdims=True)
        ms.append(m)
        acc_s[j] = _dot(vmt, jnp.exp2(s - m).astype(vmt.dtype))

    def scores_into(dst, t):
        start = pl.multiple_of(t * tq, tq)
        for j in range(2):
            dst[j] = _dot_nt(k_ref[pl.ds(start, tq), j * DA_HD:(j + 1) * DA_HD], q_half(j))

    def value_update(t, a, p_rd):
        vt = vt_ref[t]
        for j in range(2):
            acc_s[j] = a[j] * acc_s[j] + _dot(vt, p_rd[j])

    def tile_step(t, carry, src, dst, p_rd, p_wr, masked, has_next):
        m, a = carry
        if masked:
            kc = lax.broadcasted_iota(jnp.int32, (tq, tq), 0) // CHUNK
            qc = lax.broadcasted_iota(jnp.int32, (tq, tq), 1) // CHUNK
            ok = kc <= qc
        m_out, a_out = [], []
        for j in range(2):
            s = src[j]
            if masked:
                s = jnp.where(ok, s, NEG_INF)
            m_new = jnp.maximum(m[j], jnp.max(s, axis=0, keepdims=True))
            a_out.append(jnp.exp2(m[j] - m_new))
            m_out.append(m_new)
            p_wr[j] = jnp.exp2(s - m_new).astype(p_wr.dtype)
        value_update(jnp.maximum(t - 1, 0), a, p_rd)
        if has_next:
            scores_into(dst, t + 1)
        return tuple(m_out), tuple(a_out)

    scores_into(sa_s, 0)
    pb_s[...] = jnp.zeros_like(pb_s)
    one = jnp.ones((1, tq), _F32)
    carry = ((ms[0], ms[1]), (one, one))

    def pair_body(u, carry):
        carry = tile_step(2 * u, carry, sa_s, sb_s, pb_s, pa_s, False, True)
        return tile_step(2 * u + 1, carry, sb_s, sa_s, pa_s, pb_s, False, True)

    carry = lax.fori_loop(0, i // 2, pair_body, carry)

    @pl.when(i % 2 == 0)
    def _():
        _, a = tile_step(i, carry, sa_s, sb_s, pb_s, pa_s, True, False)
        value_update(i, a, pa_s)

    @pl.when(i % 2 == 1)
    def _():
        c = tile_step(i - 1, carry, sa_s, sb_s, pb_s, pa_s, False, True)
        _, a = tile_step(i, c, sb_s, sa_s, pa_s, pb_s, True, False)
        value_update(i, a, pb_s)

    acc0 = acc_s[0]
    acc1 = acc_s[1]
    lam = _lam_value(lam_ref, lam_init)
    inv0 = 1.0 / acc0[hw:hw + 1]
    inv1 = lam / acc1[hw:hw + 1]
    ot = acc0[:hw] * inv0 - acc1[:hw] * inv1
    sub = jnp.concatenate([subc_ref[...]] * (tq // LANES), axis=1)
    ot = ot * lax.rsqrt(jnp.mean(ot * ot, axis=0, keepdims=True) + NORM_EPS) * sub * (1.0 - lam_init)
    o_ref[...] = ot.T.astype(o_ref.dtype)


def _diff_frames(q, k, vt, km, vmt, lams, subln_col, lam_init, nb, seq):
    m = q.shape[0]
    hw = 2 * DA_HD
    hr = hw + ONES_ROWS
    tq = vt.shape[2]
    assert seq % tq == 0 and tq % LANES == 0
    nt = seq // tq
    return pl.pallas_call(
        functools.partial(_diff_frames_kernel, tq=tq, lam_init=lam_init),
        grid=(nb, DA_HEADS, nt),
        in_specs=[pl.BlockSpec((tq, hw), lambda b, h, i: (b * nt + i, h)),
                  pl.BlockSpec((seq, hw), lambda b, h, i: (b, h)),
                  pl.BlockSpec((nt, hr, tq), lambda b, h, i: (b, h, 0)),
                  pl.BlockSpec((N_META, hw), lambda b, h, i: (0, h)),
                  pl.BlockSpec((hr, N_META), lambda b, h, i: (h, 0)),
                  pl.BlockSpec((4, DA_HD), lambda b, h, i: (0, 0)),
                  pl.BlockSpec((hw, LANES), lambda b, h, i: (0, 0))],
        out_specs=pl.BlockSpec((tq, hw), lambda b, h, i: (b * nt + i, h)),
        out_shape=jax.ShapeDtypeStruct((m, DA_WIDTH), _MXU),
        scratch_shapes=[pltpu.VMEM((2, hr, tq), _F32),
                        pltpu.VMEM((2, tq, tq), _F32), pltpu.VMEM((2, tq, tq), _F32),
                        pltpu.VMEM((2, tq, tq), _MXU), pltpu.VMEM((2, tq, tq), _MXU)],
        compiler_params=_cparams(("parallel", "parallel", "arbitrary")),
        name="diff_frames",
    )(q, k, vt, km, vmt, lams, subln_col)


def _diff_small_kernel(*refs, has_cache, lam_init):
    if has_cache:
        q_ref, k_ref, v_ref, ck_ref, cv_ref, lam_ref, sub_ref, o_ref = refs
    else:
        q_ref, k_ref, v_ref, lam_ref, sub_ref, o_ref = refs
    q = q_ref[...]
    kn = k_ref[...]
    vn = v_ref[...]
    parts = [(kn, vn)]
    if has_cache:
        parts.append((ck_ref[0].astype(kn.dtype), cv_ref[0].astype(vn.dtype)))
    accs, ls = [], []
    for j in range(2):
        qj = q[:, j * DA_HD:(j + 1) * DA_HD]
        ss = [_dot_nt(qj, kp[:, j * DA_HD:(j + 1) * DA_HD]) for kp, _ in parts]
        m = ss[0].max(axis=-1, keepdims=True)
        for s in ss[1:]:
            m = jnp.maximum(m, s.max(axis=-1, keepdims=True))
        l = 0.0
        acc = 0.0
        for s, (_, vp) in zip(ss, parts):
            p = jnp.exp2(s - m)
            l = l + jnp.sum(p, axis=-1, keepdims=True)
            acc = acc + _dot(p.astype(vp.dtype), vp)
        accs.append(acc)
        ls.append(l)
    lam = _lam_value(lam_ref, lam_init)
    o = _diff_finish(accs[0], ls[0], accs[1], ls[1], lam, sub_ref[...], lam_init)
    o_ref[...] = o.astype(o_ref.dtype)


def _diff_small(q, k, v, cache_k, cache_v, lams, subln, lam_init, nb, nq):
    hw = 2 * DA_HD
    has_cache = cache_k is not None
    qspec = pl.BlockSpec((nq, hw), lambda b, h: (b, h))
    in_specs = [qspec, qspec, qspec]
    args = [q, k, v]
    if has_cache:
        past = cache_k.shape[1]
        cspec = pl.BlockSpec((1, past, hw), lambda b, h: (b, 0, h))
        in_specs += [cspec, cspec]
        args += [cache_k, cache_v]
    in_specs += [pl.BlockSpec((4, DA_HD), lambda b, h: (0, 0)),
                 pl.BlockSpec((1, hw), lambda b, h: (0, 0))]
    args += [lams, subln]
    return pl.pallas_call(
        functools.partial(_diff_small_kernel, has_cache=has_cache, lam_init=lam_init),
        grid=(nb, DA_HEADS),
        in_specs=in_specs,
        out_specs=qspec,
        out_shape=jax.ShapeDtypeStruct((nb * nq, DA_WIDTH), _MXU),
        compiler_params=_cparams(("parallel", "parallel")),
        name="diff_small",
    )(*args)


def _stack_heads(qg, nheads):
    n = qg.shape[0]
    lane = lax.broadcasted_iota(jnp.int32, (n, LANES), 1)
    lo = lane < SW_HD
    pieces = []
    for r in range(nheads):
        blk = qg[:, (r // 2) * LANES:(r // 2 + 1) * LANES]
        keep = lo if r % 2 == 0 else jnp.logical_not(lo)
        pieces.append(jnp.where(keep, blk, jnp.zeros_like(blk)))
    return jnp.concatenate(pieces, axis=0)


def _unstack_heads(o, n, nheads):
    lane = lax.broadcasted_iota(jnp.int32, (n, LANES), 1)
    lo = lane < SW_HD
    return [jnp.where(lo, o[(2 * p) * n:(2 * p + 1) * n], o[(2 * p + 1) * n:(2 * p + 2) * n])
            for p in range(nheads // 2)]


def _sink_softmax_pv(ss, vs, sk):
    m = sk
    for s in ss:
        m = jnp.maximum(m, jnp.max(s, axis=-1, keepdims=True))
    den = jnp.exp2(sk - m)
    acc = 0.0
    for s, v in zip(ss, vs):
        e = jnp.exp2(s - m)
        den = den + jnp.sum(e, axis=-1, keepdims=True)
        acc = acc + _dot(e.astype(v.dtype), v)
    return acc / den


def _swa_frames_kernel(q_ref, kc_ref, vc_ref, kp_ref, vp_ref, km_ref, vm_ref, sk_ref, o_ref,
                       kcat, vcat, *, tq):
    i = pl.program_id(1)
    first = i == 0
    prev_rows = WINDOW
    kcat[0:prev_rows] = jnp.where(first, km_ref[...], kp_ref[...])
    vcat[0:prev_rows] = jnp.where(first, vm_ref[...], vp_ref[...])
    kcat[prev_rows:prev_rows + tq] = kc_ref[...]
    vcat[prev_rows:prev_rows + tq] = vc_ref[...]
    nwin = WINDOW + CHUNK
    for c in range(tq // CHUNK):
        rows = slice(c * CHUNK, (c + 1) * CHUNK)
        win = slice(c * CHUNK, c * CHUNK + nwin)
        for g in range(SW_KV):
            gl = slice(g * LANES, (g + 1) * LANES)
            qst = _stack_heads(q_ref[rows, g * SW_REP * SW_HD:(g + 1) * SW_REP * SW_HD], SW_REP)
            s = _dot_nt(qst, kcat[win, gl])
            if c * CHUNK < prev_rows - N_META:
                col = lax.broadcasted_iota(jnp.int32, s.shape, 1) + c * CHUNK
                ok = jnp.logical_or(col >= prev_rows - N_META, jnp.logical_not(first))
                s = jnp.where(ok, s, NEG_INF)
            o = _sink_softmax_pv([s], [vcat[win, gl]], sk_ref[g][:, 0:1])
            for p, blk in enumerate(_unstack_heads(o, CHUNK, SW_REP)):
                lo = g * SW_REP * SW_HD + p * LANES
                o_ref[rows, lo:lo + LANES] = blk.astype(o_ref.dtype)


def _swa_frames(q, kd, vd, kmeta_blk, vmeta_blk, sink_rows, nb, seq):
    m = q.shape[0]
    tq = _row_tile(seq, 512)
    nt = seq // tq
    pw = SW_KV * LANES
    per = tq // WINDOW
    cur = lambda b, i: (b * nt + i, 0)
    prev = lambda b, i: (jnp.maximum((b * nt + i) * per - 1, 0), 0)
    fixed = lambda b, i: (0, 0)
    return pl.pallas_call(
        functools.partial(_swa_frames_kernel, tq=tq),
        grid=(nb, nt),
        in_specs=[pl.BlockSpec((tq, SW_WIDTH), cur),
                  pl.BlockSpec((tq, pw), cur), pl.BlockSpec((tq, pw), cur),
                  pl.BlockSpec((WINDOW, pw), prev), pl.BlockSpec((WINDOW, pw), prev),
                  pl.BlockSpec((WINDOW, pw), fixed), pl.BlockSpec((WINDOW, pw), fixed),
                  pl.BlockSpec((SW_KV, SW_REP * CHUNK, LANES), lambda b, i: (0, 0, 0))],
        out_specs=pl.BlockSpec((tq, SW_WIDTH), cur),
        out_shape=jax.ShapeDtypeStruct((m, SW_WIDTH), _MXU),
        scratch_shapes=[pltpu.VMEM((WINDOW + tq, pw), _MXU), pltpu.VMEM((WINDOW + tq, pw), _MXU)],
        compiler_params=_cparams(("parallel", "arbitrary")),
        name="swa_frames",
    )(q, kd, vd, kd, vd, kmeta_blk, vmeta_blk, sink_rows)


def _swa_small_kernel(*refs, has_cache, nq):
    if has_cache:
        q_ref, k_ref, v_ref, ck_ref, cv_ref, sk_ref, o_ref = refs
    else:
        q_ref, k_ref, v_ref, sk_ref, o_ref = refs
    for g in range(SW_KV):
        gl = slice(g * LANES, (g + 1) * LANES)
        qst = _stack_heads(q_ref[:, g * SW_REP * SW_HD:(g + 1) * SW_REP * SW_HD], SW_REP)
        ks, vs = [k_ref[:, gl]], [v_ref[:, gl]]
        if has_cache:
            ks.append(ck_ref[0][:, gl])
            vs.append(cv_ref[0][:, gl])
        ss = [_dot_nt(qst, kk) for kk in ks]
        o = _sink_softmax_pv(ss, vs, sk_ref[g][:, 0:1])
        for p, blk in enumerate(_unstack_heads(o, nq, SW_REP)):
            lo = g * SW_REP * SW_HD + p * LANES
            o_ref[:, lo:lo + LANES] = blk.astype(o_ref.dtype)


def _swa_small(q, kd, vd, cache_kd, cache_vd, sink_rows, nb, nq):
    pw = SW_KV * LANES
    has_cache = cache_kd is not None
    row = lambda b: (b, 0)
    in_specs = [pl.BlockSpec((nq, SW_WIDTH), row), pl.BlockSpec((nq, pw), row), pl.BlockSpec((nq, pw), row)]
    args = [q, kd, vd]
    if has_cache:
        past = cache_kd.shape[1]
        cspec = pl.BlockSpec((1, past, pw), lambda b: (b, 0, 0))
        in_specs += [cspec, cspec]
        args += [cache_kd, cache_vd]
    in_specs.append(pl.BlockSpec((SW_KV, SW_REP * nq, LANES), lambda b: (0, 0, 0)))
    args.append(sink_rows)
    return pl.pallas_call(
        functools.partial(_swa_small_kernel, has_cache=has_cache, nq=nq),
        grid=(nb,),
        in_specs=in_specs,
        out_specs=pl.BlockSpec((nq, SW_WIDTH), row),
        out_shape=jax.ShapeDtypeStruct((nb * nq, SW_WIDTH), _MXU),
        compiler_params=_cparams(("parallel",)),
        name="swa_small",
    )(*args)


def _order_key(score, valid):
    bits = lax.bitcast_convert_type(score + 0.0, jnp.int32)
    key = jnp.where(bits < 0, bits ^ jnp.int32(0x7FFFFFFF), bits)
    return jnp.where(valid, key, jnp.int32(INT_MIN))


def _stack_idx_heads(qi):
    return _stack_heads(qi, IDX_HEADS)


def _index_scores(qst, wb_of, kid, n):
    score = None
    hb = 4
    for h0 in range(0, IDX_HEADS, hb):
        sc = _dot_nt(qst[h0 * n:(h0 + hb) * n], kid)
        for h in range(h0, h0 + hb):
            part = jnp.maximum(sc[(h - h0) * n:(h - h0 + 1) * n], 0.0) * wb_of(h)
            score = part if score is None else score + part
    return score


def _index_scores_t(kid, qst, wit, n):
    score = None
    hb = 4
    for h0 in range(0, IDX_HEADS, hb):
        sc = _dot_nt(kid, qst[h0 * n:(h0 + hb) * n])
        for h in range(h0, h0 + hb):
            part = jnp.maximum(sc[:, (h - h0) * n:(h - h0 + 1) * n], 0.0) * wit[h:h + 1, :]
            score = part if score is None else score + part
    return score


def _dsa_frames_kernel(q_ref, k_ref, vt_ref, qi_ref, kid_ref, wi_ref, km_ref, vmt_ref, kidm_ref, o_ref,
                       keym_s, key_s, biasm_s, bias_s, qst_s, wit_s, qg_s, m_s, acc_s, *, tq, tk, topk):
    i = pl.program_id(1)
    n_t = (i * tq + tq + tk - 1) // tk
    qst_s[...] = _stack_idx_heads(qi_ref[...])
    wit_s[...] = wi_ref[...].T
    sc_m = _index_scores_t(kidm_ref[...], qst_s[...], wit_s[...], tq)
    row_m = lax.broadcasted_iota(jnp.int32, (LANES, tq), 0)
    keym_s[...] = _order_key(sc_m, row_m < N_META)

    q_chunk = (lax.broadcasted_iota(jnp.int32, (tk, tq), 1) + i * tq) // CHUNK

    def score_body(t, carry):
        start = pl.multiple_of(t * tk, tk)
        sc = _index_scores_t(kid_ref[pl.ds(start, tk), :], qst_s[...], wit_s[...], tq)
        k_chunk = (lax.broadcasted_iota(jnp.int32, (tk, tq), 0) + t * tk) // CHUNK
        key_s[t] = _order_key(sc, k_chunk <= q_chunk)
        return carry

    lax.fori_loop(0, n_t, score_body, 0)

    def count_ge(thr):
        def fold(x):
            return jnp.sum(x.reshape(x.shape[0] // 64, 64, tq), axis=0)

        part = fold(jnp.where(keym_s[...] >= thr, 1.0, 0.0))

        def cbody(t, acc):
            return acc + fold(jnp.where(key_s[t] >= thr, 1.0, 0.0))

        part = lax.fori_loop(0, n_t, cbody, part)
        return jnp.sum(part, axis=0, keepdims=True)

    def bit_body(it, u):
        bit = jnp.int32(31) - it
        cand = u | lax.shift_left(jnp.int32(1), bit)
        cnt = count_ge(cand ^ jnp.int32(INT_MIN))
        return jnp.where(cnt >= float(topk), cand, u)

    u = lax.fori_loop(0, 32, bit_body, jnp.zeros((1, tq), jnp.int32))
    thr = jnp.maximum(u ^ jnp.int32(INT_MIN), jnp.int32(INT_MIN + 1))

    biasm_s[...] = jnp.where(keym_s[...] >= thr, 0.0, NEG_INF)

    def bias_body(t, carry):
        bias_s[t] = jnp.where(key_s[t] >= thr, 0.0, NEG_INF)
        return carry

    lax.fori_loop(0, n_t, bias_body, 0)

    vr = DS_HD + ONES_ROWS
    groups = [(g, slice(g * DS_HD, (g + 1) * DS_HD), slice(g * vr, (g + 1) * vr)) for g in range(DS_KV)]
    for g, gl, gv in groups:
        qg_s[g] = jnp.concatenate(
            [q_ref[:, (g * DS_REP + r) * DS_HD:(g * DS_REP + r + 1) * DS_HD] for r in range(DS_REP)], axis=0)
        s = _dot_nt(km_ref[:, gl], qg_s[g]) + jnp.concatenate([biasm_s[...]] * DS_REP, axis=1)
        m0 = jnp.max(s, axis=0, keepdims=True)
        m_s[g] = m0
        acc_s[g] = _dot(vmt_ref[gv, :], jnp.exp2(s - m0).astype(_MXU))

    def att_body(t, carry):
        start = pl.multiple_of(t * tk, tk)
        bias = jnp.concatenate([bias_s[t]] * DS_REP, axis=1)
        for pair in (groups[:2], groups[2:]):
            ss = [_dot_nt(k_ref[pl.ds(start, tk), gl], qg_s[g]) for g, gl, _ in pair]
            for (g, _, gv), s in zip(pair, ss):
                s = s + bias
                m_prev = m_s[g]
                m_new = jnp.maximum(m_prev, jnp.max(s, axis=0, keepdims=True))
                alpha = jnp.exp2(m_prev - m_new)
                p = jnp.exp2(s - m_new).astype(_MXU)
                acc_s[g] = alpha * acc_s[g] + _dot(vt_ref[t, gv, :], p)
                m_s[g] = m_new
        return carry

    lax.fori_loop(0, n_t, att_body, 0)
    for g in range(DS_KV):
        acc = acc_s[g]
        ot = acc[:DS_HD] * (1.0 / acc[DS_HD:DS_HD + 1])
        for r in range(DS_REP):
            lo = (g * DS_REP + r) * DS_HD
            o_ref[:, lo:lo + DS_HD] = ot[:, r * tq:(r + 1) * tq].T.astype(o_ref.dtype)


def _dsa_frames(q, k, vt, qi, kid, wi, km_blk, vmt_blk, kidm_blk, nb, seq, topk):
    m = q.shape[0]
    tq = _row_tile(seq, 128)
    tk = vt.shape[2]
    assert seq % tk == 0 and tq == LANES
    nt = seq // tq
    nkt = seq // tk
    kvw = DS_KV * DS_HD
    kvr = DS_KV * (DS_HD + ONES_ROWS)
    cur = lambda b, i: (b * nt + i, 0)
    per_b = lambda b, i: (b, 0)
    fixed = lambda b, i: (0, 0)
    return pl.pallas_call(
        functools.partial(_dsa_frames_kernel, tq=tq, tk=tk, topk=topk),
        grid=(nb, nt),
        in_specs=[pl.BlockSpec((tq, DS_WIDTH), cur),
                  pl.BlockSpec((seq, kvw), per_b),
                  pl.BlockSpec((nkt, kvr, tk), lambda b, i: (b, 0, 0)),
                  pl.BlockSpec((tq, IDX_HEADS * IDX_HD), cur),
                  pl.BlockSpec((seq, LANES), per_b),
                  pl.BlockSpec((tq, LANES), cur),
                  pl.BlockSpec((LANES, kvw), fixed), pl.BlockSpec((kvr, LANES), fixed),
                  pl.BlockSpec((LANES, LANES), fixed)],
        out_specs=pl.BlockSpec((tq, DS_WIDTH), cur),
        out_shape=jax.ShapeDtypeStruct((m, DS_WIDTH), _MXU),
        scratch_shapes=[pltpu.VMEM((LANES, tq), jnp.int32), pltpu.VMEM((nkt, tk, tq), jnp.int32),
                        pltpu.VMEM((LANES, tq), _F32), pltpu.VMEM((nkt, tk, tq), _F32),
                        pltpu.VMEM((IDX_HEADS * tq, LANES), _MXU),
                        pltpu.VMEM((LANES, tq), _F32),
                        pltpu.VMEM((DS_KV, DS_REP * tq, DS_HD), _MXU),
                        pltpu.VMEM((DS_KV, 1, DS_REP * tq), _F32),
                        pltpu.VMEM((DS_KV, DS_HD + ONES_ROWS, DS_REP * tq), _F32)],
        compiler_params=_cparams(("parallel", "arbitrary")),
        name="dsa_frames",
    )(q, k, vt, qi, kid, wi, km_blk, vmt_blk, kidm_blk)


def _dsa_small_kernel(*refs, has_cache, nq, n_new, topk):
    if has_cache:
        q_ref, kn_ref, vn_ref, qi_ref, kidn_ref, wi_ref, ck_ref, cv_ref, ckid_ref, o_ref = refs
    else:
        q_ref, kn_ref, vn_ref, qi_ref, kidn_ref, wi_ref, o_ref = refs
    qst = _stack_idx_heads(qi_ref[...])
    wi = wi_ref[...]
    wbs = [jnp.broadcast_to(wi[:, h:h + 1], (nq, LANES)) for h in range(IDX_HEADS)]

    def wb_tile(width):
        return lambda h: jnp.concatenate([wbs[h]] * (width // LANES), axis=1)

    col_n = lax.broadcasted_iota(jnp.int32, (nq, LANES), 1)
    keys = [_order_key(_index_scores(qst, wb_tile(LANES), kidn_ref[...], nq), col_n < n_new)]
    if has_cache:
        past = ckid_ref.shape[1]
        sc = _index_scores(qst, wb_tile(past), ckid_ref[0], nq)
        keys.append(_order_key(sc, jnp.full(sc.shape, True)))

    def count_ge(thr):
        cnt = 0.0
        for kk in keys:
            cnt = cnt + jnp.sum(jnp.where(kk >= thr, 1.0, 0.0), axis=-1, keepdims=True)
        return cnt

    def bit_body(it, u):
        bit = jnp.int32(31) - it
        cand = u | lax.shift_left(jnp.int32(1), bit)
        cnt = count_ge(cand ^ jnp.int32(INT_MIN))
        return jnp.where(cnt >= float(topk), cand, u)

    u = lax.fori_loop(0, 32, bit_body, jnp.zeros((nq, 1), jnp.int32))
    thr = jnp.maximum(u ^ jnp.int32(INT_MIN), jnp.int32(INT_MIN + 1))
    biases = [jnp.where(kk >= thr, 0.0, NEG_INF) for kk in keys]

    for g in range(DS_KV):
        gl = slice(g * DS_HD, (g + 1) * DS_HD)
        qg = jnp.concatenate(
            [q_ref[:, (g * DS_REP + r) * DS_HD:(g * DS_REP + r + 1) * DS_HD] for r in range(DS_REP)], axis=0)
        ks, vs = [kn_ref[:, gl]], [vn_ref[:, gl]]
        if has_cache:
            ks.append(ck_ref[0][:, gl].astype(_MXU))
            vs.append(cv_ref[0][:, gl].astype(_MXU))
        ss = [_dot_nt(qg, kk) + jnp.concatenate([bb] * DS_REP, axis=0) for kk, bb in zip(ks, biases)]
        m = ss[0].max(axis=-1, keepdims=True)
        for s in ss[1:]:
            m = jnp.maximum(m, s.max(axis=-1, keepdims=True))
        l = 0.0
        acc = 0.0
        for s, vv in zip(ss, vs):
            p = jnp.exp2(s - m)
            l = l + jnp.sum(p, axis=-1, keepdims=True)
            acc = acc + _dot(p.astype(_MXU), vv)
        og = acc / l
        for r in range(DS_REP):
            lo = (g * DS_REP + r) * DS_HD
            o_ref[:, lo:lo + DS_HD] = og[r * nq:(r + 1) * nq].astype(o_ref.dtype)


def _dsa_small(q, kn_blk, vn_blk, qi, kidn_blk, wi, cache_k, cache_v, cache_kid, nb, nq, n_new, topk):
    kvw = DS_KV * DS_HD
    has_cache = cache_k is not None
    row = lambda b: (b, 0)
    in_specs = [pl.BlockSpec((nq, DS_WIDTH), row),
                pl.BlockSpec((LANES, kvw), row), pl.BlockSpec((LANES, kvw), row),
                pl.BlockSpec((nq, IDX_HEADS * IDX_HD), row),
                pl.BlockSpec((LANES, LANES), row),
                pl.BlockSpec((nq, LANES), row)]
    args = [q, kn_blk, vn_blk, qi, kidn_blk, wi]
    if has_cache:
        past = cache_k.shape[1]
        in_specs += [pl.BlockSpec((1, past, kvw), lambda b: (b, 0, 0)),
                     pl.BlockSpec((1, past, kvw), lambda b: (b, 0, 0)),
                     pl.BlockSpec((1, past, LANES), lambda b: (b, 0, 0))]
        args += [cache_k, cache_v, cache_kid]
    return pl.pallas_call(
        functools.partial(_dsa_small_kernel, has_cache=has_cache, nq=nq, n_new=n_new, topk=topk),
        grid=(nb,),
        in_specs=in_specs,
        out_specs=pl.BlockSpec((nq, DS_WIDTH), row),
        out_shape=jax.ShapeDtypeStruct((nb * nq, DS_WIDTH), _MXU),
        compiler_params=_cparams(("parallel",)),
        name="dsa_small",
    )(*args)


def _pad_rows_front(a, rows):
    return jnp.pad(a, ((rows - a.shape[0], 0), (0, 0)))


def _pad_rows_back(a, nb, nq, rows):
    w = a.shape[1]
    return jnp.pad(a.reshape(nb, nq, w), ((0, 0), (0, rows - nq), (0, 0))).reshape(nb * rows, w)


def _transpose_with_ones(v, group_cols):
    n = v.shape[0]
    g = v.shape[1] // group_cols
    vt = v.T.reshape(g, group_cols, n)
    pad = jnp.zeros((g, ONES_ROWS, n), v.dtype).at[:, 0, :].set(1)
    return jnp.concatenate([vt, pad], axis=1).reshape(g * (group_cols + ONES_ROWS), n)


def _dup_pairs(a, hd):
    lead = a.shape[:-1]
    g = a.shape[-1] // hd
    a = a.reshape(*lead, g, 1, hd)
    return jnp.broadcast_to(a, (*lead, g, 2, hd)).reshape(*lead, g * 2 * hd)


def kernel(x_prompt, x_sample, cache_l0_k, cache_l0_v, cache_l1_k, cache_l1_v, cache_l2_k, cache_l2_v, cache_l2_kidx, cache_l3_k, cache_l3_v, meta_tokens, l0_norm, l0_w_in, l0_w_out, l0_lam_q1, l0_lam_k1, l0_lam_q2, l0_lam_k2, l0_subln, l1_norm, l1_w_in, l1_w_out, l1_sinks, l2_norm, l2_w_in, l2_w_out, l3_norm, l3_w_in, l3_w_out, l3_lam_q1, l3_lam_k1, l3_lam_q2, l3_lam_k2, l3_subln, final_norm):
    nb, seq, d = x_prompt.shape
    db, ds, _ = x_sample.shape
    past = cache_l0_k.shape[1]
    sw_rows = cache_l1_k.shape[1]
    assert seq % CHUNK == 0 and meta_tokens.shape[0] == N_META and ds <= LANES

    xs = {"f": x_prompt.reshape(nb * seq, d), "m": meta_tokens.astype(_F32), "s": x_sample.reshape(db * ds, d)}
    pos = {"f": N_META + jnp.arange(seq), "m": jnp.arange(N_META), "s": past + jnp.arange(ds)}
    pos["s"] = jnp.tile(pos["s"], db)

    tabs128 = {n: _rope_tables(p, 128, 1.0) for n, p in pos.items()}
    tabs128_da = {n: _rope_tables(p, 128, DA_HD ** -0.5 * LOG2E) for n, p in pos.items()}
    tabs128_ds = {n: _rope_tables(p, 128, DS_HD ** -0.5 * LOG2E) for n, p in pos.items()}
    tabs64 = {n: _rope_tables(p, 64, 1.0) for n, p in pos.items()}
    tabs64_sw = {n: _rope_tables(p, 64, SW_HD ** -0.5 * LOG2E) for n, p in pos.items()}

    layers = [
        dict(norm=l0_norm, w_in=l0_w_in, w_out=l0_w_out, lam=(l0_lam_q1, l0_lam_k1, l0_lam_q2, l0_lam_k2),
             subln=l0_subln, cache=(cache_l0_k, cache_l0_v)),
        dict(norm=l1_norm, w_in=l1_w_in, w_out=l1_w_out, sinks=l1_sinks, cache=(cache_l1_k, cache_l1_v)),
        dict(norm=l2_norm, w_in=l2_w_in, w_out=l2_w_out, cache=(cache_l2_k, cache_l2_v, cache_l2_kidx)),
        dict(norm=l3_norm, w_in=l3_w_in, w_out=l3_w_out, lam=(l3_lam_q1, l3_lam_k1, l3_lam_q2, l3_lam_k2),
             subln=l3_subln, cache=(cache_l3_k, cache_l3_v)),
    ]
    norms = [lp["norm"] for lp in layers[1:]] + [final_norm]
    depth = len(layers)

    hs = {n: _rms_norm_rows(x, layers[0]["norm"]) for n, x in xs.items()}
    p_st, s_st = [], []
    ys = {}
    for li, lp in enumerate(layers):
        kind = li % N_MIXERS
        w_in = lp["w_in"].astype(_MXU)
        w_out = lp["w_out"].astype(_MXU)
        o, z = {}, {}
        if kind == 0:
            lam_init = 0.8 - 0.6 * math.exp(-0.3 * li)
            wq, wk, wv, wz = (w_in[:, i * DA_WIDTH:(i + 1) * DA_WIDTH] for i in range(4))
            lams = jnp.stack([l.astype(_F32) for l in lp["lam"]])
            subln = lp["subln"].reshape(1, 2 * DA_HD).astype(_F32)
            subln_col = jnp.broadcast_to(lp["subln"].astype(_F32)[:, None], (2 * DA_HD, LANES))
            hshape = (DA_HEADS, 2 * DA_HD)
            q, k32, k16, v32, v16 = {}, {}, {}, {}, {}
            for n in xs:
                (q[n],) = _project(hs[n], wq, [_MXU], *tabs128_da[n])
                k32[n], k16[n] = _project(hs[n], wk, [_F32, _MXU], *tabs128[n])
                if n == "f":
                    v32[n], vt_f = _project(hs[n], wv, [_F32], group_cols=2 * DA_HD)
                else:
                    v32[n], v16[n] = _project(hs[n], wv, [_F32, _MXU])
                (z[n],) = _project(hs[n], wz, [_MXU])
            o["f"] = _diff_frames(q["f"], k16["f"], vt_f, k16["m"], _transpose_with_ones(v16["m"], 2 * DA_HD),
                                  lams, subln_col, lam_init, nb, seq)
            o["m"] = _diff_small(q["m"], k16["m"], v16["m"], None, None, lams, subln, lam_init, 1, N_META)
            ck, cv = lp["cache"]
            o["s"] = _diff_small(q["s"], k16["s"], v16["s"], ck.reshape(db, past, DA_WIDTH),
                                 cv.reshape(db, past, DA_WIDTH), lams, subln, lam_init, db, ds)

            def with_meta(f, mrows):
                mm = jnp.broadcast_to(mrows.reshape(1, N_META, *hshape), (nb, N_META, *hshape))
                return jnp.concatenate([mm, f.reshape(nb, seq, *hshape)], axis=1)

            p_st.append([with_meta(k32["f"], k32["m"]), with_meta(v32["f"], v32["m"])])
            s_st.append([k32["s"].reshape(db, ds, *hshape), v32["s"].reshape(db, ds, *hshape)])
        elif kind == 1:
            kvd = SW_KV * SW_HD
            wq = w_in[:, :SW_WIDTH]
            wk = w_in[:, SW_WIDTH:SW_WIDTH + kvd]
            wv = w_in[:, SW_WIDTH + kvd:SW_WIDTH + 2 * kvd]
            wz = w_in[:, SW_WIDTH + 2 * kvd:]
            wkd, wvd = _dup_pairs(wk, SW_HD), _dup_pairs(wv, SW_HD)
            sink_rows = {}
            q, kv32, kd, vd = {}, {}, {}, {}
            for n, rows in (("f", CHUNK), ("m", N_META), ("s", ds)):
                sk = (lp["sinks"].astype(_F32) * LOG2E).reshape(SW_KV, SW_REP, 1, 1)
                sink_rows[n] = jnp.broadcast_to(sk, (SW_KV, SW_REP, rows, LANES)).reshape(SW_KV, SW_REP * rows, LANES)
                (q[n],) = _project(hs[n], wq, [_MXU], *tabs64_sw[n])
                (kd[n],) = _project(hs[n], wkd, [_MXU], *tabs64[n])
                (vd[n],) = _project(hs[n], wvd, [_MXU])
                (z[n],) = _project(hs[n], wz, [_MXU])
            kmeta_blk = _pad_rows_front(kd["m"], WINDOW)
            vmeta_blk = _pad_rows_front(vd["m"], WINDOW)
            o["f"] = _swa_frames(q["f"], kd["f"], vd["f"], kmeta_blk, vmeta_blk, sink_rows["f"], nb, seq)
            o["m"] = _swa_small(q["m"], kd["m"], vd["m"], None, None, sink_rows["m"], 1, N_META)
            ck, cv = lp["cache"]
            ckd = _dup_pairs(ck.reshape(db, sw_rows, kvd), SW_HD).astype(_MXU)
            cvd = _dup_pairs(cv.reshape(db, sw_rows, kvd), SW_HD).astype(_MXU)
            o["s"] = _swa_small(q["s"], kd["s"], vd["s"], ckd, cvd, sink_rows["s"], db, ds)
            assert seq >= sw_rows and sw_rows % 8 == 0
            h_tail = hs["f"].reshape(nb, seq, d)[:, seq - sw_rows:].reshape(nb * sw_rows, d)
            tail_pos = N_META + seq - sw_rows + jnp.arange(sw_rows)
            tail_tabs, tail_shift = _rope_tables(tail_pos, 64, 1.0)
            (kt,) = _project(h_tail, _pad_cols(wk), [_F32], tail_tabs, tail_shift)
            (vt,) = _project(h_tail, _pad_cols(wv), [_F32])
            p_st.append([kt[:, :kvd].reshape(nb, sw_rows, SW_KV, SW_HD), vt[:, :kvd].reshape(nb, sw_rows, SW_KV, SW_HD)])
            (ks,) = _project(hs["s"], _pad_cols(wk), [_F32], *tabs64["s"])
            (vs_,) = _project(hs["s"], _pad_cols(wv), [_F32])
            ks = ks[:, :kvd].reshape(db, ds, SW_KV, SW_HD)
            vs_ = vs_[:, :kvd].reshape(db, ds, SW_KV, SW_HD)
            k_all = jnp.concatenate([ck, ks], axis=1)
            v_all = jnp.concatenate([cv, vs_], axis=1)
            s_st.append([k_all[:, k_all.shape[1] - sw_rows:], v_all[:, v_all.shape[1] - sw_rows:]])
        else:
            kvd = DS_KV * DS_HD
            c0 = 0
            wq = w_in[:, c0:c0 + DS_WIDTH]; c0 += DS_WIDTH
            wk = w_in[:, c0:c0 + kvd]; c0 += kvd
            wv = w_in[:, c0:c0 + kvd]; c0 += kvd
            wz = w_in[:, c0:c0 + DS_WIDTH]; c0 += DS_WIDTH
            wqi = w_in[:, c0:c0 + IDX_HEADS * IDX_HD]; c0 += IDX_HEADS * IDX_HD
            wki = w_in[:, c0:c0 + IDX_HD]; c0 += IDX_HD
            wwi = w_in[:, c0:c0 + IDX_HEADS]
            wkid = jnp.concatenate([wki, wki], axis=1)
            wwi_p = _pad_cols(wwi)
            wi_scale = IDX_HEADS ** -0.5 * IDX_HD ** -0.5
            q, k32, k16, v32, v16, qi, kid32, kid16, wi = {}, {}, {}, {}, {}, {}, {}, {}, {}
            for n in xs:
                (q[n],) = _project(hs[n], wq, [_MXU], *tabs128_ds[n])
                k32[n], k16[n] = _project(hs[n], wk, [_F32, _MXU], *tabs128[n])
                if n == "f":
                    v32[n], vt_f = _project(hs[n], wv, [_F32], group_cols=DS_HD)
                else:
                    v32[n], v16[n] = _project(hs[n], wv, [_F32, _MXU])
                (z[n],) = _project(hs[n], wz, [_MXU])
                (qi[n],) = _project(hs[n], wqi, [_MXU], *tabs64[n])
                kid32[n], kid16[n] = _project(hs[n], wkid, [_F32, _MXU], *tabs64[n])
                (wi[n],) = _project(hs[n], wwi_p, [_F32], scale=wi_scale)
            topk_p = min(TOPK_MAX, seq // 4)
            topk_s = min(TOPK_MAX, (past + ds) // 4)
            km_blk = _pad_rows_back(k16["m"], 1, N_META, LANES)
            vm_blk = _pad_rows_back(v16["m"], 1, N_META, LANES)
            kidm_blk = _pad_rows_back(kid16["m"], 1, N_META, LANES)
            o["f"] = _dsa_frames(q["f"], k16["f"], vt_f, qi["f"], kid16["f"], wi["f"],
                                 km_blk, _transpose_with_ones(vm_blk, DS_HD), kidm_blk, nb, seq, topk_p)
            o["m"] = _dsa_small(q["m"], km_blk, vm_blk, qi["m"], kidm_blk, wi["m"], None, None, None,
                                1, N_META, N_META, topk_p)
            ck, cv, cki = lp["cache"]
            ckid = jnp.concatenate([cki, cki], axis=-1).astype(_MXU)
            o["s"] = _dsa_small(q["s"], _pad_rows_back(k16["s"], db, ds, LANES), _pad_rows_back(v16["s"], db, ds, LANES),
                                qi["s"], _pad_rows_back(kid16["s"], db, ds, LANES), wi["s"],
                                ck.reshape(db, past, kvd), cv.reshape(db, past, kvd), ckid,
                                db, ds, ds, topk_s)

            def with_meta(f, mrows, *tail):
                mm = jnp.broadcast_to(mrows.reshape(1, N_META, *tail), (nb, N_META, *tail))
                return jnp.concatenate([mm, f.reshape(nb, seq, *tail)], axis=1)

            p_st.append([with_meta(k32["f"], k32["m"], DS_KV, DS_HD), with_meta(v32["f"], v32["m"], DS_KV, DS_HD),
                         with_meta(kid32["f"][:, :IDX_HD], kid32["m"][:, :IDX_HD], IDX_HD)])
            s_st.append([k32["s"].reshape(db, ds, DS_KV, DS_HD), v32["s"].reshape(db, ds, DS_KV, DS_HD),
                         kid32["s"][:, :IDX_HD].reshape(db, ds, IDX_HD)])
        last = li == depth - 1
        for n in xs:
            res = _out_project(o[n], z[n], w_out, xs[n], norms[li], last)
            if last:
                (ys[n],) = res
            else:
                xs[n], hs[n] = res

    y_prompt = ys["f"].reshape(nb, seq, d)
    y_sample = ys["s"].reshape(db, ds, d)
    return (y_prompt, y_sample,
            p_st[0][0], p_st[0][1], s_st[0][0], s_st[0][1],
            p_st[1][0], p_st[1][1], s_st[1][0], s_st[1][1],
            p_st[2][0], p_st[2][1], p_st[2][2], s_st[2][0], s_st[2][1], s_st[2][2],
            p_st[3][0], p_st[3][1], s_st[3][0], s_st[3][1])


def _pad_cols(w):
    n = w.shape[1]
    return jnp.pad(w, ((0, 0), (0, (-n) % LANES)))
```

```python
import functools
import math

import jax
import jax.numpy as jnp
from jax import lax
from jax.experimental import pallas as pl
from jax.experimental.pallas import tpu as pltpu

CHUNK = 64
N_META = 16
ROPE_THETA = 500000.0
NORM_EPS = 1e-6
NEG_INF = -1e30
DA_HEADS = 8
DA_HD = 128
DA_WIDTH = DA_HEADS * 2 * DA_HD
SW_HEADS = 32
SW_KV = 4
SW_HD = 64
SW_WIDTH = SW_HEADS * SW_HD
SW_REP = SW_HEADS // SW_KV
WINDOW = 128
DS_HEADS = 16
DS_KV = 4
DS_HD = 128
DS_WIDTH = DS_HEADS * DS_HD
DS_REP = DS_HEADS // DS_KV
IDX_HEADS = 16
IDX_HD = 64
TOPK_MAX = 256
N_MIXERS = 3

LANES = 128
INT_MIN = -(2 ** 31)
LOG2E = math.log2(math.e)
ONES_ROWS = 16
VMEM_LIMIT = 56 * 1024 * 1024

_MXU = jnp.bfloat16
_F32 = jnp.float32

_NT = (((1,), (1,)), ((), ()))


def _cparams(sem):
    return pltpu.CompilerParams(dimension_semantics=sem, vmem_limit_bytes=VMEM_LIMIT)


def _row_tile(m, pref):
    t = pref
    while t > 8 and m % t:
        t //= 2
    assert m % t == 0, (m, pref)
    return t


def _dot(a, b):
    return jnp.dot(a, b, preferred_element_type=_F32)


def _dot_nt(a, b):
    return lax.dot_general(a, b, _NT, preferred_element_type=_F32)


def _norm_kernel(x_ref, g_ref, h_ref):
    x = x_ref[...]
    y = x * lax.rsqrt(jnp.mean(x * x, axis=-1, keepdims=True) + NORM_EPS)
    h_ref[...] = (y * g_ref[...]).astype(h_ref.dtype)


def _rms_norm_rows(x, g):
    m, d = x.shape
    out_dtype = _MXU
    tm = _row_tile(m, 512)
    return pl.pallas_call(
        _norm_kernel,
        grid=(m // tm,),
        in_specs=[pl.BlockSpec((tm, d), lambda i: (i, 0)),
                  pl.BlockSpec((1, d), lambda i: (0, 0))],
        out_specs=pl.BlockSpec((tm, d), lambda i: (i, 0)),
        out_shape=jax.ShapeDtypeStruct((m, d), out_dtype),
        compiler_params=_cparams(("parallel",)),
        name="rms_norm",
    )(x, g.reshape(1, d).astype(_F32))


def _proj_kernel(*refs, rope_shift, scale, n_out, tn, group_cols):
    h_ref, w_ref = refs[0], refs[1]
    pos = 2
    if rope_shift is not None:
        c_ref, a_ref, b_ref = refs[2:5]
        pos = 5
    outs = refs[pos:pos + n_out]
    ot_ref = refs[pos + n_out] if group_cols else None
    if ot_ref is not None:
        tm = ot_ref.shape[2]
        row = lax.broadcasted_iota(jnp.int32, (ONES_ROWS, tm), 0)
        ones_blk = jnp.where(row == 0, 1.0, 0.0).astype(ot_ref.dtype)
        for g in range(tn // group_cols):
            base = g * (group_cols + ONES_ROWS) + group_cols
            ot_ref[0, base:base + ONES_ROWS, :] = ones_blk
    h = h_ref[...]
    cw = 2 * LANES if tn % (2 * LANES) == 0 else LANES
    for c in range(tn // cw):
        wide = _dot(h, w_ref[:, c * cw:(c + 1) * cw])
        for sub in range(cw // LANES):
            acc = wide[:, sub * LANES:(sub + 1) * LANES]
            if rope_shift is not None:
                acc = (acc * c_ref[...]
                       + pltpu.roll(acc, rope_shift, 1) * a_ref[...]
                       + pltpu.roll(acc, LANES - rope_shift, 1) * b_ref[...])
            elif scale is not None:
                acc = acc * scale
            cols = slice(c * cw + sub * LANES, c * cw + (sub + 1) * LANES)
            for o in outs:
                o[:, cols] = acc.astype(o.dtype)
            if ot_ref is not None:
                lo = cols.start
                r0 = (lo // group_cols) * (group_cols + ONES_ROWS) + lo % group_cols
                ot_ref[0, r0:r0 + LANES, :] = acc.T.astype(ot_ref.dtype)


def _project(h, w, out_dtypes, tabs=None, rope_shift=None, scale=None, group_cols=0):
    m, d = h.shape
    n = w.shape[1]
    tm = _row_tile(m if tabs is None else math.gcd(m, tabs[0].shape[0]), 512)
    tn = n if n <= 2048 else 2048
    assert n % tn == 0 and tn % LANES == 0
    in_specs = [pl.BlockSpec((tm, d), lambda j, i: (i, 0)),
                pl.BlockSpec((d, tn), lambda j, i: (0, j))]
    args = [h, w]
    if tabs is not None:
        trows = tabs[0].shape[0]
        assert trows % tm == 0
        nrep = trows // tm
        for t in tabs:
            in_specs.append(pl.BlockSpec((tm, LANES), lambda j, i: (i % nrep, 0)))
            args.append(t)
    else:
        rope_shift = None
    out_specs = [pl.BlockSpec((tm, tn), lambda j, i: (i, j)) for _ in out_dtypes]
    out_shape = [jax.ShapeDtypeStruct((m, n), dt) for dt in out_dtypes]
    if group_cols:
        assert tn % group_cols == 0 and group_cols % LANES == 0
        grow = group_cols + ONES_ROWS
        out_specs.append(pl.BlockSpec((1, tn // group_cols * grow, tm), lambda j, i: (i, j, 0)))
        out_shape.append(jax.ShapeDtypeStruct((m // tm, n // group_cols * grow, tm), _MXU))
    outs = pl.pallas_call(
        functools.partial(_proj_kernel, rope_shift=rope_shift, scale=scale,
                          n_out=len(out_dtypes), tn=tn, group_cols=group_cols),
        grid=(n // tn, m // tm),
        in_specs=in_specs,
        out_specs=out_specs,
        out_shape=out_shape,
        compiler_params=_cparams(("parallel", "parallel")),
        name="in_proj",
    )(*args)
    return outs


def _outproj_kernel(o_ref, z_ref, w_ref, x_ref, g_ref, *outs, emit_x):
    z = z_ref[...].astype(_F32)
    a = o_ref[...].astype(_F32) * (z / (1.0 + jnp.exp(-z)))
    y = x_ref[...] + _dot(a.astype(w_ref.dtype), w_ref[...])
    if emit_x:
        outs[0][...] = y
    n = y * lax.rsqrt(jnp.mean(y * y, axis=-1, keepdims=True) + NORM_EPS) * g_ref[...]
    outs[-1][...] = n.astype(outs[-1].dtype)


def _out_project(o, z, w, x, g_next, last):
    m, d = x.shape
    wd = o.shape[1]
    tm = _row_tile(m, 512)
    row = lambda i: (i, 0)
    fixed = lambda i: (0, 0)
    if last:
        out_shape = [jax.ShapeDtypeStruct((m, d), _F32)]
    else:
        out_shape = [jax.ShapeDtypeStruct((m, d), _F32), jax.ShapeDtypeStruct((m, d), _MXU)]
    return pl.pallas_call(
        functools.partial(_outproj_kernel, emit_x=not last),
        grid=(m // tm,),
        in_specs=[pl.BlockSpec((tm, wd), row), pl.BlockSpec((tm, wd), row),
                  pl.BlockSpec((wd, d), fixed), pl.BlockSpec((tm, d), row),
                  pl.BlockSpec((1, d), fixed)],
        out_specs=[pl.BlockSpec((tm, d), row) for _ in out_shape],
        out_shape=out_shape,
        compiler_params=_cparams(("parallel",)),
        name="out_proj",
    )(o, z, w, x, g_next.reshape(1, d).astype(_F32))


def _rope_tables(pos, hd, scale=1.0):
    rd = hd // 4
    half = rd // 2
    lane = jnp.arange(LANES)
    d = lane % hd
    f = d % half
    inv_freq = ROPE_THETA ** (-(jnp.arange(half, dtype=_F32) * 2.0 / rd))
    ang = pos.astype(_F32)[:, None] * inv_freq[f][None, :]
    cos, sin = jnp.cos(ang), jnp.sin(ang)
    in_rot = (d < rd)[None, :]
    c = jnp.where(in_rot, cos, 1.0)
    a = jnp.where(((d >= half) & (d < rd))[None, :], sin, 0.0)
    b = jnp.where((d < half)[None, :], -sin, 0.0)
    s = jnp.float32(scale)
    return (c * s, a * s, b * s), half


def _lam_value(lam_ref, lam_init):
    l = lam_ref[...]
    s1 = jnp.sum(l[0:1] * l[1:2], axis=-1, keepdims=True)
    s2 = jnp.sum(l[2:3] * l[3:4], axis=-1, keepdims=True)
    return jnp.exp(s1) - jnp.exp(s2) + lam_init


def _diff_finish(acc1, l1, acc2, l2, lam, sub, lam_init):
    o = acc1 / l1 - lam * (acc2 / l2)
    o = o * lax.rsqrt(jnp.mean(o * o, axis=-1, keepdims=True) + NORM_EPS) * sub
    return o * (1.0 - lam_init)


def _diff_frames_kernel(q_ref, k_ref, vt_ref, km_ref, vmt_ref, lam_ref, subc_ref, o_ref,
                        acc_s, sa_s, sb_s, pa_s, pb_s, *, tq, lam_init):
    i = pl.program_id(2)
    hw = 2 * DA_HD

    def q_half(j):
        return q_ref[:, j * DA_HD:(j + 1) * DA_HD]

    km = km_ref[...]
    vmt = vmt_ref[...]
    ms = []
    for j in range(2):
        s = _dot_nt(km[:, j * DA_HD:(j + 1) * DA_HD], q_half(j))
        m = jnp.max(s, axis=0, keepdims=True)
        ms.append(m)
        acc_s[j] = _dot(vmt, jnp.exp2(s - m).astype(vmt.dtype))

    def scores_into(dst, t):
        start = pl.multiple_of(t * tq, tq)
        for j in range(2):
            dst[j] = _dot_nt(k_ref[pl.ds(start, tq), j * DA_HD:(j + 1) * DA_HD], q_half(j))

    def value_update(t, a, p_rd):
        vt = vt_ref[t]
        for j in range(2):
            acc_s[j] = a[j] * acc_s[j] + _dot(vt, p_rd[j])

    def tile_step(t, carry, src, dst, p_rd, p_wr, masked, has_next):
        m, a = carry
        if masked:
            kc = lax.broadcasted_iota(jnp.int32, (tq, tq), 0) // CHUNK
            qc = lax.broadcasted_iota(jnp.int32, (tq, tq), 1) // CHUNK
            ok = kc <= qc
        m_out, a_out = [], []
        for j in range(2):
            s = src[j]
            if masked:
                s = jnp.where(ok, s, NEG_INF)
            m_new = jnp.maximum(m[j], jnp.max(s, axis=0, keepdims=True))
            a_out.append(jnp.exp2(m[j] - m_new))
            m_out.append(m_new)
            p_wr[j] = jnp.exp2(s - m_new).astype(p_wr.dtype)
        value_update(jnp.maximum(t - 1, 0), a, p_rd)
        if has_next:
            scores_into(dst, t + 1)
        return tuple(m_out), tuple(a_out)

    scores_into(sa_s, 0)
    pb_s[...] = jnp.zeros_like(pb_s)
    one = jnp.ones((1, tq), _F32)
    carry = ((ms[0], ms[1]), (one, one))

    def pair_body(u, carry):
        carry = tile_step(2 * u, carry, sa_s, sb_s, pb_s, pa_s, False, True)
        return tile_step(2 * u + 1, carry, sb_s, sa_s, pa_s, pb_s, False, True)

    carry = lax.fori_loop(0, i // 2, pair_body, carry)

    @pl.when(i % 2 == 0)
    def _():
        _, a = tile_step(i, carry, sa_s, sb_s, pb_s, pa_s, True, False)
        value_update(i, a, pa_s)

    @pl.when(i % 2 == 1)
    def _():
        c = tile_step(i - 1, carry, sa_s, sb_s, pb_s, pa_s, False, True)
        _, a = tile_step(i, c, sb_s, sa_s, pa_s, pb_s, True, False)
        value_update(i, a, pb_s)

    acc0 = acc_s[0]
    acc1 = acc_s[1]
    lam = _lam_value(lam_ref, lam_init)
    inv0 = 1.0 / acc0[hw:hw + 1]
    inv1 = lam / acc1[hw:hw + 1]
    ot = acc0[:hw] * inv0 - acc1[:hw] * inv1
    sub = jnp.concatenate([subc_ref[...]] * (tq // LANES), axis=1)
    ot = ot * lax.rsqrt(jnp.mean(ot * ot, axis=0, keepdims=True) + NORM_EPS) * sub * (1.0 - lam_init)
    o_ref[...] = ot.T.astype(o_ref.dtype)


def _diff_frames(q, k, vt, km, vmt, lams, subln_col, lam_init, nb, seq):
    m = q.shape[0]
    hw = 2 * DA_HD
    hr = hw + ONES_ROWS
    tq = vt.shape[2]
    assert seq % tq == 0 and tq % LANES == 0
    nt = seq // tq
    return pl.pallas_call(
        functools.partial(_diff_frames_kernel, tq=tq, lam_init=lam_init),
        grid=(nb, DA_HEADS, nt),
        in_specs=[pl.BlockSpec((tq, hw), lambda b, h, i: (b * nt + i, h)),
                  pl.BlockSpec((seq, hw), lambda b, h, i: (b, h)),
                  pl.BlockSpec((nt, hr, tq), lambda b, h, i: (b, h, 0)),
                  pl.BlockSpec((N_META, hw), lambda b, h, i: (0, h)),
                  pl.BlockSpec((hr, N_META), lambda b, h, i: (h, 0)),
                  pl.BlockSpec((4, DA_HD), lambda b, h, i: (0, 0)),
                  pl.BlockSpec((hw, LANES), lambda b, h, i: (0, 0))],
        out_specs=pl.BlockSpec((tq, hw), lambda b, h, i: (b * nt + i, h)),
        out_shape=jax.ShapeDtypeStruct((m, DA_WIDTH), _MXU),
        scratch_shapes=[pltpu.VMEM((2, hr, tq), _F32),
                        pltpu.VMEM((2, tq, tq), _F32), pltpu.VMEM((2, tq, tq), _F32),
                        pltpu.VMEM((2, tq, tq), _MXU), pltpu.VMEM((2, tq, tq), _MXU)],
        compiler_params=_cparams(("parallel", "parallel", "arbitrary")),
        name="diff_frames",
    )(q, k, vt, km, vmt, lams, subln_col)


def _diff_small_kernel(*refs, has_cache, lam_init):
    if has_cache:
        q_ref, k_ref, v_ref, ck_ref, cv_ref, lam_ref, sub_ref, o_ref = refs
    else:
        q_ref, k_ref, v_ref, lam_ref, sub_ref, o_ref = refs
    q = q_ref[...]
    kn = k_ref[...]
    vn = v_ref[...]
    parts = [(kn, vn)]
    if has_cache:
        parts.append((ck_ref[0].astype(kn.dtype), cv_ref[0].astype(vn.dtype)))
    accs, ls = [], []
    for j in range(2):
        qj = q[:, j * DA_HD:(j + 1) * DA_HD]
        ss = [_dot_nt(qj, kp[:, j * DA_HD:(j + 1) * DA_HD]) for kp, _ in parts]
        m = ss[0].max(axis=-1, keepdims=True)
        for s in ss[1:]:
            m = jnp.maximum(m, s.max(axis=-1, keepdims=True))
        l = 0.0
        acc = 0.0
        for s, (_, vp) in zip(ss, parts):
            p = jnp.exp2(s - m)
            l = l + jnp.sum(p, axis=-1, keepdims=True)
            acc = acc + _dot(p.astype(vp.dtype), vp)
        accs.append(acc)
        ls.append(l)
    lam = _lam_value(lam_ref, lam_init)
    o = _diff_finish(accs[0], ls[0], accs[1], ls[1], lam, sub_ref[...], lam_init)
    o_ref[...] = o.astype(o_ref.dtype)


def _diff_small(q, k, v, cache_k, cache_v, lams, subln, lam_init, nb, nq):
    hw = 2 * DA_HD
    has_cache = cache_k is not None
    qspec = pl.BlockSpec((nq, hw), lambda b, h: (b, h))
    in_specs = [qspec, qspec, qspec]
    args = [q, k, v]
    if has_cache:
        past = cache_k.shape[1]
        cspec = pl.BlockSpec((1, past, hw), lambda b, h: (b, 0, h))
        in_specs += [cspec, cspec]
        args += [cache_k, cache_v]
    in_specs += [pl.BlockSpec((4, DA_HD), lambda b, h: (0, 0)),
                 pl.BlockSpec((1, hw), lambda b, h: (0, 0))]
    args += [lams, subln]
    return pl.pallas_call(
        functools.partial(_diff_small_kernel, has_cache=has_cache, lam_init=lam_init),
        grid=(nb, DA_HEADS),
        in_specs=in_specs,
        out_specs=qspec,
        out_shape=jax.ShapeDtypeStruct((nb * nq, DA_WIDTH), _MXU),
        compiler_params=_cparams(("parallel", "parallel")),
        name="diff_small",
    )(*args)


def _stack_heads(qg, nheads):
    n = qg.shape[0]
    lane = lax.broadcasted_iota(jnp.int32, (n, LANES), 1)
    lo = lane < SW_HD
    pieces = []
    for r in range(nheads):
        blk = qg[:, (r // 2) * LANES:(r // 2 + 1) * LANES]
        keep = lo if r % 2 == 0 else jnp.logical_not(lo)
        pieces.append(jnp.where(keep, blk, jnp.zeros_like(blk)))
    return jnp.concatenate(pieces, axis=0)


def _unstack_heads(o, n, nheads):
    lane = lax.broadcasted_iota(jnp.int32, (n, LANES), 1)
    lo = lane < SW_HD
    return [jnp.where(lo, o[(2 * p) * n:(2 * p + 1) * n], o[(2 * p + 1) * n:(2 * p + 2) * n])
            for p in range(nheads // 2)]


def _sink_softmax_pv(ss, vs, sk):
    m = sk
    for s in ss:
        m = jnp.maximum(m, jnp.max(s, axis=-1, keepdims=True))
    den = jnp.exp2(sk - m)
    acc = 0.0
    for s, v in zip(ss, vs):
        e = jnp.exp2(s - m)
        den = den + jnp.sum(e, axis=-1, keepdims=True)
        acc = acc + _dot(e.astype(v.dtype), v)
    return acc / den


def _swa_frames_kernel(q_ref, kc_ref, vc_ref, kp_ref, vp_ref, km_ref, vm_ref, sk_ref, o_ref,
                       kcat, vcat, *, tq):
    i = pl.program_id(1)
    first = i == 0
    prev_rows = WINDOW
    kcat[0:prev_rows] = jnp.where(first, km_ref[...], kp_ref[...])
    vcat[0:prev_rows] = jnp.where(first, vm_ref[...], vp_ref[...])
    kcat[prev_rows:prev_rows + tq] = kc_ref[...]
    vcat[prev_rows:prev_rows + tq] = vc_ref[...]
    nwin = WINDOW + CHUNK
    for c in range(tq // CHUNK):
        rows = slice(c * CHUNK, (c + 1) * CHUNK)
        win = slice(c * CHUNK, c * CHUNK + nwin)
        for g in range(SW_KV):
            gl = slice(g * LANES, (g + 1) * LANES)
            qst = _stack_heads(q_ref[rows, g * SW_REP * SW_HD:(g + 1) * SW_REP * SW_HD], SW_REP)
            s = _dot_nt(qst, kcat[win, gl])
            if c * CHUNK < prev_rows - N_META:
                col = lax.broadcasted_iota(jnp.int32, s.shape, 1) + c * CHUNK
                ok = jnp.logical_or(col >= prev_rows - N_META, jnp.logical_not(first))
                s = jnp.where(ok, s, NEG_INF)
            o = _sink_softmax_pv([s], [vcat[win, gl]], sk_ref[g][:, 0:1])
            for p, blk in enumerate(_unstack_heads(o, CHUNK, SW_REP)):
                lo = g * SW_REP * SW_HD + p * LANES
                o_ref[rows, lo:lo + LANES] = blk.astype(o_ref.dtype)


def _swa_frames(q, kd, vd, kmeta_blk, vmeta_blk, sink_rows, nb, seq):
    m = q.shape[0]
    tq = _row_tile(seq, 512)
    nt = seq // tq
    pw = SW_KV * LANES
    per = tq // WINDOW
    cur = lambda b, i: (b * nt + i, 0)
    prev = lambda b, i: (jnp.maximum((b * nt + i) * per - 1, 0), 0)
    fixed = lambda b, i: (0, 0)
    return pl.pallas_call(
        functools.partial(_swa_frames_kernel, tq=tq),
        grid=(nb, nt),
        in_specs=[pl.BlockSpec((tq, SW_WIDTH), cur),
                  pl.BlockSpec((tq, pw), cur), pl.BlockSpec((tq, pw), cur),
                  pl.BlockSpec((WINDOW, pw), prev), pl.BlockSpec((WINDOW, pw), prev),
                  pl.BlockSpec((WINDOW, pw), fixed), pl.BlockSpec((WINDOW, pw), fixed),
                  pl.BlockSpec((SW_KV, SW_REP * CHUNK, LANES), lambda b, i: (0, 0, 0))],
        out_specs=pl.BlockSpec((tq, SW_WIDTH), cur),
        out_shape=jax.ShapeDtypeStruct((m, SW_WIDTH), _MXU),
        scratch_shapes=[pltpu.VMEM((WINDOW + tq, pw), _MXU), pltpu.VMEM((WINDOW + tq, pw), _MXU)],
        compiler_params=_cparams(("parallel", "arbitrary")),
        name="swa_frames",
    )(q, kd, vd, kd, vd, kmeta_blk, vmeta_blk, sink_rows)


def _swa_small_kernel(*refs, has_cache, nq):
    if has_cache:
        q_ref, k_ref, v_ref, ck_ref, cv_ref, sk_ref, o_ref = refs
    else:
        q_ref, k_ref, v_ref, sk_ref, o_ref = refs
    for g in range(SW_KV):
        gl = slice(g * LANES, (g + 1) * LANES)
        qst = _stack_heads(q_ref[:, g * SW_REP * SW_HD:(g + 1) * SW_REP * SW_HD], SW_REP)
        ks, vs = [k_ref[:, gl]], [v_ref[:, gl]]
        if has_cache:
            ks.append(ck_ref[0][:, gl])
            vs.append(cv_ref[0][:, gl])
        ss = [_dot_nt(qst, kk) for kk in ks]
        o = _sink_softmax_pv(ss, vs, sk_ref[g][:, 0:1])
        for p, blk in enumerate(_unstack_heads(o, nq, SW_REP)):
            lo = g * SW_REP * SW_HD + p * LANES
            o_ref[:, lo:lo + LANES] = blk.astype(o_ref.dtype)


def _swa_small(q, kd, vd, cache_kd, cache_vd, sink_rows, nb, nq):
    pw = SW_KV * LANES
    has_cache = cache_kd is not None
    row = lambda b: (b, 0)
    in_specs = [pl.BlockSpec((nq, SW_WIDTH), row), pl.BlockSpec((nq, pw), row), pl.BlockSpec((nq, pw), row)]
    args = [q, kd, vd]
    if has_cache:
        past = cache_kd.shape[1]
        cspec = pl.BlockSpec((1, past, pw), lambda b: (b, 0, 0))
        in_specs += [cspec, cspec]
        args += [cache_kd, cache_vd]
    in_specs.append(pl.BlockSpec((SW_KV, SW_REP * nq, LANES), lambda b: (0, 0, 0)))
    args.append(sink_rows)
    return pl.pallas_call(
        functools.partial(_swa_small_kernel, has_cache=has_cache, nq=nq),
        grid=(nb,),
        in_specs=in_specs,
        out_specs=pl.BlockSpec((nq, SW_WIDTH), row),
        out_shape=jax.ShapeDtypeStruct((nb * nq, SW_WIDTH), _MXU),
        compiler_params=_cparams(("parallel",)),
        name="swa_small",
    )(*args)


def _order_key(score, valid):
    bits = lax.bitcast_convert_type(score + 0.0, jnp.int32)
    key = jnp.where(bits < 0, bits ^ jnp.int32(0x7FFFFFFF), bits)
    return jnp.where(valid, key, jnp.int32(INT_MIN))


def _stack_idx_heads(qi):
    return _stack_heads(qi, IDX_HEADS)


def _index_scores(qst, wb_of, kid, n):
    score = None
    hb = 4
    for h0 in range(0, IDX_HEADS, hb):
        sc = _dot_nt(qst[h0 * n:(h0 + hb) * n], kid)
        for h in range(h0, h0 + hb):
            part = jnp.maximum(sc[(h - h0) * n:(h - h0 + 1) * n], 0.0) * wb_of(h)
            score = part if score is None else score + part
    return score


def _index_scores_t(kid, qst, wit, n):
    score = None
    hb = 4
    for h0 in range(0, IDX_HEADS, hb):
        sc = _dot_nt(kid, qst[h0 * n:(h0 + hb) * n])
        for h in range(h0, h0 + hb):
            part = jnp.maximum(sc[:, (h - h0) * n:(h - h0 + 1) * n], 0.0) * wit[h:h + 1, :]
            score = part if score is None else score + part
    return score


def _dsa_frames_kernel(q_ref, k_ref, vt_ref, qi_ref, kid_ref, wi_ref, km_ref, vmt_ref, kidm_ref, o_ref,
                       keym_s, key_s, biasm_s, bias_s, qst_s, wit_s, qg_s, m_s, acc_s, *, tq, tk, topk):
    i = pl.program_id(1)
    n_t = (i * tq + tq + tk - 1) // tk
    qst_s[...] = _stack_idx_heads(qi_ref[...])
    wit_s[...] = wi_ref[...].T
    sc_m = _index_scores_t(kidm_ref[...], qst_s[...], wit_s[...], tq)
    row_m = lax.broadcasted_iota(jnp.int32, (LANES, tq), 0)
    keym_s[...] = _order_key(sc_m, row_m < N_META)

    q_chunk = (lax.broadcasted_iota(jnp.int32, (tk, tq), 1) + i * tq) // CHUNK

    def score_body(t, carry):
        start = pl.multiple_of(t * tk, tk)
        sc = _index_scores_t(kid_ref[pl.ds(start, tk), :], qst_s[...], wit_s[...], tq)
        k_chunk = (lax.broadcasted_iota(jnp.int32, (tk, tq), 0) + t * tk) // CHUNK
        key_s[t] = _order_key(sc, k_chunk <= q_chunk)
        return carry

    lax.fori_loop(0, n_t, score_body, 0)

    def count_ge(thr):
        def fold(x):
            return jnp.sum(x.reshape(x.shape[0] // 64, 64, tq), axis=0)

        part = fold(jnp.where(keym_s[...] >= thr, 1.0, 0.0))

        def cbody(t, acc):
            return acc + fold(jnp.where(key_s[t] >= thr, 1.0, 0.0))

        part = lax.fori_loop(0, n_t, cbody, part)
        return jnp.sum(part, axis=0, keepdims=True)

    def bit_body(it, carry):
        u, cnt_u = carry
        bit = jnp.int32(31) - it
        cand = u | lax.shift_left(jnp.int32(1), bit)
        cnt = count_ge(cand ^ jnp.int32(INT_MIN))
        ok = cnt >= float(topk)
        return jnp.where(ok, cand, u), jnp.where(ok, cnt, cnt_u)

    u, cnt_u = lax.fori_loop(0, 32, bit_body, (jnp.zeros((1, tq), jnp.int32), jnp.zeros((1, tq), _F32)))
    thr = jnp.maximum(u ^ jnp.int32(INT_MIN), jnp.int32(INT_MIN + 1))

    biasm_s[...] = jnp.where(keym_s[...] >= thr, 0.0, NEG_INF)

    def bias_body(t, carry):
        bias_s[t] = jnp.where(key_s[t] >= thr, 0.0, NEG_INF)
        return carry

    lax.fori_loop(0, n_t, bias_body, 0)

    @pl.when(jnp.max(cnt_u) > float(topk))
    def _():
        need = float(topk) - count_ge(thr + 1)

        def tie_bias(key, seen):
            rows = key.shape[0]
            eq = key == thr
            r = lax.broadcasted_iota(jnp.int32, (rows, rows), 0)
            c = lax.broadcasted_iota(jnp.int32, (rows, rows), 1)
            upto = _dot(jnp.where(c <= r, 1.0, 0.0).astype(_MXU), jnp.where(eq, 1.0, 0.0).astype(_MXU))
            keep = jnp.logical_or(key > thr, jnp.logical_and(eq, seen + upto <= need))
            return jnp.where(keep, 0.0, NEG_INF), seen + upto[rows - 1:rows]

        biasm_s[...], seen = tie_bias(keym_s[...], jnp.zeros((1, tq), _F32))

        def tie_body(t, seen):
            bias_s[t], seen = tie_bias(key_s[t], seen)
            return seen

        lax.fori_loop(0, n_t, tie_body, seen)

    vr = DS_HD + ONES_ROWS
    groups = [(g, slice(g * DS_HD, (g + 1) * DS_HD), slice(g * vr, (g + 1) * vr)) for g in range(DS_KV)]
    for g, gl, gv in groups:
        qg_s[g] = jnp.concatenate(
            [q_ref[:, (g * DS_REP + r) * DS_HD:(g * DS_REP + r + 1) * DS_HD] for r in range(DS_REP)], axis=0)
        s = _dot_nt(km_ref[:, gl], qg_s[g]) + jnp.concatenate([biasm_s[...]] * DS_REP, axis=1)
        m0 = jnp.max(s, axis=0, keepdims=True)
        m_s[g] = m0
        acc_s[g] = _dot(vmt_ref[gv, :], jnp.exp2(s - m0).astype(_MXU))

    def att_body(t, carry):
        start = pl.multiple_of(t * tk, tk)
        bias = jnp.concatenate([bias_s[t]] * DS_REP, axis=1)
        for pair in (groups[:2], groups[2:]):
            ss = [_dot_nt(k_ref[pl.ds(start, tk), gl], qg_s[g]) for g, gl, _ in pair]
            for (g, _, gv), s in zip(pair, ss):
                s = s + bias
                m_prev = m_s[g]
                m_new = jnp.maximum(m_prev, jnp.max(s, axis=0, keepdims=True))
                alpha = jnp.exp2(m_prev - m_new)
                p = jnp.exp2(s - m_new).astype(_MXU)
                acc_s[g] = alpha * acc_s[g] + _dot(vt_ref[t, gv, :], p)
                m_s[g] = m_new
        return carry

    lax.fori_loop(0, n_t, att_body, 0)
    for g in range(DS_KV):
        acc = acc_s[g]
        ot = acc[:DS_HD] * (1.0 / acc[DS_HD:DS_HD + 1])
        for r in range(DS_REP):
            lo = (g * DS_REP + r) * DS_HD
            o_ref[:, lo:lo + DS_HD] = ot[:, r * tq:(r + 1) * tq].T.astype(o_ref.dtype)


def _dsa_frames(q, k, vt, qi, kid, wi, km_blk, vmt_blk, kidm_blk, nb, seq, topk):
    m = q.shape[0]
    tq = _row_tile(seq, 128)
    tk = vt.shape[2]
    assert seq % tk == 0 and tq == LANES
    nt = seq // tq
    nkt = seq // tk
    kvw = DS_KV * DS_HD
    kvr = DS_KV * (DS_HD + ONES_ROWS)
    cur = lambda b, i: (b * nt + i, 0)
    per_b = lambda b, i: (b, 0)
    fixed = lambda b, i: (0, 0)
    return pl.pallas_call(
        functools.partial(_dsa_frames_kernel, tq=tq, tk=tk, topk=topk),
        grid=(nb, nt),
        in_specs=[pl.BlockSpec((tq, DS_WIDTH), cur),
                  pl.BlockSpec((seq, kvw), per_b),
                  pl.BlockSpec((nkt, kvr, tk), lambda b, i: (b, 0, 0)),
                  pl.BlockSpec((tq, IDX_HEADS * IDX_HD), cur),
                  pl.BlockSpec((seq, LANES), per_b),
                  pl.BlockSpec((tq, LANES), cur),
                  pl.BlockSpec((LANES, kvw), fixed), pl.BlockSpec((kvr, LANES), fixed),
                  pl.BlockSpec((LANES, LANES), fixed)],
        out_specs=pl.BlockSpec((tq, DS_WIDTH), cur),
        out_shape=jax.ShapeDtypeStruct((m, DS_WIDTH), _MXU),
        scratch_shapes=[pltpu.VMEM((LANES, tq), jnp.int32), pltpu.VMEM((nkt, tk, tq), jnp.int32),
                        pltpu.VMEM((LANES, tq), _F32), pltpu.VMEM((nkt, tk, tq), _F32),
                        pltpu.VMEM((IDX_HEADS * tq, LANES), _MXU),
                        pltpu.VMEM((LANES, tq), _F32),
                        pltpu.VMEM((DS_KV, DS_REP * tq, DS_HD), _MXU),
                        pltpu.VMEM((DS_KV, 1, DS_REP * tq), _F32),
                        pltpu.VMEM((DS_KV, DS_HD + ONES_ROWS, DS_REP * tq), _F32)],
        compiler_params=_cparams(("parallel", "arbitrary")),
        name="dsa_frames",
    )(q, k, vt, qi, kid, wi, km_blk, vmt_blk, kidm_blk)


def _dsa_small_kernel(*refs, has_cache, nq, n_new, topk):
    if has_cache:
        q_ref, kn_ref, vn_ref, qi_ref, kidn_ref, wi_ref, ck_ref, cv_ref, ckid_ref, o_ref = refs
    else:
        q_ref, kn_ref, vn_ref, qi_ref, kidn_ref, wi_ref, o_ref = refs
    qst = _stack_idx_heads(qi_ref[...])
    wi = wi_ref[...]
    wbs = [jnp.broadcast_to(wi[:, h:h + 1], (nq, LANES)) for h in range(IDX_HEADS)]

    def wb_tile(width):
        return lambda h: jnp.concatenate([wbs[h]] * (width // LANES), axis=1)

    col_n = lax.broadcasted_iota(jnp.int32, (nq, LANES), 1)
    keys = [_order_key(_index_scores(qst, wb_tile(LANES), kidn_ref[...], nq), col_n < n_new)]
    if has_cache:
        past = ckid_ref.shape[1]
        sc = _index_scores(qst, wb_tile(past), ckid_ref[0], nq)
        keys.append(_order_key(sc, jnp.full(sc.shape, True)))

    def count_ge(thr):
        cnt = 0.0
        for kk in keys:
            cnt = cnt + jnp.sum(jnp.where(kk >= thr, 1.0, 0.0), axis=-1, keepdims=True)
        return cnt

    def bit_body(it, u):
        bit = jnp.int32(31) - it
        cand = u | lax.shift_left(jnp.int32(1), bit)
        cnt = count_ge(cand ^ jnp.int32(INT_MIN))
        return jnp.where(cnt >= float(topk), cand, u)

    u = lax.fori_loop(0, 32, bit_body, jnp.zeros((nq, 1), jnp.int32))
    thr = jnp.maximum(u ^ jnp.int32(INT_MIN), jnp.int32(INT_MIN + 1))

    need = float(topk) - count_ge(thr + 1)
    r = lax.broadcasted_iota(jnp.int32, (LANES, LANES), 0)
    c = lax.broadcasted_iota(jnp.int32, (LANES, LANES), 1)
    before = jnp.where(r <= c, 1.0, 0.0).astype(_MXU)
    seen = jnp.zeros((nq, 1), _F32)
    biases = [None] * len(keys)
    for idx in reversed(range(len(keys))):
        kk = keys[idx]
        cols = []
        for c0 in range(0, kk.shape[1], LANES):
            blk = kk[:, c0:c0 + LANES]
            eq = blk == thr
            upto = _dot(jnp.where(eq, 1.0, 0.0).astype(_MXU), before)
            keep = jnp.logical_or(blk > thr, jnp.logical_and(eq, seen + upto <= need))
            cols.append(jnp.where(keep, 0.0, NEG_INF))
            seen = seen + upto[:, LANES - 1:LANES]
        biases[idx] = jnp.concatenate(cols, axis=1)

    for g in range(DS_KV):
        gl = slice(g * DS_HD, (g + 1) * DS_HD)
        qg = jnp.concatenate(
            [q_ref[:, (g * DS_REP + r) * DS_HD:(g * DS_REP + r + 1) * DS_HD] for r in range(DS_REP)], axis=0)
        ks, vs = [kn_ref[:, gl]], [vn_ref[:, gl]]
        if has_cache:
            ks.append(ck_ref[0][:, gl].astype(_MXU))
            vs.append(cv_ref[0][:, gl].astype(_MXU))
        ss = [_dot_nt(qg, kk) + jnp.concatenate([bb] * DS_REP, axis=0) for kk, bb in zip(ks, biases)]
        m = ss[0].max(axis=-1, keepdims=True)
        for s in ss[1:]:
            m = jnp.maximum(m, s.max(axis=-1, keepdims=True))
        l = 0.0
        acc = 0.0
        for s, vv in zip(ss, vs):
            p = jnp.exp2(s - m)
            l = l + jnp.sum(p, axis=-1, keepdims=True)
            acc = acc + _dot(p.astype(_MXU), vv)
        og = acc / l
        for r in range(DS_REP):
            lo = (g * DS_REP + r) * DS_HD
            o_ref[:, lo:lo + DS_HD] = og[r * nq:(r + 1) * nq].astype(o_ref.dtype)


def _dsa_small(q, kn_blk, vn_blk, qi, kidn_blk, wi, cache_k, cache_v, cache_kid, nb, nq, n_new, topk):
    kvw = DS_KV * DS_HD
    has_cache = cache_k is not None
    row = lambda b: (b, 0)
    in_specs = [pl.BlockSpec((nq, DS_WIDTH), row),
                pl.BlockSpec((LANES, kvw), row), pl.BlockSpec((LANES, kvw), row),
                pl.BlockSpec((nq, IDX_HEADS * IDX_HD), row),
                pl.BlockSpec((LANES, LANES), row),
                pl.BlockSpec((nq, LANES), row)]
    args = [q, kn_blk, vn_blk, qi, kidn_blk, wi]
    if has_cache:
        past = cache_k.shape[1]
        in_specs += [pl.BlockSpec((1, past, kvw), lambda b: (b, 0, 0)),
                     pl.BlockSpec((1, past, kvw), lambda b: (b, 0, 0)),
                     pl.BlockSpec((1, past, LANES), lambda b: (b, 0, 0))]
        args += [cache_k, cache_v, cache_kid]
    return pl.pallas_call(
        functools.partial(_dsa_small_kernel, has_cache=has_cache, nq=nq, n_new=n_new, topk=topk),
        grid=(nb,),
        in_specs=in_specs,
        out_specs=pl.BlockSpec((nq, DS_WIDTH), row),
        out_shape=jax.ShapeDtypeStruct((nb * nq, DS_WIDTH), _MXU),
        compiler_params=_cparams(("parallel",)),
        name="dsa_small",
    )(*args)


def _pad_rows_front(a, rows):
    return jnp.pad(a, ((rows - a.shape[0], 0), (0, 0)))


def _pad_rows_back(a, nb, nq, rows):
    w = a.shape[1]
    return jnp.pad(a.reshape(nb, nq, w), ((0, 0), (0, rows - nq), (0, 0))).reshape(nb * rows, w)


def _transpose_with_ones(v, group_cols):
    n = v.shape[0]
    g = v.shape[1] // group_cols
    vt = v.T.reshape(g, group_cols, n)
    pad = jnp.zeros((g, ONES_ROWS, n), v.dtype).at[:, 0, :].set(1)
    return jnp.concatenate([vt, pad], axis=1).reshape(g * (group_cols + ONES_ROWS), n)


def _dup_pairs(a, hd):
    lead = a.shape[:-1]
    g = a.shape[-1] // hd
    a = a.reshape(*lead, g, 1, hd)
    return jnp.broadcast_to(a, (*lead, g, 2, hd)).reshape(*lead, g * 2 * hd)


def kernel(x_prompt, x_sample, cache_l0_k, cache_l0_v, cache_l1_k, cache_l1_v, cache_l2_k, cache_l2_v, cache_l2_kidx, cache_l3_k, cache_l3_v, meta_tokens, l0_norm, l0_w_in, l0_w_out, l0_lam_q1, l0_lam_k1, l0_lam_q2, l0_lam_k2, l0_subln, l1_norm, l1_w_in, l1_w_out, l1_sinks, l2_norm, l2_w_in, l2_w_out, l3_norm, l3_w_in, l3_w_out, l3_lam_q1, l3_lam_k1, l3_lam_q2, l3_lam_k2, l3_subln, final_norm):
    nb, seq, d = x_prompt.shape
    db, ds, _ = x_sample.shape
    past = cache_l0_k.shape[1]
    sw_rows = cache_l1_k.shape[1]
    assert seq % CHUNK == 0 and meta_tokens.shape[0] == N_META and ds <= LANES

    xs = {"f": x_prompt.reshape(nb * seq, d), "m": meta_tokens.astype(_F32), "s": x_sample.reshape(db * ds, d)}
    pos = {"f": N_META + jnp.arange(seq), "m": jnp.arange(N_META), "s": past + jnp.arange(ds)}
    pos["s"] = jnp.tile(pos["s"], db)

    tabs128 = {n: _rope_tables(p, 128, 1.0) for n, p in pos.items()}
    tabs128_da = {n: _rope_tables(p, 128, DA_HD ** -0.5 * LOG2E) for n, p in pos.items()}
    tabs128_ds = {n: _rope_tables(p, 128, DS_HD ** -0.5 * LOG2E) for n, p in pos.items()}
    tabs64 = {n: _rope_tables(p, 64, 1.0) for n, p in pos.items()}
    tabs64_sw = {n: _rope_tables(p, 64, SW_HD ** -0.5 * LOG2E) for n, p in pos.items()}

    layers = [
        dict(norm=l0_norm, w_in=l0_w_in, w_out=l0_w_out, lam=(l0_lam_q1, l0_lam_k1, l0_lam_q2, l0_lam_k2),
             subln=l0_subln, cache=(cache_l0_k, cache_l0_v)),
        dict(norm=l1_norm, w_in=l1_w_in, w_out=l1_w_out, sinks=l1_sinks, cache=(cache_l1_k, cache_l1_v)),
        dict(norm=l2_norm, w_in=l2_w_in, w_out=l2_w_out, cache=(cache_l2_k, cache_l2_v, cache_l2_kidx)),
        dict(norm=l3_norm, w_in=l3_w_in, w_out=l3_w_out, lam=(l3_lam_q1, l3_lam_k1, l3_lam_q2, l3_lam_k2),
             subln=l3_subln, cache=(cache_l3_k, cache_l3_v)),
    ]
    norms = [lp["norm"] for lp in layers[1:]] + [final_norm]
    depth = len(layers)

    hs = {n: _rms_norm_rows(x, layers[0]["norm"]) for n, x in xs.items()}
    p_st, s_st = [], []
    ys = {}
    for li, lp in enumerate(layers):
        kind = li % N_MIXERS
        w_in = lp["w_in"].astype(_MXU)
        w_out = lp["w_out"].astype(_MXU)
        o, z = {}, {}
        if kind == 0:
            lam_init = 0.8 - 0.6 * math.exp(-0.3 * li)
            wq, wk, wv, wz = (w_in[:, i * DA_WIDTH:(i + 1) * DA_WIDTH] for i in range(4))
            lams = jnp.stack([l.astype(_F32) for l in lp["lam"]])
            subln = lp["subln"].reshape(1, 2 * DA_HD).astype(_F32)
            subln_col = jnp.broadcast_to(lp["subln"].astype(_F32)[:, None], (2 * DA_HD, LANES))
            hshape = (DA_HEADS, 2 * DA_HD)
            q, k32, k16, v32, v16 = {}, {}, {}, {}, {}
            for n in xs:
                (q[n],) = _project(hs[n], wq, [_MXU], *tabs128_da[n])
                k32[n], k16[n] = _project(hs[n], wk, [_F32, _MXU], *tabs128[n])
                if n == "f":
                    v32[n], vt_f = _project(hs[n], wv, [_F32], group_cols=2 * DA_HD)
                else:
                    v32[n], v16[n] = _project(hs[n], wv, [_F32, _MXU])
                (z[n],) = _project(hs[n], wz, [_MXU])
            o["f"] = _diff_frames(q["f"], k16["f"], vt_f, k16["m"], _transpose_with_ones(v16["m"], 2 * DA_HD),
                                  lams, subln_col, lam_init, nb, seq)
            o["m"] = _diff_small(q["m"], k16["m"], v16["m"], None, None, lams, subln, lam_init, 1, N_META)
            ck, cv = lp["cache"]
            o["s"] = _diff_small(q["s"], k16["s"], v16["s"], ck.reshape(db, past, DA_WIDTH),
                                 cv.reshape(db, past, DA_WIDTH), lams, subln, lam_init, db, ds)

            def with_meta(f, mrows):
                mm = jnp.broadcast_to(mrows.reshape(1, N_META, *hshape), (nb, N_META, *hshape))
                return jnp.concatenate([mm, f.reshape(nb, seq, *hshape)], axis=1)

            p_st.append([with_meta(k32["f"], k32["m"]), with_meta(v32["f"], v32["m"])])
            s_st.append([k32["s"].reshape(db, ds, *hshape), v32["s"].reshape(db, ds, *hshape)])
        elif kind == 1:
            kvd = SW_KV * SW_HD
            wq = w_in[:, :SW_WIDTH]
            wk = w_in[:, SW_WIDTH:SW_WIDTH + kvd]
            wv = w_in[:, SW_WIDTH + kvd:SW_WIDTH + 2 * kvd]
            wz = w_in[:, SW_WIDTH + 2 * kvd:]
            wkd, wvd = _dup_pairs(wk, SW_HD), _dup_pairs(wv, SW_HD)
            sink_rows = {}
            q, kv32, kd, vd = {}, {}, {}, {}
            for n, rows in (("f", CHUNK), ("m", N_META), ("s", ds)):
                sk = (lp["sinks"].astype(_F32) * LOG2E).reshape(SW_KV, SW_REP, 1, 1)
                sink_rows[n] = jnp.broadcast_to(sk, (SW_KV, SW_REP, rows, LANES)).reshape(SW_KV, SW_REP * rows, LANES)
                (q[n],) = _project(hs[n], wq, [_MXU], *tabs64_sw[n])
                (kd[n],) = _project(hs[n], wkd, [_MXU], *tabs64[n])
                (vd[n],) = _project(hs[n], wvd, [_MXU])
                (z[n],) = _project(hs[n], wz, [_MXU])
            kmeta_blk = _pad_rows_front(kd["m"], WINDOW)
            vmeta_blk = _pad_rows_front(vd["m"], WINDOW)
            o["f"] = _swa_frames(q["f"], kd["f"], vd["f"], kmeta_blk, vmeta_blk, sink_rows["f"], nb, seq)
            o["m"] = _swa_small(q["m"], kd["m"], vd["m"], None, None, sink_rows["m"], 1, N_META)
            ck, cv = lp["cache"]
            ckd = _dup_pairs(ck.reshape(db, sw_rows, kvd), SW_HD).astype(_MXU)
            cvd = _dup_pairs(cv.reshape(db, sw_rows, kvd), SW_HD).astype(_MXU)
            o["s"] = _swa_small(q["s"], kd["s"], vd["s"], ckd, cvd, sink_rows["s"], db, ds)
            assert seq >= sw_rows and sw_rows % 8 == 0
            h_tail = hs["f"].reshape(nb, seq, d)[:, seq - sw_rows:].reshape(nb * sw_rows, d)
            tail_pos = N_META + seq - sw_rows + jnp.arange(sw_rows)
            tail_tabs, tail_shift = _rope_tables(tail_pos, 64, 1.0)
            (kt,) = _project(h_tail, _pad_cols(wk), [_F32], tail_tabs, tail_shift)
            (vt,) = _project(h_tail, _pad_cols(wv), [_F32])
            p_st.append([kt[:, :kvd].reshape(nb, sw_rows, SW_KV, SW_HD), vt[:, :kvd].reshape(nb, sw_rows, SW_KV, SW_HD)])
            (ks,) = _project(hs["s"], _pad_cols(wk), [_F32], *tabs64["s"])
            (vs_,) = _project(hs["s"], _pad_cols(wv), [_F32])
            ks = ks[:, :kvd].reshape(db, ds, SW_KV, SW_HD)
            vs_ = vs_[:, :kvd].reshape(db, ds, SW_KV, SW_HD)
            k_all = jnp.concatenate([ck, ks], axis=1)
            v_all = jnp.concatenate([cv, vs_], axis=1)
            s_st.append([k_all[:, k_all.shape[1] - sw_rows:], v_all[:, v_all.shape[1] - sw_rows:]])
        else:
            kvd = DS_KV * DS_HD
            c0 = 0
            wq = w_in[:, c0:c0 + DS_WIDTH]; c0 += DS_WIDTH
            wk = w_in[:, c0:c0 + kvd]; c0 += kvd
            wv = w_in[:, c0:c0 + kvd]; c0 += kvd
            wz = w_in[:, c0:c0 + DS_WIDTH]; c0 += DS_WIDTH
            wqi = w_in[:, c0:c0 + IDX_HEADS * IDX_HD]; c0 += IDX_HEADS * IDX_HD
            wki = w_in[:, c0:c0 + IDX_HD]; c0 += IDX_HD
            wwi = w_in[:, c0:c0 + IDX_HEADS]
            wkid = jnp.concatenate([wki, wki], axis=1)
            wwi_p = _pad_cols(wwi)
            wi_scale = IDX_HEADS ** -0.5 * IDX_HD ** -0.5
            q, k32, k16, v32, v16, qi, kid32, kid16, wi = {}, {}, {}, {}, {}, {}, {}, {}, {}
            for n in xs:
                (q[n],) = _project(hs[n], wq, [_MXU], *tabs128_ds[n])
                k32[n], k16[n] = _project(hs[n], wk, [_F32, _MXU], *tabs128[n])
                if n == "f":
                    v32[n], vt_f = _project(hs[n], wv, [_F32], group_cols=DS_HD)
                else:
                    v32[n], v16[n] = _project(hs[n], wv, [_F32, _MXU])
                (z[n],) = _project(hs[n], wz, [_MXU])
                (qi[n],) = _project(hs[n], wqi, [_MXU], *tabs64[n])
                kid32[n], kid16[n] = _project(hs[n], wkid, [_F32, _MXU], *tabs64[n])
                (wi[n],) = _project(hs[n], wwi_p, [_F32], scale=wi_scale)
            topk_p = min(TOPK_MAX, seq // 4)
            topk_s = min(TOPK_MAX, (past + ds) // 4)
            km_blk = _pad_rows_back(k16["m"], 1, N_META, LANES)
            vm_blk = _pad_rows_back(v16["m"], 1, N_META, LANES)
            kidm_blk = _pad_rows_back(kid16["m"], 1, N_META, LANES)
            o["f"] = _dsa_frames(q["f"], k16["f"], vt_f, qi["f"], kid16["f"], wi["f"],
                                 km_blk, _transpose_with_ones(vm_blk, DS_HD), kidm_blk, nb, seq, topk_p)
            o["m"] = _dsa_small(q["m"], km_blk, vm_blk, qi["m"], kidm_blk, wi["m"], None, None, None,
                                1, N_META, N_META, topk_p)
            ck, cv, cki = lp["cache"]
            ckid = jnp.concatenate([cki, cki], axis=-1).astype(_MXU)
            o["s"] = _dsa_small(q["s"], _pad_rows_back(k16["s"], db, ds, LANES), _pad_rows_back(v16["s"], db, ds, LANES),
                                qi["s"], _pad_rows_back(kid16["s"], db, ds, LANES), wi["s"],
                                ck.reshape(db, past, kvd), cv.reshape(db, past, kvd), ckid,
                                db, ds, ds, topk_s)

            def with_meta(f, mrows, *tail):
                mm = jnp.broadcast_to(mrows.reshape(1, N_META, *tail), (nb, N_META, *tail))
                return jnp.concatenate([mm, f.reshape(nb, seq, *tail)], axis=1)

            p_st.append([with_meta(k32["f"], k32["m"], DS_KV, DS_HD), with_meta(v32["f"], v32["m"], DS_KV, DS_HD),
                         with_meta(kid32["f"][:, :IDX_HD], kid32["m"][:, :IDX_HD], IDX_HD)])
            s_st.append([k32["s"].reshape(db, ds, DS_KV, DS_HD), v32["s"].reshape(db, ds, DS_KV, DS_HD),
                         kid32["s"][:, :IDX_HD].reshape(db, ds, IDX_HD)])
        last = li == depth - 1
        for n in xs:
            res = _out_project(o[n], z[n], w_out, xs[n], norms[li], last)
            if last:
                (ys[n],) = res
            else:
                xs[n], hs[n] = res

    y_prompt = ys["f"].reshape(nb, seq, d)
    y_sample = ys["s"].reshape(db, ds, d)
    return (y_prompt, y_sample,
            p_st[0][0], p_st[0][1], s_st[0][0], s_st[0][1],
            p_st[1][0], p_st[1][1], s_st[1][0], s_st[1][1],
            p_st[2][0], p_st[2][1], p_st[2][2], s_st[2][0], s_st[2][1], s_st[2][2],
            p_st[3][0], p_st[3][1], s_st[3][0], s_st[3][1])


def _pad_cols(w):
    n = w.shape[1]
    return jnp.pad(w, ((0, 0), (0, (-n) % LANES)))
```

```python
import functools
import math

import jax
import jax.numpy as jnp
from jax import lax
from jax.experimental import pallas as pl
from jax.experimental.pallas import tpu as pltpu

CHUNK = 64
N_META = 16
ROPE_THETA = 500000.0
NORM_EPS = 1e-6
NEG_INF = -1e30
DA_HEADS = 8
DA_HD = 128
DA_WIDTH = DA_HEADS * 2 * DA_HD
SW_HEADS = 32
SW_KV = 4
SW_HD = 64
SW_WIDTH = SW_HEADS * SW_HD
SW_REP = SW_HEADS // SW_KV
WINDOW = 128
DS_HEADS = 16
DS_KV = 4
DS_HD = 128
DS_WIDTH = DS_HEADS * DS_HD
DS_REP = DS_HEADS // DS_KV
IDX_HEADS = 16
IDX_HD = 64
TOPK_MAX = 256
N_MIXERS = 3

LANES = 128
INT_MIN = -(2 ** 31)
LOG2E = math.log2(math.e)
ONES_ROWS = 16
VMEM_LIMIT = 56 * 1024 * 1024

_MXU = jnp.bfloat16
_F32 = jnp.float32

_NT = (((1,), (1,)), ((), ()))


def _cparams(sem):
    return pltpu.CompilerParams(dimension_semantics=sem, vmem_limit_bytes=VMEM_LIMIT)


def _row_tile(m, pref):
    t = pref
    while t > 8 and m % t:
        t //= 2
    assert m % t == 0, (m, pref)
    return t


def _dot(a, b):
    return jnp.dot(a, b, preferred_element_type=_F32)


def _dot_nt(a, b):
    return lax.dot_general(a, b, _NT, preferred_element_type=_F32)


def _norm_kernel(x_ref, g_ref, h_ref):
    x = x_ref[...]
    y = x * lax.rsqrt(jnp.mean(x * x, axis=-1, keepdims=True) + NORM_EPS)
    h_ref[...] = (y * g_ref[...]).astype(h_ref.dtype)


def _rms_norm_rows(x, g):
    m, d = x.shape
    out_dtype = _MXU
    tm = _row_tile(m, 512)
    return pl.pallas_call(
        _norm_kernel,
        grid=(m // tm,),
        in_specs=[pl.BlockSpec((tm, d), lambda i: (i, 0)),
                  pl.BlockSpec((1, d), lambda i: (0, 0))],
        out_specs=pl.BlockSpec((tm, d), lambda i: (i, 0)),
        out_shape=jax.ShapeDtypeStruct((m, d), out_dtype),
        compiler_params=_cparams(("parallel",)),
        name="rms_norm",
    )(x, g.reshape(1, d).astype(_F32))


def _proj_kernel(*refs, rope_shift, scale, n_out, tn, group_cols):
    h_ref, w_ref = refs[0], refs[1]
    pos = 2
    if rope_shift is not None:
        c_ref, a_ref, b_ref = refs[2:5]
        pos = 5
    outs = refs[pos:pos + n_out]
    ot_ref = refs[pos + n_out] if group_cols else None
    if ot_ref is not None:
        tm = ot_ref.shape[2]
        row = lax.broadcasted_iota(jnp.int32, (ONES_ROWS, tm), 0)
        ones_blk = jnp.where(row == 0, 1.0, 0.0).astype(ot_ref.dtype)
        for g in range(tn // group_cols):
            base = g * (group_cols + ONES_ROWS) + group_cols
            ot_ref[0, base:base + ONES_ROWS, :] = ones_blk
    h = h_ref[...]
    cw = 2 * LANES if tn % (2 * LANES) == 0 else LANES
    for c in range(tn // cw):
        wide = _dot(h, w_ref[:, c * cw:(c + 1) * cw])
        for sub in range(cw // LANES):
            acc = wide[:, sub * LANES:(sub + 1) * LANES]
            if rope_shift is not None:
                acc = (acc * c_ref[...]
                       + pltpu.roll(acc, rope_shift, 1) * a_ref[...]
                       + pltpu.roll(acc, LANES - rope_shift, 1) * b_ref[...])
            elif scale is not None:
                acc = acc * scale
            cols = slice(c * cw + sub * LANES, c * cw + (sub + 1) * LANES)
            for o in outs:
                o[:, cols] = acc.astype(o.dtype)
            if ot_ref is not None:
                lo = cols.start
                r0 = (lo // group_cols) * (group_cols + ONES_ROWS) + lo % group_cols
                ot_ref[0, r0:r0 + LANES, :] = acc.T.astype(ot_ref.dtype)


def _project(h, w, out_dtypes, tabs=None, rope_shift=None, scale=None, group_cols=0):
    m, d = h.shape
    n = w.shape[1]
    tm = _row_tile(m if tabs is None else math.gcd(m, tabs[0].shape[0]), 512)
    tn = n if n <= 2048 else 2048
    assert n % tn == 0 and tn % LANES == 0
    in_specs = [pl.BlockSpec((tm, d), lambda j, i: (i, 0)),
                pl.BlockSpec((d, tn), lambda j, i: (0, j))]
    args = [h, w]
    if tabs is not None:
        trows = tabs[0].shape[0]
        assert trows % tm == 0
        nrep = trows // tm
        for t in tabs:
            in_specs.append(pl.BlockSpec((tm, LANES), lambda j, i: (i % nrep, 0)))
            args.append(t)
    else:
        rope_shift = None
    out_specs = [pl.BlockSpec((tm, tn), lambda j, i: (i, j)) for _ in out_dtypes]
    out_shape = [jax.ShapeDtypeStruct((m, n), dt) for dt in out_dtypes]
    if group_cols:
        assert tn % group_cols == 0 and group_cols % LANES == 0
        grow = group_cols + ONES_ROWS
        out_specs.append(pl.BlockSpec((1, tn // group_cols * grow, tm), lambda j, i: (i, j, 0)))
        out_shape.append(jax.ShapeDtypeStruct((m // tm, n // group_cols * grow, tm), _MXU))
    outs = pl.pallas_call(
        functools.partial(_proj_kernel, rope_shift=rope_shift, scale=scale,
                          n_out=len(out_dtypes), tn=tn, group_cols=group_cols),
        grid=(n // tn, m // tm),
        in_specs=in_specs,
        out_specs=out_specs,
        out_shape=out_shape,
        compiler_params=_cparams(("parallel", "parallel")),
        name="in_proj",
    )(*args)
    return outs


def _outproj_kernel(o_ref, z_ref, w_ref, x_ref, g_ref, *outs, emit_x):
    z = z_ref[...].astype(_F32)
    a = o_ref[...].astype(_F32) * (z / (1.0 + jnp.exp(-z)))
    y = x_ref[...] + _dot(a.astype(w_ref.dtype), w_ref[...])
    if emit_x:
        outs[0][...] = y
    n = y * lax.rsqrt(jnp.mean(y * y, axis=-1, keepdims=True) + NORM_EPS) * g_ref[...]
    outs[-1][...] = n.astype(outs[-1].dtype)


def _out_project(o, z, w, x, g_next, last):
    m, d = x.shape
    wd = o.shape[1]
    tm = _row_tile(m, 512)
    row = lambda i: (i, 0)
    fixed = lambda i: (0, 0)
    if last:
        out_shape = [jax.ShapeDtypeStruct((m, d), _F32)]
    else:
        out_shape = [jax.ShapeDtypeStruct((m, d), _F32), jax.ShapeDtypeStruct((m, d), _MXU)]
    return pl.pallas_call(
        functools.partial(_outproj_kernel, emit_x=not last),
        grid=(m // tm,),
        in_specs=[pl.BlockSpec((tm, wd), row), pl.BlockSpec((tm, wd), row),
                  pl.BlockSpec((wd, d), fixed), pl.BlockSpec((tm, d), row),
                  pl.BlockSpec((1, d), fixed)],
        out_specs=[pl.BlockSpec((tm, d), row) for _ in out_shape],
        out_shape=out_shape,
        compiler_params=_cparams(("parallel",)),
        name="out_proj",
    )(o, z, w, x, g_next.reshape(1, d).astype(_F32))


def _rope_tables(pos, hd, scale=1.0):
    rd = hd // 4
    half = rd // 2
    lane = jnp.arange(LANES)
    d = lane % hd
    f = d % half
    inv_freq = ROPE_THETA ** (-(jnp.arange(half, dtype=_F32) * 2.0 / rd))
    ang = pos.astype(_F32)[:, None] * inv_freq[f][None, :]
    cos, sin = jnp.cos(ang), jnp.sin(ang)
    in_rot = (d < rd)[None, :]
    c = jnp.where(in_rot, cos, 1.0)
    a = jnp.where(((d >= half) & (d < rd))[None, :], sin, 0.0)
    b = jnp.where((d < half)[None, :], -sin, 0.0)
    s = jnp.float32(scale)
    return (c * s, a * s, b * s), half


def _lam_value(lam_ref, lam_init):
    l = lam_ref[...]
    s1 = jnp.sum(l[0:1] * l[1:2], axis=-1, keepdims=True)
    s2 = jnp.sum(l[2:3] * l[3:4], axis=-1, keepdims=True)
    return jnp.exp(s1) - jnp.exp(s2) + lam_init


def _diff_finish(acc1, l1, acc2, l2, lam, sub, lam_init):
    o = acc1 / l1 - lam * (acc2 / l2)
    o = o * lax.rsqrt(jnp.mean(o * o, axis=-1, keepdims=True) + NORM_EPS) * sub
    return o * (1.0 - lam_init)


def _diff_frames_kernel(q_ref, k_ref, vt_ref, km_ref, vmt_ref, lam_ref, subc_ref, o_ref,
                        acc_s, sa_s, sb_s, pa_s, pb_s, *, tq, lam_init):
    i = pl.program_id(2)
    hw = 2 * DA_HD

    def q_half(j):
        return q_ref[:, j * DA_HD:(j + 1) * DA_HD]

    km = km_ref[...]
    vmt = vmt_ref[...]
    ms = []
    for j in range(2):
        s = _dot_nt(km[:, j * DA_HD:(j + 1) * DA_HD], q_half(j))
        m = jnp.max(s, axis=0, keepdims=True)
        ms.append(m)
        acc_s[j] = _dot(vmt, jnp.exp2(s - m).astype(vmt.dtype))

    def scores_into(dst, t):
        start = pl.multiple_of(t * tq, tq)
        for j in range(2):
            dst[j] = _dot_nt(k_ref[pl.ds(start, tq), j * DA_HD:(j + 1) * DA_HD], q_half(j))

    def value_update(t, a, p_rd):
        vt = vt_ref[t]
        for j in range(2):
            acc_s[j] = a[j] * acc_s[j] + _dot(vt, p_rd[j])

    def tile_step(t, carry, src, dst, p_rd, p_wr, masked, has_next):
        m, a = carry
        if masked:
            kc = lax.broadcasted_iota(jnp.int32, (tq, tq), 0) // CHUNK
            qc = lax.broadcasted_iota(jnp.int32, (tq, tq), 1) // CHUNK
            ok = kc <= qc
        m_out, a_out = [], []
        for j in range(2):
            s = src[j]
            if masked:
                s = jnp.where(ok, s, NEG_INF)
            m_new = jnp.maximum(m[j], jnp.max(s, axis=0, keepdims=True))
            a_out.append(jnp.exp2(m[j] - m_new))
            m_out.append(m_new)
            p_wr[j] = jnp.exp2(s - m_new).astype(p_wr.dtype)
        value_update(jnp.maximum(t - 1, 0), a, p_rd)
        if has_next:
            scores_into(dst, t + 1)
        return tuple(m_out), tuple(a_out)

    scores_into(sa_s, 0)
    pb_s[...] = jnp.zeros_like(pb_s)
    one = jnp.ones((1, tq), _F32)
    carry = ((ms[0], ms[1]), (one, one))

    def pair_body(u, carry):
        carry = tile_step(2 * u, carry, sa_s, sb_s, pb_s, pa_s, False, True)
        return tile_step(2 * u + 1, carry, sb_s, sa_s, pa_s, pb_s, False, True)

    carry = lax.fori_loop(0, i // 2, pair_body, carry)

    @pl.when(i % 2 == 0)
    def _():
        _, a = tile_step(i, carry, sa_s, sb_s, pb_s, pa_s, True, False)
        value_update(i, a, pa_s)

    @pl.when(i % 2 == 1)
    def _():
        c = tile_step(i - 1, carry, sa_s, sb_s, pb_s, pa_s, False, True)
        _, a = tile_step(i, c, sb_s, sa_s, pa_s, pb_s, True, False)
        value_update(i, a, pb_s)

    acc0 = acc_s[0]
    acc1 = acc_s[1]
    lam = _lam_value(lam_ref, lam_init)
    inv0 = 1.0 / acc0[hw:hw + 1]
    inv1 = lam / acc1[hw:hw + 1]
    ot = acc0[:hw] * inv0 - acc1[:hw] * inv1
    sub = jnp.concatenate([subc_ref[...]] * (tq // LANES), axis=1)
    ot = ot * lax.rsqrt(jnp.mean(ot * ot, axis=0, keepdims=True) + NORM_EPS) * sub * (1.0 - lam_init)
    o_ref[...] = ot.T.astype(o_ref.dtype)


def _diff_frames(q, k, vt, km, vmt, lams, subln_col, lam_init, nb, seq):
    m = q.shape[0]
    hw = 2 * DA_HD
    hr = hw + ONES_ROWS
    tq = vt.shape[2]
    assert seq % tq == 0 and tq % LANES == 0
    nt = seq // tq
    return pl.pallas_call(
        functools.partial(_diff_frames_kernel, tq=tq, lam_init=lam_init),
        grid=(nb, DA_HEADS, nt),
        in_specs=[pl.BlockSpec((tq, hw), lambda b, h, i: (b * nt + i, h)),
                  pl.BlockSpec((seq, hw), lambda b, h, i: (b, h)),
                  pl.BlockSpec((nt, hr, tq), lambda b, h, i: (b, h, 0)),
                  pl.BlockSpec((N_META, hw), lambda b, h, i: (0, h)),
                  pl.BlockSpec((hr, N_META), lambda b, h, i: (h, 0)),
                  pl.BlockSpec((4, DA_HD), lambda b, h, i: (0, 0)),
                  pl.BlockSpec((hw, LANES), lambda b, h, i: (0, 0))],
        out_specs=pl.BlockSpec((tq, hw), lambda b, h, i: (b * nt + i, h)),
        out_shape=jax.ShapeDtypeStruct((m, DA_WIDTH), _MXU),
        scratch_shapes=[pltpu.VMEM((2, hr, tq), _F32),
                        pltpu.VMEM((2, tq, tq), _F32), pltpu.VMEM((2, tq, tq), _F32),
                        pltpu.VMEM((2, tq, tq), _MXU), pltpu.VMEM((2, tq, tq), _MXU)],
        compiler_params=_cparams(("parallel", "parallel", "arbitrary")),
        name="diff_frames",
    )(q, k, vt, km, vmt, lams, subln_col)


def _diff_small_kernel(*refs, has_cache, lam_init):
    if has_cache:
        q_ref, k_ref, v_ref, ck_ref, cv_ref, lam_ref, sub_ref, o_ref = refs
    else:
        q_ref, k_ref, v_ref, lam_ref, sub_ref, o_ref = refs
    q = q_ref[...]
    kn = k_ref[...]
    vn = v_ref[...]
    parts = [(kn, vn)]
    if has_cache:
        parts.append((ck_ref[0].astype(kn.dtype), cv_ref[0].astype(vn.dtype)))
    accs, ls = [], []
    for j in range(2):
        qj = q[:, j * DA_HD:(j + 1) * DA_HD]
        ss = [_dot_nt(qj, kp[:, j * DA_HD:(j + 1) * DA_HD]) for kp, _ in parts]
        m = ss[0].max(axis=-1, keepdims=True)
        for s in ss[1:]:
            m = jnp.maximum(m, s.max(axis=-1, keepdims=True))
        l = 0.0
        acc = 0.0
        for s, (_, vp) in zip(ss, parts):
            p = jnp.exp2(s - m)
            l = l + jnp.sum(p, axis=-1, keepdims=True)
            acc = acc + _dot(p.astype(vp.dtype), vp)
        accs.append(acc)
        ls.append(l)
    lam = _lam_value(lam_ref, lam_init)
    o = _diff_finish(accs[0], ls[0], accs[1], ls[1], lam, sub_ref[...], lam_init)
    o_ref[...] = o.astype(o_ref.dtype)


def _diff_small(q, k, v, cache_k, cache_v, lams, subln, lam_init, nb, nq):
    hw = 2 * DA_HD
    has_cache = cache_k is not None
    qspec = pl.BlockSpec((nq, hw), lambda b, h: (b, h))
    in_specs = [qspec, qspec, qspec]
    args = [q, k, v]
    if has_cache:
        past = cache_k.shape[1]
        cspec = pl.BlockSpec((1, past, hw), lambda b, h: (b, 0, h))
        in_specs += [cspec, cspec]
        args += [cache_k, cache_v]
    in_specs += [pl.BlockSpec((4, DA_HD), lambda b, h: (0, 0)),
                 pl.BlockSpec((1, hw), lambda b, h: (0, 0))]
    args += [lams, subln]
    return pl.pallas_call(
        functools.partial(_diff_small_kernel, has_cache=has_cache, lam_init=lam_init),
        grid=(nb, DA_HEADS),
        in_specs=in_specs,
        out_specs=qspec,
        out_shape=jax.ShapeDtypeStruct((nb * nq, DA_WIDTH), _MXU),
        compiler_params=_cparams(("parallel", "parallel")),
        name="diff_small",
    )(*args)


def _stack_heads(qg, nheads):
    n = qg.shape[0]
    lane = lax.broadcasted_iota(jnp.int32, (n, LANES), 1)
    lo = lane < SW_HD
    pieces = []
    for r in range(nheads):
        blk = qg[:, (r // 2) * LANES:(r // 2 + 1) * LANES]
        keep = lo if r % 2 == 0 else jnp.logical_not(lo)
        pieces.append(jnp.where(keep, blk, jnp.zeros_like(blk)))
    return jnp.concatenate(pieces, axis=0)


def _unstack_heads(o, n, nheads):
    lane = lax.broadcasted_iota(jnp.int32, (n, LANES), 1)
    lo = lane < SW_HD
    return [jnp.where(lo, o[(2 * p) * n:(2 * p + 1) * n], o[(2 * p + 1) * n:(2 * p + 2) * n])
            for p in range(nheads // 2)]


def _sink_softmax_pv(ss, vs, sk):
    m = sk
    for s in ss:
        m = jnp.maximum(m, jnp.max(s, axis=-1, keepdims=True))
    den = jnp.exp2(sk - m)
    acc = 0.0
    for s, v in zip(ss, vs):
        e = jnp.exp2(s - m)
        den = den + jnp.sum(e, axis=-1, keepdims=True)
        acc = acc + _dot(e.astype(v.dtype), v)
    return acc / den


def _swa_frames_kernel(q_ref, kc_ref, vc_ref, kp_ref, vp_ref, km_ref, vm_ref, sk_ref, o_ref,
                       kcat, vcat_t, bias_s, *, tq):
    i = pl.program_id(1)
    first = i == 0
    prev_rows = WINDOW
    pair = 2 * CHUNK
    kwin = 2 * pair
    ncol = SW_REP * pair
    kcat[0:prev_rows] = jnp.where(first, km_ref[...], kp_ref[...])
    kcat[prev_rows:prev_rows + tq] = kc_ref[...]
    vprev = jnp.where(first, vm_ref[...], vp_ref[...])
    vcat_t[:, 0:prev_rows] = vprev.astype(_F32).T.astype(vcat_t.dtype)
    vcat_t[:, prev_rows:prev_rows + tq] = vc_ref[...].astype(_F32).T.astype(vcat_t.dtype)

    krow = lax.broadcasted_iota(jnp.int32, (kwin, ncol), 0)
    second = (lax.broadcasted_iota(jnp.int32, (kwin, ncol), 1) % pair) >= CHUNK
    lo = jnp.where(second, CHUNK, 0)
    hi = jnp.where(second, kwin, kwin - CHUNK)
    bias_s[0] = jnp.where(jnp.logical_and(krow >= lo, krow < hi), 0.0, NEG_INF)
    lo_first = jnp.maximum(lo, jnp.where(first, prev_rows - N_META, 0))
    bias_s[1] = jnp.where(jnp.logical_and(krow >= lo_first, krow < hi), 0.0, NEG_INF)

    for k in range(tq // pair):
        rows = slice(k * pair, (k + 1) * pair)
        win = slice(k * pair, k * pair + kwin)
        bias = bias_s[1 if k == 0 else 0]
        for g in range(SW_KV):
            gl = slice(g * LANES, (g + 1) * LANES)
            qst = _stack_heads(q_ref[rows, g * SW_REP * SW_HD:(g + 1) * SW_REP * SW_HD], SW_REP)
            s = _dot_nt(kcat[win, gl], qst) + bias
            sk = sk_ref[g]
            m = jnp.maximum(jnp.max(s, axis=0, keepdims=True), sk)
            e = jnp.exp2(s - m)
            den = jnp.sum(e, axis=0, keepdims=True) + jnp.exp2(sk - m)
            ot = _dot(vcat_t[gl, win], e.astype(vcat_t.dtype)) * (1.0 / den)
            o = ot.T
            for p, blk in enumerate(_unstack_heads(o, pair, SW_REP)):
                lo = g * SW_REP * SW_HD + p * LANES
                o_ref[rows, lo:lo + LANES] = blk.astype(o_ref.dtype)


def _swa_frames(q, kd, vd, kmeta_blk, vmeta_blk, sink_cols, nb, seq):
    m = q.shape[0]
    tq = _row_tile(seq, 512)
    assert tq % (2 * CHUNK) == 0 and WINDOW == 2 * CHUNK
    nt = seq // tq
    pw = SW_KV * LANES
    per = tq // WINDOW
    cur = lambda b, i: (b * nt + i, 0)
    prev = lambda b, i: (jnp.maximum((b * nt + i) * per - 1, 0), 0)
    fixed = lambda b, i: (0, 0)
    return pl.pallas_call(
        functools.partial(_swa_frames_kernel, tq=tq),
        grid=(nb, nt),
        in_specs=[pl.BlockSpec((tq, SW_WIDTH), cur),
                  pl.BlockSpec((tq, pw), cur), pl.BlockSpec((tq, pw), cur),
                  pl.BlockSpec((WINDOW, pw), prev), pl.BlockSpec((WINDOW, pw), prev),
                  pl.BlockSpec((WINDOW, pw), fixed), pl.BlockSpec((WINDOW, pw), fixed),
                  pl.BlockSpec((SW_KV, 1, SW_REP * 2 * CHUNK), lambda b, i: (0, 0, 0))],
        out_specs=pl.BlockSpec((tq, SW_WIDTH), cur),
        out_shape=jax.ShapeDtypeStruct((m, SW_WIDTH), _MXU),
        scratch_shapes=[pltpu.VMEM((WINDOW + tq, pw), _MXU), pltpu.VMEM((pw, WINDOW + tq), _MXU),
                        pltpu.VMEM((2, 4 * CHUNK, SW_REP * 2 * CHUNK), _F32)],
        compiler_params=_cparams(("parallel", "arbitrary")),
        name="swa_frames",
    )(q, kd, vd, kd, vd, kmeta_blk, vmeta_blk, sink_cols)


def _swa_small_kernel(*refs, has_cache, nq):
    if has_cache:
        q_ref, k_ref, v_ref, ck_ref, cv_ref, sk_ref, o_ref = refs
    else:
        q_ref, k_ref, v_ref, sk_ref, o_ref = refs
    for g in range(SW_KV):
        gl = slice(g * LANES, (g + 1) * LANES)
        qst = _stack_heads(q_ref[:, g * SW_REP * SW_HD:(g + 1) * SW_REP * SW_HD], SW_REP)
        ks, vs = [k_ref[:, gl]], [v_ref[:, gl]]
        if has_cache:
            ks.append(ck_ref[0][:, gl])
            vs.append(cv_ref[0][:, gl])
        ss = [_dot_nt(qst, kk) for kk in ks]
        o = _sink_softmax_pv(ss, vs, sk_ref[g][:, 0:1])
        for p, blk in enumerate(_unstack_heads(o, nq, SW_REP)):
            lo = g * SW_REP * SW_HD + p * LANES
            o_ref[:, lo:lo + LANES] = blk.astype(o_ref.dtype)


def _swa_small(q, kd, vd, cache_kd, cache_vd, sink_rows, nb, nq):
    pw = SW_KV * LANES
    has_cache = cache_kd is not None
    row = lambda b: (b, 0)
    in_specs = [pl.BlockSpec((nq, SW_WIDTH), row), pl.BlockSpec((nq, pw), row), pl.BlockSpec((nq, pw), row)]
    args = [q, kd, vd]
    if has_cache:
        past = cache_kd.shape[1]
        cspec = pl.BlockSpec((1, past, pw), lambda b: (b, 0, 0))
        in_specs += [cspec, cspec]
        args += [cache_kd, cache_vd]
    in_specs.append(pl.BlockSpec((SW_KV, SW_REP * nq, LANES), lambda b: (0, 0, 0)))
    args.append(sink_rows)
    return pl.pallas_call(
        functools.partial(_swa_small_kernel, has_cache=has_cache, nq=nq),
        grid=(nb,),
        in_specs=in_specs,
        out_specs=pl.BlockSpec((nq, SW_WIDTH), row),
        out_shape=jax.ShapeDtypeStruct((nb * nq, SW_WIDTH), _MXU),
        compiler_params=_cparams(("parallel",)),
        name="swa_small",
    )(*args)


def _order_key(score, valid):
    bits = lax.bitcast_convert_type(score + 0.0, jnp.int32)
    key = jnp.where(bits < 0, bits ^ jnp.int32(0x7FFFFFFF), bits)
    return jnp.where(valid, key, jnp.int32(INT_MIN))


def _stack_idx_heads(qi):
    return _stack_heads(qi, IDX_HEADS)


def _index_scores(qst, wb_of, kid, n):
    score = None
    hb = 4
    for h0 in range(0, IDX_HEADS, hb):
        sc = _dot_nt(qst[h0 * n:(h0 + hb) * n], kid)
        for h in range(h0, h0 + hb):
            part = jnp.maximum(sc[(h - h0) * n:(h - h0 + 1) * n], 0.0) * wb_of(h)
            score = part if score is None else score + part
    return score


def _index_scores_t(kid, qst, wit, n):
    score = None
    hb = 4
    for h0 in range(0, IDX_HEADS, hb):
        sc = _dot_nt(kid, qst[h0 * n:(h0 + hb) * n])
        for h in range(h0, h0 + hb):
            part = jnp.maximum(sc[:, (h - h0) * n:(h - h0 + 1) * n], 0.0) * wit[h:h + 1, :]
            score = part if score is None else score + part
    return score


def _dsa_frames_kernel(q_ref, k_ref, vt_ref, qi_ref, kid_ref, wi_ref, km_ref, vmt_ref, kidm_ref, o_ref,
                       keym_s, key_s, biasm_s, bias_s, qst_s, wit_s, qg_s, m_s, acc_s, *, tq, tk, topk):
    i = pl.program_id(1)
    n_t = (i * tq + tq + tk - 1) // tk
    qst_s[...] = _stack_idx_heads(qi_ref[...])
    wit_s[...] = wi_ref[...].T
    sc_m = _index_scores_t(kidm_ref[...], qst_s[...], wit_s[...], tq)
    row_m = lax.broadcasted_iota(jnp.int32, (LANES, tq), 0)
    keym_s[...] = _order_key(sc_m, row_m < N_META)

    q_chunk = (lax.broadcasted_iota(jnp.int32, (tk, tq), 1) + i * tq) // CHUNK

    def score_body(t, carry):
        start = pl.multiple_of(t * tk, tk)
        sc = _index_scores_t(kid_ref[pl.ds(start, tk), :], qst_s[...], wit_s[...], tq)
        k_chunk = (lax.broadcasted_iota(jnp.int32, (tk, tq), 0) + t * tk) // CHUNK
        key_s[t] = _order_key(sc, k_chunk <= q_chunk)
        return carry

    lax.fori_loop(0, n_t, score_body, 0)

    def count_ge(thr):
        def fold(x):
            return jnp.sum(x.reshape(x.shape[0] // 64, 64, tq), axis=0)

        part = fold(jnp.where(keym_s[...] >= thr, 1.0, 0.0))

        def cbody(t, acc):
            return acc + fold(jnp.where(key_s[t] >= thr, 1.0, 0.0))

        part = lax.fori_loop(0, n_t, cbody, part)
        return jnp.sum(part, axis=0, keepdims=True)

    def bit_body(it, carry):
        u, cnt_u = carry
        bit = jnp.int32(31) - it
        cand = u | lax.shift_left(jnp.int32(1), bit)
        cnt = count_ge(cand ^ jnp.int32(INT_MIN))
        ok = cnt >= float(topk)
        return jnp.where(ok, cand, u), jnp.where(ok, cnt, cnt_u)

    u, cnt_u = lax.fori_loop(0, 32, bit_body, (jnp.zeros((1, tq), jnp.int32), jnp.zeros((1, tq), _F32)))
    thr = jnp.maximum(u ^ jnp.int32(INT_MIN), jnp.int32(INT_MIN + 1))

    biasm_s[...] = jnp.where(keym_s[...] >= thr, 0.0, NEG_INF)

    def bias_body(t, carry):
        bias_s[t] = jnp.where(key_s[t] >= thr, 0.0, NEG_INF)
        return carry

    lax.fori_loop(0, n_t, bias_body, 0)

    @pl.when(jnp.max(cnt_u) > float(topk))
    def _():
        need = float(topk) - count_ge(thr + 1)

        def tie_bias(key, seen):
            rows = key.shape[0]
            eq = key == thr
            r = lax.broadcasted_iota(jnp.int32, (rows, rows), 0)
            c = lax.broadcasted_iota(jnp.int32, (rows, rows), 1)
            upto = _dot(jnp.where(c <= r, 1.0, 0.0).astype(_MXU), jnp.where(eq, 1.0, 0.0).astype(_MXU))
            keep = jnp.logical_or(key > thr, jnp.logical_and(eq, seen + upto <= need))
            return jnp.where(keep, 0.0, NEG_INF), seen + upto[rows - 1:rows]

        biasm_s[...], seen = tie_bias(keym_s[...], jnp.zeros((1, tq), _F32))

        def tie_body(t, seen):
            bias_s[t], seen = tie_bias(key_s[t], seen)
            return seen

        lax.fori_loop(0, n_t, tie_body, seen)

    vr = DS_HD + ONES_ROWS
    groups = [(g, slice(g * DS_HD, (g + 1) * DS_HD), slice(g * vr, (g + 1) * vr)) for g in range(DS_KV)]
    for g, gl, gv in groups:
        qg_s[g] = jnp.concatenate(
            [q_ref[:, (g * DS_REP + r) * DS_HD:(g * DS_REP + r + 1) * DS_HD] for r in range(DS_REP)], axis=0)
        s = _dot_nt(km_ref[:, gl], qg_s[g]) + jnp.concatenate([biasm_s[...]] * DS_REP, axis=1)
        m0 = jnp.max(s, axis=0, keepdims=True)
        m_s[g] = m0
        acc_s[g] = _dot(vmt_ref[gv, :], jnp.exp2(s - m0).astype(_MXU))

    def att_body(t, carry):
        start = pl.multiple_of(t * tk, tk)
        bias = jnp.concatenate([bias_s[t]] * DS_REP, axis=1)
        for pair in (groups[:2], groups[2:]):
            ss = [_dot_nt(k_ref[pl.ds(start, tk), gl], qg_s[g]) for g, gl, _ in pair]
            for (g, _, gv), s in zip(pair, ss):
                s = s + bias
                m_prev = m_s[g]
                m_new = jnp.maximum(m_prev, jnp.max(s, axis=0, keepdims=True))
                alpha = jnp.exp2(m_prev - m_new)
                p = jnp.exp2(s - m_new).astype(_MXU)
                acc_s[g] = alpha * acc_s[g] + _dot(vt_ref[t, gv, :], p)
                m_s[g] = m_new
        return carry

    lax.fori_loop(0, n_t, att_body, 0)
    for g in range(DS_KV):
        acc = acc_s[g]
        ot = acc[:DS_HD] * (1.0 / acc[DS_HD:DS_HD + 1])
        for r in range(DS_REP):
            lo = (g * DS_REP + r) * DS_HD
            o_ref[:, lo:lo + DS_HD] = ot[:, r * tq:(r + 1) * tq].T.astype(o_ref.dtype)


def _dsa_frames(q, k, vt, qi, kid, wi, km_blk, vmt_blk, kidm_blk, nb, seq, topk):
    m = q.shape[0]
    tq = _row_tile(seq, 128)
    tk = vt.shape[2]
    assert seq % tk == 0 and tq == LANES
    nt = seq // tq
    nkt = seq // tk
    kvw = DS_KV * DS_HD
    kvr = DS_KV * (DS_HD + ONES_ROWS)
    cur = lambda b, i: (b * nt + i, 0)
    per_b = lambda b, i: (b, 0)
    fixed = lambda b, i: (0, 0)
    return pl.pallas_call(
        functools.partial(_dsa_frames_kernel, tq=tq, tk=tk, topk=topk),
        grid=(nb, nt),
        in_specs=[pl.BlockSpec((tq, DS_WIDTH), cur),
                  pl.BlockSpec((seq, kvw), per_b),
                  pl.BlockSpec((nkt, kvr, tk), lambda b, i: (b, 0, 0)),
                  pl.BlockSpec((tq, IDX_HEADS * IDX_HD), cur),
                  pl.BlockSpec((seq, LANES), per_b),
                  pl.BlockSpec((tq, LANES), cur),
                  pl.BlockSpec((LANES, kvw), fixed), pl.BlockSpec((kvr, LANES), fixed),
                  pl.BlockSpec((LANES, LANES), fixed)],
        out_specs=pl.BlockSpec((tq, DS_WIDTH), cur),
        out_shape=jax.ShapeDtypeStruct((m, DS_WIDTH), _MXU),
        scratch_shapes=[pltpu.VMEM((LANES, tq), jnp.int32), pltpu.VMEM((nkt, tk, tq), jnp.int32),
                        pltpu.VMEM((LANES, tq), _F32), pltpu.VMEM((nkt, tk, tq), _F32),
                        pltpu.VMEM((IDX_HEADS * tq, LANES), _MXU),
                        pltpu.VMEM((LANES, tq), _F32),
                        pltpu.VMEM((DS_KV, DS_REP * tq, DS_HD), _MXU),
                        pltpu.VMEM((DS_KV, 1, DS_REP * tq), _F32),
                        pltpu.VMEM((DS_KV, DS_HD + ONES_ROWS, DS_REP * tq), _F32)],
        compiler_params=_cparams(("parallel", "arbitrary")),
        name="dsa_frames",
    )(q, k, vt, qi, kid, wi, km_blk, vmt_blk, kidm_blk)


def _dsa_small_kernel(*refs, has_cache, nq, n_new, topk):
    if has_cache:
        q_ref, kn_ref, vn_ref, qi_ref, kidn_ref, wi_ref, ck_ref, cv_ref, ckid_ref, o_ref = refs
    else:
        q_ref, kn_ref, vn_ref, qi_ref, kidn_ref, wi_ref, o_ref = refs
    qst = _stack_idx_heads(qi_ref[...])
    wi = wi_ref[...]
    wbs = [jnp.broadcast_to(wi[:, h:h + 1], (nq, LANES)) for h in range(IDX_HEADS)]

    def wb_tile(width):
        return lambda h: jnp.concatenate([wbs[h]] * (width // LANES), axis=1)

    col_n = lax.broadcasted_iota(jnp.int32, (nq, LANES), 1)
    keys = [_order_key(_index_scores(qst, wb_tile(LANES), kidn_ref[...], nq), col_n < n_new)]
    if has_cache:
        past = ckid_ref.shape[1]
        sc = _index_scores(qst, wb_tile(past), ckid_ref[0], nq)
        keys.append(_order_key(sc, jnp.full(sc.shape, True)))

    def count_ge(thr):
        cnt = 0.0
        for kk in keys:
            cnt = cnt + jnp.sum(jnp.where(kk >= thr, 1.0, 0.0), axis=-1, keepdims=True)
        return cnt

    def bit_body(it, u):
        bit = jnp.int32(31) - it
        cand = u | lax.shift_left(jnp.int32(1), bit)
        cnt = count_ge(cand ^ jnp.int32(INT_MIN))
        return jnp.where(cnt >= float(topk), cand, u)

    u = lax.fori_loop(0, 32, bit_body, jnp.zeros((nq, 1), jnp.int32))
    thr = jnp.maximum(u ^ jnp.int32(INT_MIN), jnp.int32(INT_MIN + 1))

    need = float(topk) - count_ge(thr + 1)
    r = lax.broadcasted_iota(jnp.int32, (LANES, LANES), 0)
    c = lax.broadcasted_iota(jnp.int32, (LANES, LANES), 1)
    before = jnp.where(r <= c, 1.0, 0.0).astype(_MXU)
    seen = jnp.zeros((nq, 1), _F32)
    biases = [None] * len(keys)
    for idx in reversed(range(len(keys))):
        kk = keys[idx]
        cols = []
        for c0 in range(0, kk.shape[1], LANES):
            blk = kk[:, c0:c0 + LANES]
            eq = blk == thr
            upto = _dot(jnp.where(eq, 1.0, 0.0).astype(_MXU), before)
            keep = jnp.logical_or(blk > thr, jnp.logical_and(eq, seen + upto <= need))
            cols.append(jnp.where(keep, 0.0, NEG_INF))
            seen = seen + upto[:, LANES - 1:LANES]
        biases[idx] = jnp.concatenate(cols, axis=1)

    for g in range(DS_KV):
        gl = slice(g * DS_HD, (g + 1) * DS_HD)
        qg = jnp.concatenate(
            [q_ref[:, (g * DS_REP + r) * DS_HD:(g * DS_REP + r + 1) * DS_HD] for r in range(DS_REP)], axis=0)
        ks, vs = [kn_ref[:, gl]], [vn_ref[:, gl]]
        if has_cache:
            ks.append(ck_ref[0][:, gl].astype(_MXU))
            vs.append(cv_ref[0][:, gl].astype(_MXU))
        ss = [_dot_nt(qg, kk) + jnp.concatenate([bb] * DS_REP, axis=0) for kk, bb in zip(ks, biases)]
        m = ss[0].max(axis=-1, keepdims=True)
        for s in ss[1:]:
            m = jnp.maximum(m, s.max(axis=-1, keepdims=True))
        l = 0.0
        acc = 0.0
        for s, vv in zip(ss, vs):
            p = jnp.exp2(s - m)
            l = l + jnp.sum(p, axis=-1, keepdims=True)
            acc = acc + _dot(p.astype(_MXU), vv)
        og = acc / l
        for r in range(DS_REP):
            lo = (g * DS_REP + r) * DS_HD
            o_ref[:, lo:lo + DS_HD] = og[r * nq:(r + 1) * nq].astype(o_ref.dtype)


def _dsa_small(q, kn_blk, vn_blk, qi, kidn_blk, wi, cache_k, cache_v, cache_kid, nb, nq, n_new, topk):
    kvw = DS_KV * DS_HD
    has_cache = cache_k is not None
    row = lambda b: (b, 0)
    in_specs = [pl.BlockSpec((nq, DS_WIDTH), row),
                pl.BlockSpec((LANES, kvw), row), pl.BlockSpec((LANES, kvw), row),
                pl.BlockSpec((nq, IDX_HEADS * IDX_HD), row),
                pl.BlockSpec((LANES, LANES), row),
                pl.BlockSpec((nq, LANES), row)]
    args = [q, kn_blk, vn_blk, qi, kidn_blk, wi]
    if has_cache:
        past = cache_k.shape[1]
        in_specs += [pl.BlockSpec((1, past, kvw), lambda b: (b, 0, 0)),
                     pl.BlockSpec((1, past, kvw), lambda b: (b, 0, 0)),
                     pl.BlockSpec((1, past, LANES), lambda b: (b, 0, 0))]
        args += [cache_k, cache_v, cache_kid]
    return pl.pallas_call(
        functools.partial(_dsa_small_kernel, has_cache=has_cache, nq=nq, n_new=n_new, topk=topk),
        grid=(nb,),
        in_specs=in_specs,
        out_specs=pl.BlockSpec((nq, DS_WIDTH), row),
        out_shape=jax.ShapeDtypeStruct((nb * nq, DS_WIDTH), _MXU),
        compiler_params=_cparams(("parallel",)),
        name="dsa_small",
    )(*args)


def _pad_rows_front(a, rows):
    return jnp.pad(a, ((rows - a.shape[0], 0), (0, 0)))


def _pad_rows_back(a, nb, nq, rows):
    w = a.shape[1]
    return jnp.pad(a.reshape(nb, nq, w), ((0, 0), (0, rows - nq), (0, 0))).reshape(nb * rows, w)


def _transpose_with_ones(v, group_cols):
    n = v.shape[0]
    g = v.shape[1] // group_cols
    vt = v.T.reshape(g, group_cols, n)
    pad = jnp.zeros((g, ONES_ROWS, n), v.dtype).at[:, 0, :].set(1)
    return jnp.concatenate([vt, pad], axis=1).reshape(g * (group_cols + ONES_ROWS), n)


def _dup_pairs(a, hd):
    lead = a.shape[:-1]
    g = a.shape[-1] // hd
    a = a.reshape(*lead, g, 1, hd)
    return jnp.broadcast_to(a, (*lead, g, 2, hd)).reshape(*lead, g * 2 * hd)


def kernel(x_prompt, x_sample, cache_l0_k, cache_l0_v, cache_l1_k, cache_l1_v, cache_l2_k, cache_l2_v, cache_l2_kidx, cache_l3_k, cache_l3_v, meta_tokens, l0_norm, l0_w_in, l0_w_out, l0_lam_q1, l0_lam_k1, l0_lam_q2, l0_lam_k2, l0_subln, l1_norm, l1_w_in, l1_w_out, l1_sinks, l2_norm, l2_w_in, l2_w_out, l3_norm, l3_w_in, l3_w_out, l3_lam_q1, l3_lam_k1, l3_lam_q2, l3_lam_k2, l3_subln, final_norm):
    nb, seq, d = x_prompt.shape
    db, ds, _ = x_sample.shape
    past = cache_l0_k.shape[1]
    sw_rows = cache_l1_k.shape[1]
    assert seq % CHUNK == 0 and meta_tokens.shape[0] == N_META and ds <= LANES

    xs = {"f": x_prompt.reshape(nb * seq, d), "m": meta_tokens.astype(_F32), "s": x_sample.reshape(db * ds, d)}
    pos = {"f": N_META + jnp.arange(seq), "m": jnp.arange(N_META), "s": past + jnp.arange(ds)}
    pos["s"] = jnp.tile(pos["s"], db)

    tabs128 = {n: _rope_tables(p, 128, 1.0) for n, p in pos.items()}
    tabs128_da = {n: _rope_tables(p, 128, DA_HD ** -0.5 * LOG2E) for n, p in pos.items()}
    tabs128_ds = {n: _rope_tables(p, 128, DS_HD ** -0.5 * LOG2E) for n, p in pos.items()}
    tabs64 = {n: _rope_tables(p, 64, 1.0) for n, p in pos.items()}
    tabs64_sw = {n: _rope_tables(p, 64, SW_HD ** -0.5 * LOG2E) for n, p in pos.items()}

    layers = [
        dict(norm=l0_norm, w_in=l0_w_in, w_out=l0_w_out, lam=(l0_lam_q1, l0_lam_k1, l0_lam_q2, l0_lam_k2),
             subln=l0_subln, cache=(cache_l0_k, cache_l0_v)),
        dict(norm=l1_norm, w_in=l1_w_in, w_out=l1_w_out, sinks=l1_sinks, cache=(cache_l1_k, cache_l1_v)),
        dict(norm=l2_norm, w_in=l2_w_in, w_out=l2_w_out, cache=(cache_l2_k, cache_l2_v, cache_l2_kidx)),
        dict(norm=l3_norm, w_in=l3_w_in, w_out=l3_w_out, lam=(l3_lam_q1, l3_lam_k1, l3_lam_q2, l3_lam_k2),
             subln=l3_subln, cache=(cache_l3_k, cache_l3_v)),
    ]
    norms = [lp["norm"] for lp in layers[1:]] + [final_norm]
    depth = len(layers)

    hs = {n: _rms_norm_rows(x, layers[0]["norm"]) for n, x in xs.items()}
    p_st, s_st = [], []
    ys = {}
    for li, lp in enumerate(layers):
        kind = li % N_MIXERS
        w_in = lp["w_in"].astype(_MXU)
        w_out = lp["w_out"].astype(_MXU)
        o, z = {}, {}
        if kind == 0:
            lam_init = 0.8 - 0.6 * math.exp(-0.3 * li)
            wq, wk, wv, wz = (w_in[:, i * DA_WIDTH:(i + 1) * DA_WIDTH] for i in range(4))
            lams = jnp.stack([l.astype(_F32) for l in lp["lam"]])
            subln = lp["subln"].reshape(1, 2 * DA_HD).astype(_F32)
            subln_col = jnp.broadcast_to(lp["subln"].astype(_F32)[:, None], (2 * DA_HD, LANES))
            hshape = (DA_HEADS, 2 * DA_HD)
            q, k32, k16, v32, v16 = {}, {}, {}, {}, {}
            for n in xs:
                (q[n],) = _project(hs[n], wq, [_MXU], *tabs128_da[n])
                k32[n], k16[n] = _project(hs[n], wk, [_F32, _MXU], *tabs128[n])
                if n == "f":
                    v32[n], vt_f = _project(hs[n], wv, [_F32], group_cols=2 * DA_HD)
                else:
                    v32[n], v16[n] = _project(hs[n], wv, [_F32, _MXU])
                (z[n],) = _project(hs[n], wz, [_MXU])
            o["f"] = _diff_frames(q["f"], k16["f"], vt_f, k16["m"], _transpose_with_ones(v16["m"], 2 * DA_HD),
                                  lams, subln_col, lam_init, nb, seq)
            o["m"] = _diff_small(q["m"], k16["m"], v16["m"], None, None, lams, subln, lam_init, 1, N_META)
            ck, cv = lp["cache"]
            o["s"] = _diff_small(q["s"], k16["s"], v16["s"], ck.reshape(db, past, DA_WIDTH),
                                 cv.reshape(db, past, DA_WIDTH), lams, subln, lam_init, db, ds)

            def with_meta(f, mrows):
                mm = jnp.broadcast_to(mrows.reshape(1, N_META, *hshape), (nb, N_META, *hshape))
                return jnp.concatenate([mm, f.reshape(nb, seq, *hshape)], axis=1)

            p_st.append([with_meta(k32["f"], k32["m"]), with_meta(v32["f"], v32["m"])])
            s_st.append([k32["s"].reshape(db, ds, *hshape), v32["s"].reshape(db, ds, *hshape)])
        elif kind == 1:
            kvd = SW_KV * SW_HD
            wq = w_in[:, :SW_WIDTH]
            wk = w_in[:, SW_WIDTH:SW_WIDTH + kvd]
            wv = w_in[:, SW_WIDTH + kvd:SW_WIDTH + 2 * kvd]
            wz = w_in[:, SW_WIDTH + 2 * kvd:]
            wkd, wvd = _dup_pairs(wk, SW_HD), _dup_pairs(wv, SW_HD)
            sink_rows = {}
            q, kv32, kd, vd = {}, {}, {}, {}
            sk = (lp["sinks"].astype(_F32) * LOG2E).reshape(SW_KV, SW_REP, 1, 1)
            sink_cols = jnp.broadcast_to(sk.reshape(SW_KV, 1, SW_REP, 1),
                                         (SW_KV, 1, SW_REP, 2 * CHUNK)).reshape(SW_KV, 1, SW_REP * 2 * CHUNK)
            for n, rows in (("m", N_META), ("s", ds)):
                sink_rows[n] = jnp.broadcast_to(sk, (SW_KV, SW_REP, rows, LANES)).reshape(SW_KV, SW_REP * rows, LANES)
            for n in xs:
                (q[n],) = _project(hs[n], wq, [_MXU], *tabs64_sw[n])
                (kd[n],) = _project(hs[n], wkd, [_MXU], *tabs64[n])
                (vd[n],) = _project(hs[n], wvd, [_MXU])
                (z[n],) = _project(hs[n], wz, [_MXU])
            kmeta_blk = _pad_rows_front(kd["m"], WINDOW)
            vmeta_blk = _pad_rows_front(vd["m"], WINDOW)
            o["f"] = _swa_frames(q["f"], kd["f"], vd["f"], kmeta_blk, vmeta_blk, sink_cols, nb, seq)
            o["m"] = _swa_small(q["m"], kd["m"], vd["m"], None, None, sink_rows["m"], 1, N_META)
            ck, cv = lp["cache"]
            ckd = _dup_pairs(ck.reshape(db, sw_rows, kvd), SW_HD).astype(_MXU)
            cvd = _dup_pairs(cv.reshape(db, sw_rows, kvd), SW_HD).astype(_MXU)
            o["s"] = _swa_small(q["s"], kd["s"], vd["s"], ckd, cvd, sink_rows["s"], db, ds)
            assert seq >= sw_rows and sw_rows % 8 == 0
            h_tail = hs["f"].reshape(nb, seq, d)[:, seq - sw_rows:].reshape(nb * sw_rows, d)
            tail_pos = N_META + seq - sw_rows + jnp.arange(sw_rows)
            tail_tabs, tail_shift = _rope_tables(tail_pos, 64, 1.0)
            (kt,) = _project(h_tail, _pad_cols(wk), [_F32], tail_tabs, tail_shift)
            (vt,) = _project(h_tail, _pad_cols(wv), [_F32])
            p_st.append([kt[:, :kvd].reshape(nb, sw_rows, SW_KV, SW_HD), vt[:, :kvd].reshape(nb, sw_rows, SW_KV, SW_HD)])
            (ks,) = _project(hs["s"], _pad_cols(wk), [_F32], *tabs64["s"])
            (vs_,) = _project(hs["s"], _pad_cols(wv), [_F32])
            ks = ks[:, :kvd].reshape(db, ds, SW_KV, SW_HD)
            vs_ = vs_[:, :kvd].reshape(db, ds, SW_KV, SW_HD)
            k_all = jnp.concatenate([ck, ks], axis=1)
            v_all = jnp.concatenate([cv, vs_], axis=1)
            s_st.append([k_all[:, k_all.shape[1] - sw_rows:], v_all[:, v_all.shape[1] - sw_rows:]])
        else:
            kvd = DS_KV * DS_HD
            c0 = 0
            wq = w_in[:, c0:c0 + DS_WIDTH]; c0 += DS_WIDTH
            wk = w_in[:, c0:c0 + kvd]; c0 += kvd
            wv = w_in[:, c0:c0 + kvd]; c0 += kvd
            wz = w_in[:, c0:c0 + DS_WIDTH]; c0 += DS_WIDTH
            wqi = w_in[:, c0:c0 + IDX_HEADS * IDX_HD]; c0 += IDX_HEADS * IDX_HD
            wki = w_in[:, c0:c0 + IDX_HD]; c0 += IDX_HD
            wwi = w_in[:, c0:c0 + IDX_HEADS]
            wkid = jnp.concatenate([wki, wki], axis=1)
            wwi_p = _pad_cols(wwi)
            wi_scale = IDX_HEADS ** -0.5 * IDX_HD ** -0.5
            q, k32, k16, v32, v16, qi, kid32, kid16, wi = {}, {}, {}, {}, {}, {}, {}, {}, {}
            for n in xs:
                (q[n],) = _project(hs[n], wq, [_MXU], *tabs128_ds[n])
                k32[n], k16[n] = _project(hs[n], wk, [_F32, _MXU], *tabs128[n])
                if n == "f":
                    v32[n], vt_f = _project(hs[n], wv, [_F32], group_cols=DS_HD)
                else:
                    v32[n], v16[n] = _project(hs[n], wv, [_F32, _MXU])
                (z[n],) = _project(hs[n], wz, [_MXU])
                (qi[n],) = _project(hs[n], wqi, [_MXU], *tabs64[n])
                kid32[n], kid16[n] = _project(hs[n], wkid, [_F32, _MXU], *tabs64[n])
                (wi[n],) = _project(hs[n], wwi_p, [_F32], scale=wi_scale)
            topk_p = min(TOPK_MAX, seq // 4)
            topk_s = min(TOPK_MAX, (past + ds) // 4)
            km_blk = _pad_rows_back(k16["m"], 1, N_META, LANES)
            vm_blk = _pad_rows_back(v16["m"], 1, N_META, LANES)
            kidm_blk = _pad_rows_back(kid16["m"], 1, N_META, LANES)
            o["f"] = _dsa_frames(q["f"], k16["f"], vt_f, qi["f"], kid16["f"], wi["f"],
                                 km_blk, _transpose_with_ones(vm_blk, DS_HD), kidm_blk, nb, seq, topk_p)
            o["m"] = _dsa_small(q["m"], km_blk, vm_blk, qi["m"], kidm_blk, wi["m"], None, None, None,
                                1, N_META, N_META, topk_p)
            ck, cv, cki = lp["cache"]
            ckid = jnp.concatenate([cki, cki], axis=-1).astype(_MXU)
            o["s"] = _dsa_small(q["s"], _pad_rows_back(k16["s"], db, ds, LANES), _pad_rows_back(v16["s"], db, ds, LANES),
                                qi["s"], _pad_rows_back(kid16["s"], db, ds, LANES), wi["s"],
                                ck.reshape(db, past, kvd), cv.reshape(db, past, kvd), ckid,
                                db, ds, ds, topk_s)

            def with_meta(f, mrows, *tail):
                mm = jnp.broadcast_to(mrows.reshape(1, N_META, *tail), (nb, N_META, *tail))
                return jnp.concatenate([mm, f.reshape(nb, seq, *tail)], axis=1)

            p_st.append([with_meta(k32["f"], k32["m"], DS_KV, DS_HD), with_meta(v32["f"], v32["m"], DS_KV, DS_HD),
                         with_meta(kid32["f"][:, :IDX_HD], kid32["m"][:, :IDX_HD], IDX_HD)])
            s_st.append([k32["s"].reshape(db, ds, DS_KV, DS_HD), v32["s"].reshape(db, ds, DS_KV, DS_HD),
                         kid32["s"][:, :IDX_HD].reshape(db, ds, IDX_HD)])
        last = li == depth - 1
        for n in xs:
            res = _out_project(o[n], z[n], w_out, xs[n], norms[li], last)
            if last:
                (ys[n],) = res
            else:
                xs[n], hs[n] = res

    y_prompt = ys["f"].reshape(nb, seq, d)
    y_sample = ys["s"].reshape(db, ds, d)
    return (y_prompt, y_sample,
            p_st[0][0], p_st[0][1], s_st[0][0], s_st[0][1],
            p_st[1][0], p_st[1][1], s_st[1][0], s_st[1][1],
            p_st[2][0], p_st[2][1], p_st[2][2], s_st[2][0], s_st[2][1], s_st[2][2],
            p_st[3][0], p_st[3][1], s_st[3][0], s_st[3][1])


def _pad_cols(w):
    n = w.shape[1]
    return jnp.pad(w, ((0, 0), (0, (-n) % LANES)))
```

```python
import functools
import math

import jax
import jax.numpy as jnp
from jax import lax
from jax.experimental import pallas as pl
from jax.experimental.pallas import tpu as pltpu

CHUNK = 64
N_META = 16
ROPE_THETA = 500000.0
NORM_EPS = 1e-6
NEG_INF = -1e30
DA_HEADS = 8
DA_HD = 128
DA_WIDTH = DA_HEADS * 2 * DA_HD
SW_HEADS = 32
SW_KV = 4
SW_HD = 64
SW_WIDTH = SW_HEADS * SW_HD
SW_REP = SW_HEADS // SW_KV
WINDOW = 128
DS_HEADS = 16
DS_KV = 4
DS_HD = 128
DS_WIDTH = DS_HEADS * DS_HD
DS_REP = DS_HEADS // DS_KV
IDX_HEADS = 16
IDX_HD = 64
TOPK_MAX = 256
N_MIXERS = 3

LANES = 128
INT_MIN = -(2 ** 31)
LOG2E = math.log2(math.e)
ONES_ROWS = 16
VMEM_LIMIT = 56 * 1024 * 1024

_MXU = jnp.bfloat16
_F32 = jnp.float32

_NT = (((1,), (1,)), ((), ()))


def _cparams(sem):
    return pltpu.CompilerParams(dimension_semantics=sem, vmem_limit_bytes=VMEM_LIMIT)


def _row_tile(m, pref):
    t = pref
    while t > 8 and m % t:
        t //= 2
    assert m % t == 0, (m, pref)
    return t


def _dot(a, b):
    return jnp.dot(a, b, preferred_element_type=_F32)


def _dot_nt(a, b):
    return lax.dot_general(a, b, _NT, preferred_element_type=_F32)


def _norm_kernel(x_ref, g_ref, h_ref):
    x = x_ref[...]
    y = x * lax.rsqrt(jnp.mean(x * x, axis=-1, keepdims=True) + NORM_EPS)
    h_ref[...] = (y * g_ref[...]).astype(h_ref.dtype)


def _rms_norm_rows(x, g):
    m, d = x.shape
    out_dtype = _MXU
    tm = _row_tile(m, 512)
    return pl.pallas_call(
        _norm_kernel,
        grid=(m // tm,),
        in_specs=[pl.BlockSpec((tm, d), lambda i: (i, 0)),
                  pl.BlockSpec((1, d), lambda i: (0, 0))],
        out_specs=pl.BlockSpec((tm, d), lambda i: (i, 0)),
        out_shape=jax.ShapeDtypeStruct((m, d), out_dtype),
        compiler_params=_cparams(("parallel",)),
        name="rms_norm",
    )(x, g.reshape(1, d).astype(_F32))


def _proj_kernel(*refs, rope_shift, scale, n_out, tn, group_cols):
    h_ref, w_ref = refs[0], refs[1]
    pos = 2
    if rope_shift is not None:
        c_ref, a_ref, b_ref = refs[2:5]
        pos = 5
    outs = refs[pos:pos + n_out]
    ot_ref = refs[pos + n_out] if group_cols else None
    if ot_ref is not None:
        tm = ot_ref.shape[2]
        row = lax.broadcasted_iota(jnp.int32, (ONES_ROWS, tm), 0)
        ones_blk = jnp.where(row == 0, 1.0, 0.0).astype(ot_ref.dtype)
        for g in range(tn // group_cols):
            base = g * (group_cols + ONES_ROWS) + group_cols
            ot_ref[0, base:base + ONES_ROWS, :] = ones_blk
    h = h_ref[...]
    cw = 2 * LANES if tn % (2 * LANES) == 0 else LANES
    for c in range(tn // cw):
        wide = _dot(h, w_ref[:, c * cw:(c + 1) * cw])
        for sub in range(cw // LANES):
            acc = wide[:, sub * LANES:(sub + 1) * LANES]
            if rope_shift is not None:
                acc = (acc * c_ref[...]
                       + pltpu.roll(acc, rope_shift, 1) * a_ref[...]
                       + pltpu.roll(acc, LANES - rope_shift, 1) * b_ref[...])
            elif scale is not None:
                acc = acc * scale
            cols = slice(c * cw + sub * LANES, c * cw + (sub + 1) * LANES)
            for o in outs:
                o[:, cols] = acc.astype(o.dtype)
            if ot_ref is not None:
                lo = cols.start
                r0 = (lo // group_cols) * (group_cols + ONES_ROWS) + lo % group_cols
                ot_ref[0, r0:r0 + LANES, :] = acc.T.astype(ot_ref.dtype)


def _project(h, w, out_dtypes, tabs=None, rope_shift=None, scale=None, group_cols=0):
    m, d = h.shape
    n = w.shape[1]
    tm = _row_tile(m if tabs is None else math.gcd(m, tabs[0].shape[0]), 512)
    tn = n if n <= 2048 else 2048
    assert n % tn == 0 and tn % LANES == 0
    in_specs = [pl.BlockSpec((tm, d), lambda j, i: (i, 0)),
                pl.BlockSpec((d, tn), lambda j, i: (0, j))]
    args = [h, w]
    if tabs is not None:
        trows = tabs[0].shape[0]
        assert trows % tm == 0
        nrep = trows // tm
        for t in tabs:
            in_specs.append(pl.BlockSpec((tm, LANES), lambda j, i: (i % nrep, 0)))
            args.append(t)
    else:
        rope_shift = None
    out_specs = [pl.BlockSpec((tm, tn), lambda j, i: (i, j)) for _ in out_dtypes]
    out_shape = [jax.ShapeDtypeStruct((m, n), dt) for dt in out_dtypes]
    if group_cols:
        assert tn % group_cols == 0 and group_cols % LANES == 0
        grow = group_cols + ONES_ROWS
        out_specs.append(pl.BlockSpec((1, tn // group_cols * grow, tm), lambda j, i: (i, j, 0)))
        out_shape.append(jax.ShapeDtypeStruct((m // tm, n // group_cols * grow, tm), _MXU))
    outs = pl.pallas_call(
        functools.partial(_proj_kernel, rope_shift=rope_shift, scale=scale,
                          n_out=len(out_dtypes), tn=tn, group_cols=group_cols),
        grid=(n // tn, m // tm),
        in_specs=in_specs,
        out_specs=out_specs,
        out_shape=out_shape,
        compiler_params=_cparams(("parallel", "parallel")),
        name="in_proj",
    )(*args)
    return outs


def _outproj_kernel(o_ref, z_ref, w_ref, x_ref, g_ref, *outs, emit_x):
    z = z_ref[...].astype(_F32)
    a = o_ref[...].astype(_F32) * (z / (1.0 + jnp.exp(-z)))
    y = x_ref[...] + _dot(a.astype(w_ref.dtype), w_ref[...])
    if emit_x:
        outs[0][...] = y
    n = y * lax.rsqrt(jnp.mean(y * y, axis=-1, keepdims=True) + NORM_EPS) * g_ref[...]
    outs[-1][...] = n.astype(outs[-1].dtype)


def _out_project(o, z, w, x, g_next, last):
    m, d = x.shape
    wd = o.shape[1]
    tm = _row_tile(m, 512)
    row = lambda i: (i, 0)
    fixed = lambda i: (0, 0)
    if last:
        out_shape = [jax.ShapeDtypeStruct((m, d), _F32)]
    else:
        out_shape = [jax.ShapeDtypeStruct((m, d), _F32), jax.ShapeDtypeStruct((m, d), _MXU)]
    return pl.pallas_call(
        functools.partial(_outproj_kernel, emit_x=not last),
        grid=(m // tm,),
        in_specs=[pl.BlockSpec((tm, wd), row), pl.BlockSpec((tm, wd), row),
                  pl.BlockSpec((wd, d), fixed), pl.BlockSpec((tm, d), row),
                  pl.BlockSpec((1, d), fixed)],
        out_specs=[pl.BlockSpec((tm, d), row) for _ in out_shape],
        out_shape=out_shape,
        compiler_params=_cparams(("parallel",)),
        name="out_proj",
    )(o, z, w, x, g_next.reshape(1, d).astype(_F32))


def _rope_tables(pos, hd, scale=1.0):
    rd = hd // 4
    half = rd // 2
    lane = jnp.arange(LANES)
    d = lane % hd
    f = d % half
    inv_freq = ROPE_THETA ** (-(jnp.arange(half, dtype=_F32) * 2.0 / rd))
    ang = pos.astype(_F32)[:, None] * inv_freq[f][None, :]
    cos, sin = jnp.cos(ang), jnp.sin(ang)
    in_rot = (d < rd)[None, :]
    c = jnp.where(in_rot, cos, 1.0)
    a = jnp.where(((d >= half) & (d < rd))[None, :], sin, 0.0)
    b = jnp.where((d < half)[None, :], -sin, 0.0)
    s = jnp.float32(scale)
    return (c * s, a * s, b * s), half


def _lam_value(lam_ref, lam_init):
    l = lam_ref[...]
    s1 = jnp.sum(l[0:1] * l[1:2], axis=-1, keepdims=True)
    s2 = jnp.sum(l[2:3] * l[3:4], axis=-1, keepdims=True)
    return jnp.exp(s1) - jnp.exp(s2) + lam_init


def _diff_finish(acc1, l1, acc2, l2, lam, sub, lam_init):
    o = acc1 / l1 - lam * (acc2 / l2)
    o = o * lax.rsqrt(jnp.mean(o * o, axis=-1, keepdims=True) + NORM_EPS) * sub
    return o * (1.0 - lam_init)


def _diff_frames_kernel(q_ref, k_ref, vt_ref, km_ref, vmt_ref, lam_ref, subc_ref, o_ref,
                        acc_s, sa_s, sb_s, pa_s, pb_s, *, tq, lam_init):
    i = pl.program_id(2)
    hw = 2 * DA_HD

    def q_half(j):
        return q_ref[:, j * DA_HD:(j + 1) * DA_HD]

    km = km_ref[...]
    vmt = vmt_ref[...]
    ms = []
    for j in range(2):
        s = _dot_nt(km[:, j * DA_HD:(j + 1) * DA_HD], q_half(j))
        m = jnp.max(s, axis=0, keepdims=True)
        ms.append(m)
        acc_s[j] = _dot(vmt, jnp.exp2(s - m).astype(vmt.dtype))

    def scores_into(dst, t):
        start = pl.multiple_of(t * tq, tq)
        for j in range(2):
            dst[j] = _dot_nt(k_ref[pl.ds(start, tq), j * DA_HD:(j + 1) * DA_HD], q_half(j))

    def value_update(t, a, p_rd):
        vt = vt_ref[t]
        for j in range(2):
            acc_s[j] = a[j] * acc_s[j] + _dot(vt, p_rd[j])

    def tile_step(t, carry, src, dst, p_rd, p_wr, masked, has_next):
        m, a = carry
        if masked:
            kc = lax.broadcasted_iota(jnp.int32, (tq, tq), 0) // CHUNK
            qc = lax.broadcasted_iota(jnp.int32, (tq, tq), 1) // CHUNK
            ok = kc <= qc
        m_out, a_out = [], []
        for j in range(2):
            s = src[j]
            if masked:
                s = jnp.where(ok, s, NEG_INF)
            m_new = jnp.maximum(m[j], jnp.max(s, axis=0, keepdims=True))
            a_out.append(jnp.exp2(m[j] - m_new))
            m_out.append(m_new)
            p_wr[j] = jnp.exp2(s - m_new).astype(p_wr.dtype)
        value_update(jnp.maximum(t - 1, 0), a, p_rd)
        if has_next:
            scores_into(dst, t + 1)
        return tuple(m_out), tuple(a_out)

    scores_into(sa_s, 0)
    pb_s[...] = jnp.zeros_like(pb_s)
    one = jnp.ones((1, tq), _F32)
    carry = ((ms[0], ms[1]), (one, one))

    def pair_body(u, carry):
        carry = tile_step(2 * u, carry, sa_s, sb_s, pb_s, pa_s, False, True)
        return tile_step(2 * u + 1, carry, sb_s, sa_s, pa_s, pb_s, False, True)

    carry = lax.fori_loop(0, i // 2, pair_body, carry)

    @pl.when(i % 2 == 0)
    def _():
        _, a = tile_step(i, carry, sa_s, sb_s, pb_s, pa_s, True, False)
        value_update(i, a, pa_s)

    @pl.when(i % 2 == 1)
    def _():
        c = tile_step(i - 1, carry, sa_s, sb_s, pb_s, pa_s, False, True)
        _, a = tile_step(i, c, sb_s, sa_s, pa_s, pb_s, True, False)
        value_update(i, a, pb_s)

    acc0 = acc_s[0]
    acc1 = acc_s[1]
    lam = _lam_value(lam_ref, lam_init)
    inv0 = 1.0 / acc0[hw:hw + 1]
    inv1 = lam / acc1[hw:hw + 1]
    ot = acc0[:hw] * inv0 - acc1[:hw] * inv1
    sub = jnp.concatenate([subc_ref[...]] * (tq // LANES), axis=1)
    ot = ot * lax.rsqrt(jnp.mean(ot * ot, axis=0, keepdims=True) + NORM_EPS) * sub * (1.0 - lam_init)
    o_ref[...] = ot.T.astype(o_ref.dtype)


def _diff_frames(q, k, vt, km, vmt, lams, subln_col, lam_init, nb, seq):
    m = q.shape[0]
    hw = 2 * DA_HD
    hr = hw + ONES_ROWS
    tq = vt.shape[2]
    assert seq % tq == 0 and tq % LANES == 0
    nt = seq // tq
    return pl.pallas_call(
        functools.partial(_diff_frames_kernel, tq=tq, lam_init=lam_init),
        grid=(nb, DA_HEADS, nt),
        in_specs=[pl.BlockSpec((tq, hw), lambda b, h, i: (b * nt + i, h)),
                  pl.BlockSpec((seq, hw), lambda b, h, i: (b, h)),
                  pl.BlockSpec((nt, hr, tq), lambda b, h, i: (b, h, 0)),
                  pl.BlockSpec((N_META, hw), lambda b, h, i: (0, h)),
                  pl.BlockSpec((hr, N_META), lambda b, h, i: (h, 0)),
                  pl.BlockSpec((4, DA_HD), lambda b, h, i: (0, 0)),
                  pl.BlockSpec((hw, LANES), lambda b, h, i: (0, 0))],
        out_specs=pl.BlockSpec((tq, hw), lambda b, h, i: (b * nt + i, h)),
        out_shape=jax.ShapeDtypeStruct((m, DA_WIDTH), _MXU),
        scratch_shapes=[pltpu.VMEM((2, hr, tq), _F32),
                        pltpu.VMEM((2, tq, tq), _F32), pltpu.VMEM((2, tq, tq), _F32),
                        pltpu.VMEM((2, tq, tq), _MXU), pltpu.VMEM((2, tq, tq), _MXU)],
        compiler_params=_cparams(("parallel", "parallel", "arbitrary")),
        name="diff_frames",
    )(q, k, vt, km, vmt, lams, subln_col)


def _diff_small_kernel(*refs, has_cache, lam_init):
    if has_cache:
        q_ref, k_ref, v_ref, ck_ref, cv_ref, lam_ref, sub_ref, o_ref = refs
    else:
        q_ref, k_ref, v_ref, lam_ref, sub_ref, o_ref = refs
    q = q_ref[...]
    kn = k_ref[...]
    vn = v_ref[...]
    parts = [(kn, vn)]
    if has_cache:
        parts.append((ck_ref[0].astype(kn.dtype), cv_ref[0].astype(vn.dtype)))
    accs, ls = [], []
    for j in range(2):
        qj = q[:, j * DA_HD:(j + 1) * DA_HD]
        ss = [_dot_nt(qj, kp[:, j * DA_HD:(j + 1) * DA_HD]) for kp, _ in parts]
        m = ss[0].max(axis=-1, keepdims=True)
        for s in ss[1:]:
            m = jnp.maximum(m, s.max(axis=-1, keepdims=True))
        l = 0.0
        acc = 0.0
        for s, (_, vp) in zip(ss, parts):
            p = jnp.exp2(s - m)
            l = l + jnp.sum(p, axis=-1, keepdims=True)
            acc = acc + _dot(p.astype(vp.dtype), vp)
        accs.append(acc)
        ls.append(l)
    lam = _lam_value(lam_ref, lam_init)
    o = _diff_finish(accs[0], ls[0], accs[1], ls[1], lam, sub_ref[...], lam_init)
    o_ref[...] = o.astype(o_ref.dtype)


def _diff_small(q, k, v, cache_k, cache_v, lams, subln, lam_init, nb, nq):
    hw = 2 * DA_HD
    has_cache = cache_k is not None
    qspec = pl.BlockSpec((nq, hw), lambda b, h: (b, h))
    in_specs = [qspec, qspec, qspec]
    args = [q, k, v]
    if has_cache:
        past = cache_k.shape[1]
        cspec = pl.BlockSpec((1, past, hw), lambda b, h: (b, 0, h))
        in_specs += [cspec, cspec]
        args += [cache_k, cache_v]
    in_specs += [pl.BlockSpec((4, DA_HD), lambda b, h: (0, 0)),
                 pl.BlockSpec((1, hw), lambda b, h: (0, 0))]
    args += [lams, subln]
    return pl.pallas_call(
        functools.partial(_diff_small_kernel, has_cache=has_cache, lam_init=lam_init),
        grid=(nb, DA_HEADS),
        in_specs=in_specs,
        out_specs=qspec,
        out_shape=jax.ShapeDtypeStruct((nb * nq, DA_WIDTH), _MXU),
        compiler_params=_cparams(("parallel", "parallel")),
        name="diff_small",
    )(*args)


def _stack_heads(qg, nheads):
    n = qg.shape[0]
    lane = lax.broadcasted_iota(jnp.int32, (n, LANES), 1)
    lo = lane < SW_HD
    pieces = []
    for r in range(nheads):
        blk = qg[:, (r // 2) * LANES:(r // 2 + 1) * LANES]
        keep = lo if r % 2 == 0 else jnp.logical_not(lo)
        pieces.append(jnp.where(keep, blk, jnp.zeros_like(blk)))
    return jnp.concatenate(pieces, axis=0)


def _unstack_heads(o, n, nheads):
    lane = lax.broadcasted_iota(jnp.int32, (n, LANES), 1)
    lo = lane < SW_HD
    return [jnp.where(lo, o[(2 * p) * n:(2 * p + 1) * n], o[(2 * p + 1) * n:(2 * p + 2) * n])
            for p in range(nheads // 2)]


def _sink_softmax_pv(ss, vs, sk):
    m = sk
    for s in ss:
        m = jnp.maximum(m, jnp.max(s, axis=-1, keepdims=True))
    den = jnp.exp2(sk - m)
    acc = 0.0
    for s, v in zip(ss, vs):
        e = jnp.exp2(s - m)
        den = den + jnp.sum(e, axis=-1, keepdims=True)
        acc = acc + _dot(e.astype(v.dtype), v)
    return acc / den


def _swa_frames_kernel(q_ref, kc_ref, vc_ref, kp_ref, vp_ref, km_ref, vm_ref, sk_ref, o_ref,
                       kcat, vcat_t, bias_s, *, tq):
    i = pl.program_id(1)
    first = i == 0
    prev_rows = WINDOW
    pair = 2 * CHUNK
    kwin = 2 * pair
    ncol = SW_REP * pair
    kcat[0:prev_rows] = jnp.where(first, km_ref[...], kp_ref[...])
    kcat[prev_rows:prev_rows + tq] = kc_ref[...]
    vprev = jnp.where(first, vm_ref[...], vp_ref[...])
    vcat_t[:, 0:prev_rows] = vprev.astype(_F32).T.astype(vcat_t.dtype)
    vcat_t[:, prev_rows:prev_rows + tq] = vc_ref[...].astype(_F32).T.astype(vcat_t.dtype)

    krow = lax.broadcasted_iota(jnp.int32, (kwin, ncol), 0)
    second = (lax.broadcasted_iota(jnp.int32, (kwin, ncol), 1) % pair) >= CHUNK
    lo = jnp.where(second, CHUNK, 0)
    hi = jnp.where(second, kwin, kwin - CHUNK)
    bias_s[0] = jnp.where(jnp.logical_and(krow >= lo, krow < hi), 0.0, NEG_INF)
    lo_first = jnp.maximum(lo, jnp.where(first, prev_rows - N_META, 0))
    bias_s[1] = jnp.where(jnp.logical_and(krow >= lo_first, krow < hi), 0.0, NEG_INF)

    for k in range(tq // pair):
        rows = slice(k * pair, (k + 1) * pair)
        win = slice(k * pair, k * pair + kwin)
        bias = bias_s[1 if k == 0 else 0]
        for g in range(SW_KV):
            gl = slice(g * LANES, (g + 1) * LANES)
            qst = _stack_heads(q_ref[rows, g * SW_REP * SW_HD:(g + 1) * SW_REP * SW_HD], SW_REP)
            s = _dot_nt(kcat[win, gl], qst) + bias
            sk = sk_ref[g]
            m = jnp.maximum(jnp.max(s, axis=0, keepdims=True), sk)
            e = jnp.exp2(s - m)
            den = jnp.sum(e, axis=0, keepdims=True) + jnp.exp2(sk - m)
            ot = _dot(vcat_t[gl, win], e.astype(vcat_t.dtype)) * (1.0 / den)
            o = ot.T
            for p, blk in enumerate(_unstack_heads(o, pair, SW_REP)):
                lo = g * SW_REP * SW_HD + p * LANES
                o_ref[rows, lo:lo + LANES] = blk.astype(o_ref.dtype)


def _swa_frames(q, kd, vd, kmeta_blk, vmeta_blk, sink_cols, nb, seq):
    m = q.shape[0]
    tq = _row_tile(seq, 512)
    assert tq % (2 * CHUNK) == 0 and WINDOW == 2 * CHUNK
    nt = seq // tq
    pw = SW_KV * LANES
    per = tq // WINDOW
    cur = lambda b, i: (b * nt + i, 0)
    prev = lambda b, i: (jnp.maximum((b * nt + i) * per - 1, 0), 0)
    fixed = lambda b, i: (0, 0)
    return pl.pallas_call(
        functools.partial(_swa_frames_kernel, tq=tq),
        grid=(nb, nt),
        in_specs=[pl.BlockSpec((tq, SW_WIDTH), cur),
                  pl.BlockSpec((tq, pw), cur), pl.BlockSpec((tq, pw), cur),
                  pl.BlockSpec((WINDOW, pw), prev), pl.BlockSpec((WINDOW, pw), prev),
                  pl.BlockSpec((WINDOW, pw), fixed), pl.BlockSpec((WINDOW, pw), fixed),
                  pl.BlockSpec((SW_KV, 1, SW_REP * 2 * CHUNK), lambda b, i: (0, 0, 0))],
        out_specs=pl.BlockSpec((tq, SW_WIDTH), cur),
        out_shape=jax.ShapeDtypeStruct((m, SW_WIDTH), _MXU),
        scratch_shapes=[pltpu.VMEM((WINDOW + tq, pw), _MXU), pltpu.VMEM((pw, WINDOW + tq), _MXU),
                        pltpu.VMEM((2, 4 * CHUNK, SW_REP * 2 * CHUNK), _F32)],
        compiler_params=_cparams(("parallel", "arbitrary")),
        name="swa_frames",
    )(q, kd, vd, kd, vd, kmeta_blk, vmeta_blk, sink_cols)


def _swa_small_kernel(*refs, has_cache, nq):
    if has_cache:
        q_ref, k_ref, v_ref, ck_ref, cv_ref, sk_ref, o_ref = refs
    else:
        q_ref, k_ref, v_ref, sk_ref, o_ref = refs
    for g in range(SW_KV):
        gl = slice(g * LANES, (g + 1) * LANES)
        qst = _stack_heads(q_ref[:, g * SW_REP * SW_HD:(g + 1) * SW_REP * SW_HD], SW_REP)
        ks, vs = [k_ref[:, gl]], [v_ref[:, gl]]
        if has_cache:
            ks.append(ck_ref[0][:, gl])
            vs.append(cv_ref[0][:, gl])
        ss = [_dot_nt(qst, kk) for kk in ks]
        o = _sink_softmax_pv(ss, vs, sk_ref[g][:, 0:1])
        for p, blk in enumerate(_unstack_heads(o, nq, SW_REP)):
            lo = g * SW_REP * SW_HD + p * LANES
            o_ref[:, lo:lo + LANES] = blk.astype(o_ref.dtype)


def _swa_small(q, kd, vd, cache_kd, cache_vd, sink_rows, nb, nq):
    pw = SW_KV * LANES
    has_cache = cache_kd is not None
    row = lambda b: (b, 0)
    in_specs = [pl.BlockSpec((nq, SW_WIDTH), row), pl.BlockSpec((nq, pw), row), pl.BlockSpec((nq, pw), row)]
    args = [q, kd, vd]
    if has_cache:
        past = cache_kd.shape[1]
        cspec = pl.BlockSpec((1, past, pw), lambda b: (b, 0, 0))
        in_specs += [cspec, cspec]
        args += [cache_kd, cache_vd]
    in_specs.append(pl.BlockSpec((SW_KV, SW_REP * nq, LANES), lambda b: (0, 0, 0)))
    args.append(sink_rows)
    return pl.pallas_call(
        functools.partial(_swa_small_kernel, has_cache=has_cache, nq=nq),
        grid=(nb,),
        in_specs=in_specs,
        out_specs=pl.BlockSpec((nq, SW_WIDTH), row),
        out_shape=jax.ShapeDtypeStruct((nb * nq, SW_WIDTH), _MXU),
        compiler_params=_cparams(("parallel",)),
        name="swa_small",
    )(*args)


def _order_key(score, valid):
    bits = lax.bitcast_convert_type(score + 0.0, jnp.int32)
    key = jnp.where(bits < 0, bits ^ jnp.int32(0x7FFFFFFF), bits)
    return key if valid is None else jnp.where(valid, key, jnp.int32(INT_MIN))


def _stack_idx_heads(qi):
    return _stack_heads(qi, IDX_HEADS)


def _index_scores(qst, wb_of, kid, n):
    score = None
    hb = 4
    for h0 in range(0, IDX_HEADS, hb):
        sc = _dot_nt(qst[h0 * n:(h0 + hb) * n], kid)
        for h in range(h0, h0 + hb):
            part = jnp.maximum(sc[(h - h0) * n:(h - h0 + 1) * n], 0.0) * wb_of(h)
            score = part if score is None else score + part
    return score


def _index_scores_t(kid, qst, wit, n):
    score = None
    hb = 4
    for h0 in range(0, IDX_HEADS, hb):
        sc = _dot_nt(kid, qst[h0 * n:(h0 + hb) * n])
        for h in range(h0, h0 + hb):
            part = jnp.maximum(sc[:, (h - h0) * n:(h - h0 + 1) * n], 0.0) * wit[h:h + 1, :]
            score = part if score is None else score + part
    return score


def _dsa_frames_kernel(q_ref, k_ref, vt_ref, qi_ref, kid_ref, wi_ref, km_ref, vmt_ref, kidm_ref, o_ref,
                       keym_s, key_s, biasm_s, bias_s, qst_s, wit_s, qg_s, m_s, acc_s, *, tq, tk, topk):
    i = pl.program_id(1)
    n_t = (i * tq + tq + tk - 1) // tk
    qst_s[...] = _stack_idx_heads(qi_ref[...])
    wit_s[...] = wi_ref[...].T
    sc_m = _index_scores_t(kidm_ref[...], qst_s[...], wit_s[...], tq)
    row_m = lax.broadcasted_iota(jnp.int32, (LANES, tq), 0)
    keym_s[...] = _order_key(sc_m, row_m < N_META)

    q_chunk = (lax.broadcasted_iota(jnp.int32, (tk, tq), 1) + i * tq) // CHUNK

    def score_body(t, carry, masked):
        start = pl.multiple_of(t * tk, tk)
        sc = _index_scores_t(kid_ref[pl.ds(start, tk), :], qst_s[...], wit_s[...], tq)
        if masked:
            k_chunk = (lax.broadcasted_iota(jnp.int32, (tk, tq), 0) + t * tk) // CHUNK
            key_s[t] = _order_key(sc, k_chunk <= q_chunk)
        else:
            key_s[t] = _order_key(sc, None)
        return carry

    n_full = (i * tq) // tk
    lax.fori_loop(0, n_full, functools.partial(score_body, masked=False), 0)
    lax.fori_loop(n_full, n_t, functools.partial(score_body, masked=True), 0)

    def count_ge(thr):
        def fold(x):
            return jnp.sum(x.reshape(x.shape[0] // 64, 64, tq), axis=0)

        part = fold(jnp.where(keym_s[...] >= thr, 1.0, 0.0))

        def cbody(t, acc):
            return acc + fold(jnp.where(key_s[t] >= thr, 1.0, 0.0))

        part = lax.fori_loop(0, n_t, cbody, part)
        return jnp.sum(part, axis=0, keepdims=True)

    def bit_body(it, carry):
        u, cnt_u = carry
        bit = jnp.int32(31) - it
        cand = u | lax.shift_left(jnp.int32(1), bit)
        cnt = count_ge(cand ^ jnp.int32(INT_MIN))
        ok = cnt >= float(topk)
        return jnp.where(ok, cand, u), jnp.where(ok, cnt, cnt_u)

    u, cnt_u = lax.fori_loop(0, 32, bit_body, (jnp.zeros((1, tq), jnp.int32), jnp.zeros((1, tq), _F32)))
    thr = jnp.maximum(u ^ jnp.int32(INT_MIN), jnp.int32(INT_MIN + 1))

    biasm_s[...] = jnp.where(keym_s[...] >= thr, 0.0, NEG_INF)

    def bias_body(t, carry):
        bias_s[t] = jnp.where(key_s[t] >= thr, 0.0, NEG_INF)
        return carry

    lax.fori_loop(0, n_t, bias_body, 0)

    @pl.when(jnp.max(cnt_u) > float(topk))
    def _():
        need = float(topk) - count_ge(thr + 1)

        def tie_bias(key, seen):
            rows = key.shape[0]
            eq = key == thr
            r = lax.broadcasted_iota(jnp.int32, (rows, rows), 0)
            c = lax.broadcasted_iota(jnp.int32, (rows, rows), 1)
            upto = _dot(jnp.where(c <= r, 1.0, 0.0).astype(_MXU), jnp.where(eq, 1.0, 0.0).astype(_MXU))
            keep = jnp.logical_or(key > thr, jnp.logical_and(eq, seen + upto <= need))
            return jnp.where(keep, 0.0, NEG_INF), seen + upto[rows - 1:rows]

        biasm_s[...], seen = tie_bias(keym_s[...], jnp.zeros((1, tq), _F32))

        def tie_body(t, seen):
            bias_s[t], seen = tie_bias(key_s[t], seen)
            return seen

        lax.fori_loop(0, n_t, tie_body, seen)

    vr = DS_HD + ONES_ROWS
    groups = [(g, slice(g * DS_HD, (g + 1) * DS_HD), slice(g * vr, (g + 1) * vr)) for g in range(DS_KV)]
    for g, gl, gv in groups:
        qg_s[g] = jnp.concatenate(
            [q_ref[:, (g * DS_REP + r) * DS_HD:(g * DS_REP + r + 1) * DS_HD] for r in range(DS_REP)], axis=0)
        s = _dot_nt(km_ref[:, gl], qg_s[g]) + jnp.concatenate([biasm_s[...]] * DS_REP, axis=1)
        m0 = jnp.max(s, axis=0, keepdims=True)
        m_s[g] = m0
        acc_s[g] = _dot(vmt_ref[gv, :], jnp.exp2(s - m0).astype(_MXU))

    def att_body(t, carry):
        start = pl.multiple_of(t * tk, tk)
        bias = jnp.concatenate([bias_s[t]] * DS_REP, axis=1)
        for pair in (groups[:2], groups[2:]):
            ss = [_dot_nt(k_ref[pl.ds(start, tk), gl], qg_s[g]) for g, gl, _ in pair]
            for (g, _, gv), s in zip(pair, ss):
                s = s + bias
                m_prev = m_s[g]
                m_new = jnp.maximum(m_prev, jnp.max(s, axis=0, keepdims=True))
                alpha = jnp.exp2(m_prev - m_new)
                p = jnp.exp2(s - m_new).astype(_MXU)
                acc_s[g] = alpha * acc_s[g] + _dot(vt_ref[t, gv, :], p)
                m_s[g] = m_new
        return carry

    lax.fori_loop(0, n_t, att_body, 0)
    for g in range(DS_KV):
        acc = acc_s[g]
        ot = acc[:DS_HD] * (1.0 / acc[DS_HD:DS_HD + 1])
        for r in range(DS_REP):
            lo = (g * DS_REP + r) * DS_HD
            o_ref[:, lo:lo + DS_HD] = ot[:, r * tq:(r + 1) * tq].T.astype(o_ref.dtype)


def _dsa_frames(q, k, vt, qi, kid, wi, km_blk, vmt_blk, kidm_blk, nb, seq, topk):
    m = q.shape[0]
    tq = _row_tile(seq, 128)
    tk = vt.shape[2]
    assert seq % tk == 0 and tq == LANES
    nt = seq // tq
    nkt = seq // tk
    kvw = DS_KV * DS_HD
    kvr = DS_KV * (DS_HD + ONES_ROWS)
    cur = lambda b, i: (b * nt + i, 0)
    per_b = lambda b, i: (b, 0)
    fixed = lambda b, i: (0, 0)
    return pl.pallas_call(
        functools.partial(_dsa_frames_kernel, tq=tq, tk=tk, topk=topk),
        grid=(nb, nt),
        in_specs=[pl.BlockSpec((tq, DS_WIDTH), cur),
                  pl.BlockSpec((seq, kvw), per_b),
                  pl.BlockSpec((nkt, kvr, tk), lambda b, i: (b, 0, 0)),
                  pl.BlockSpec((tq, IDX_HEADS * IDX_HD), cur),
                  pl.BlockSpec((seq, LANES), per_b),
                  pl.BlockSpec((tq, LANES), cur),
                  pl.BlockSpec((LANES, kvw), fixed), pl.BlockSpec((kvr, LANES), fixed),
                  pl.BlockSpec((LANES, LANES), fixed)],
        out_specs=pl.BlockSpec((tq, DS_WIDTH), cur),
        out_shape=jax.ShapeDtypeStruct((m, DS_WIDTH), _MXU),
        scratch_shapes=[pltpu.VMEM((LANES, tq), jnp.int32), pltpu.VMEM((nkt, tk, tq), jnp.int32),
                        pltpu.VMEM((LANES, tq), _F32), pltpu.VMEM((nkt, tk, tq), _F32),
                        pltpu.VMEM((IDX_HEADS * tq, LANES), _MXU),
                        pltpu.VMEM((LANES, tq), _F32),
                        pltpu.VMEM((DS_KV, DS_REP * tq, DS_HD), _MXU),
                        pltpu.VMEM((DS_KV, 1, DS_REP * tq), _F32),
                        pltpu.VMEM((DS_KV, DS_HD + ONES_ROWS, DS_REP * tq), _F32)],
        compiler_params=_cparams(("parallel", "arbitrary")),
        name="dsa_frames",
    )(q, k, vt, qi, kid, wi, km_blk, vmt_blk, kidm_blk)


def _dsa_small_kernel(*refs, has_cache, nq, n_new, topk):
    if has_cache:
        q_ref, kn_ref, vn_ref, qi_ref, kidn_ref, wi_ref, ck_ref, cv_ref, ckid_ref, o_ref = refs
    else:
        q_ref, kn_ref, vn_ref, qi_ref, kidn_ref, wi_ref, o_ref = refs
    qst = _stack_idx_heads(qi_ref[...])
    wi = wi_ref[...]
    wbs = [jnp.broadcast_to(wi[:, h:h + 1], (nq, LANES)) for h in range(IDX_HEADS)]

    def wb_tile(width):
        return lambda h: jnp.concatenate([wbs[h]] * (width // LANES), axis=1)

    col_n = lax.broadcasted_iota(jnp.int32, (nq, LANES), 1)
    keys = [_order_key(_index_scores(qst, wb_tile(LANES), kidn_ref[...], nq), col_n < n_new)]
    if has_cache:
        past = ckid_ref.shape[1]
        sc = _index_scores(qst, wb_tile(past), ckid_ref[0], nq)
        keys.append(_order_key(sc, jnp.full(sc.shape, True)))

    def count_ge(thr):
        cnt = 0.0
        for kk in keys:
            cnt = cnt + jnp.sum(jnp.where(kk >= thr, 1.0, 0.0), axis=-1, keepdims=True)
        return cnt

    def bit_body(it, u):
        bit = jnp.int32(31) - it
        cand = u | lax.shift_left(jnp.int32(1), bit)
        cnt = count_ge(cand ^ jnp.int32(INT_MIN))
        return jnp.where(cnt >= float(topk), cand, u)

    u = lax.fori_loop(0, 32, bit_body, jnp.zeros((nq, 1), jnp.int32))
    thr = jnp.maximum(u ^ jnp.int32(INT_MIN), jnp.int32(INT_MIN + 1))

    need = float(topk) - count_ge(thr + 1)
    r = lax.broadcasted_iota(jnp.int32, (LANES, LANES), 0)
    c = lax.broadcasted_iota(jnp.int32, (LANES, LANES), 1)
    before = jnp.where(r <= c, 1.0, 0.0).astype(_MXU)
    seen = jnp.zeros((nq, 1), _F32)
    biases = [None] * len(keys)
    for idx in reversed(range(len(keys))):
        kk = keys[idx]
        cols = []
        for c0 in range(0, kk.shape[1], LANES):
            blk = kk[:, c0:c0 + LANES]
            eq = blk == thr
            upto = _dot(jnp.where(eq, 1.0, 0.0).astype(_MXU), before)
            keep = jnp.logical_or(blk > thr, jnp.logical_and(eq, seen + upto <= need))
            cols.append(jnp.where(keep, 0.0, NEG_INF))
            seen = seen + upto[:, LANES - 1:LANES]
        biases[idx] = jnp.concatenate(cols, axis=1)

    for g in range(DS_KV):
        gl = slice(g * DS_HD, (g + 1) * DS_HD)
        qg = jnp.concatenate(
            [q_ref[:, (g * DS_REP + r) * DS_HD:(g * DS_REP + r + 1) * DS_HD] for r in range(DS_REP)], axis=0)
        ks, vs = [kn_ref[:, gl]], [vn_ref[:, gl]]
        if has_cache:
            ks.append(ck_ref[0][:, gl].astype(_MXU))
            vs.append(cv_ref[0][:, gl].astype(_MXU))
        ss = [_dot_nt(qg, kk) + jnp.concatenate([bb] * DS_REP, axis=0) for kk, bb in zip(ks, biases)]
        m = ss[0].max(axis=-1, keepdims=True)
        for s in ss[1:]:
            m = jnp.maximum(m, s.max(axis=-1, keepdims=True))
        l = 0.0
        acc = 0.0
        for s, vv in zip(ss, vs):
            p = jnp.exp2(s - m)
            l = l + jnp.sum(p, axis=-1, keepdims=True)
            acc = acc + _dot(p.astype(_MXU), vv)
        og = acc / l
        for r in range(DS_REP):
            lo = (g * DS_REP + r) * DS_HD
            o_ref[:, lo:lo + DS_HD] = og[r * nq:(r + 1) * nq].astype(o_ref.dtype)


def _dsa_small(q, kn_blk, vn_blk, qi, kidn_blk, wi, cache_k, cache_v, cache_kid, nb, nq, n_new, topk):
    kvw = DS_KV * DS_HD
    has_cache = cache_k is not None
    row = lambda b: (b, 0)
    in_specs = [pl.BlockSpec((nq, DS_WIDTH), row),
                pl.BlockSpec((LANES, kvw), row), pl.BlockSpec((LANES, kvw), row),
                pl.BlockSpec((nq, IDX_HEADS * IDX_HD), row),
                pl.BlockSpec((LANES, LANES), row),
                pl.BlockSpec((nq, LANES), row)]
    args = [q, kn_blk, vn_blk, qi, kidn_blk, wi]
    if has_cache:
        past = cache_k.shape[1]
        in_specs += [pl.BlockSpec((1, past, kvw), lambda b: (b, 0, 0)),
                     pl.BlockSpec((1, past, kvw), lambda b: (b, 0, 0)),
                     pl.BlockSpec((1, past, LANES), lambda b: (b, 0, 0))]
        args += [cache_k, cache_v, cache_kid]
    return pl.pallas_call(
        functools.partial(_dsa_small_kernel, has_cache=has_cache, nq=nq, n_new=n_new, topk=topk),
        grid=(nb,),
        in_specs=in_specs,
        out_specs=pl.BlockSpec((nq, DS_WIDTH), row),
        out_shape=jax.ShapeDtypeStruct((nb * nq, DS_WIDTH), _MXU),
        compiler_params=_cparams(("parallel",)),
        name="dsa_small",
    )(*args)


def _pad_rows_front(a, rows):
    return jnp.pad(a, ((rows - a.shape[0], 0), (0, 0)))


def _pad_rows_back(a, nb, nq, rows):
    w = a.shape[1]
    return jnp.pad(a.reshape(nb, nq, w), ((0, 0), (0, rows - nq), (0, 0))).reshape(nb * rows, w)


def _transpose_with_ones(v, group_cols):
    n = v.shape[0]
    g = v.shape[1] // group_cols
    vt = v.T.reshape(g, group_cols, n)
    pad = jnp.zeros((g, ONES_ROWS, n), v.dtype).at[:, 0, :].set(1)
    return jnp.concatenate([vt, pad], axis=1).reshape(g * (group_cols + ONES_ROWS), n)


def _dup_pairs(a, hd):
    lead = a.shape[:-1]
    g = a.shape[-1] // hd
    a = a.reshape(*lead, g, 1, hd)
    return jnp.broadcast_to(a, (*lead, g, 2, hd)).reshape(*lead, g * 2 * hd)


def kernel(x_prompt, x_sample, cache_l0_k, cache_l0_v, cache_l1_k, cache_l1_v, cache_l2_k, cache_l2_v, cache_l2_kidx, cache_l3_k, cache_l3_v, meta_tokens, l0_norm, l0_w_in, l0_w_out, l0_lam_q1, l0_lam_k1, l0_lam_q2, l0_lam_k2, l0_subln, l1_norm, l1_w_in, l1_w_out, l1_sinks, l2_norm, l2_w_in, l2_w_out, l3_norm, l3_w_in, l3_w_out, l3_lam_q1, l3_lam_k1, l3_lam_q2, l3_lam_k2, l3_subln, final_norm):
    nb, seq, d = x_prompt.shape
    db, ds, _ = x_sample.shape
    past = cache_l0_k.shape[1]
    sw_rows = cache_l1_k.shape[1]
    assert seq % CHUNK == 0 and meta_tokens.shape[0] == N_META and ds <= LANES

    xs = {"f": x_prompt.reshape(nb * seq, d), "m": meta_tokens.astype(_F32), "s": x_sample.reshape(db * ds, d)}
    pos = {"f": N_META + jnp.arange(seq), "m": jnp.arange(N_META), "s": past + jnp.arange(ds)}
    pos["s"] = jnp.tile(pos["s"], db)

    tabs128 = {n: _rope_tables(p, 128, 1.0) for n, p in pos.items()}
    tabs128_da = {n: _rope_tables(p, 128, DA_HD ** -0.5 * LOG2E) for n, p in pos.items()}
    tabs128_ds = {n: _rope_tables(p, 128, DS_HD ** -0.5 * LOG2E) for n, p in pos.items()}
    tabs64 = {n: _rope_tables(p, 64, 1.0) for n, p in pos.items()}
    tabs64_sw = {n: _rope_tables(p, 64, SW_HD ** -0.5 * LOG2E) for n, p in pos.items()}

    layers = [
        dict(norm=l0_norm, w_in=l0_w_in, w_out=l0_w_out, lam=(l0_lam_q1, l0_lam_k1, l0_lam_q2, l0_lam_k2),
             subln=l0_subln, cache=(cache_l0_k, cache_l0_v)),
        dict(norm=l1_norm, w_in=l1_w_in, w_out=l1_w_out, sinks=l1_sinks, cache=(cache_l1_k, cache_l1_v)),
        dict(norm=l2_norm, w_in=l2_w_in, w_out=l2_w_out, cache=(cache_l2_k, cache_l2_v, cache_l2_kidx)),
        dict(norm=l3_norm, w_in=l3_w_in, w_out=l3_w_out, lam=(l3_lam_q1, l3_lam_k1, l3_lam_q2, l3_lam_k2),
             subln=l3_subln, cache=(cache_l3_k, cache_l3_v)),
    ]
    norms = [lp["norm"] for lp in layers[1:]] + [final_norm]
    depth = len(layers)

    hs = {n: _rms_norm_rows(x, layers[0]["norm"]) for n, x in xs.items()}
    p_st, s_st = [], []
    ys = {}
    for li, lp in enumerate(layers):
        kind = li % N_MIXERS
        w_in = lp["w_in"].astype(_MXU)
        w_out = lp["w_out"].astype(_MXU)
        o, z = {}, {}
        if kind == 0:
            lam_init = 0.8 - 0.6 * math.exp(-0.3 * li)
            wq, wk, wv, wz = (w_in[:, i * DA_WIDTH:(i + 1) * DA_WIDTH] for i in range(4))
            lams = jnp.stack([l.astype(_F32) for l in lp["lam"]])
            subln = lp["subln"].reshape(1, 2 * DA_HD).astype(_F32)
            subln_col = jnp.broadcast_to(lp["subln"].astype(_F32)[:, None], (2 * DA_HD, LANES))
            hshape = (DA_HEADS, 2 * DA_HD)
            q, k32, k16, v32, v16 = {}, {}, {}, {}, {}
            for n in xs:
                (q[n],) = _project(hs[n], wq, [_MXU], *tabs128_da[n])
                k32[n], k16[n] = _project(hs[n], wk, [_F32, _MXU], *tabs128[n])
                if n == "f":
                    v32[n], vt_f = _project(hs[n], wv, [_F32], group_cols=2 * DA_HD)
                else:
                    v32[n], v16[n] = _project(hs[n], wv, [_F32, _MXU])
                (z[n],) = _project(hs[n], wz, [_MXU])
            o["f"] = _diff_frames(q["f"], k16["f"], vt_f, k16["m"], _transpose_with_ones(v16["m"], 2 * DA_HD),
                                  lams, subln_col, lam_init, nb, seq)
            o["m"] = _diff_small(q["m"], k16["m"], v16["m"], None, None, lams, subln, lam_init, 1, N_META)
            ck, cv = lp["cache"]
            o["s"] = _diff_small(q["s"], k16["s"], v16["s"], ck.reshape(db, past, DA_WIDTH),
                                 cv.reshape(db, past, DA_WIDTH), lams, subln, lam_init, db, ds)

            def with_meta(f, mrows):
                mm = jnp.broadcast_to(mrows.reshape(1, N_META, *hshape), (nb, N_META, *hshape))
                return jnp.concatenate([mm, f.reshape(nb, seq, *hshape)], axis=1)

            p_st.append([with_meta(k32["f"], k32["m"]), with_meta(v32["f"], v32["m"])])
            s_st.append([k32["s"].reshape(db, ds, *hshape), v32["s"].reshape(db, ds, *hshape)])
        elif kind == 1:
            kvd = SW_KV * SW_HD
            wq = w_in[:, :SW_WIDTH]
            wk = w_in[:, SW_WIDTH:SW_WIDTH + kvd]
            wv = w_in[:, SW_WIDTH + kvd:SW_WIDTH + 2 * kvd]
            wz = w_in[:, SW_WIDTH + 2 * kvd:]
            wkd, wvd = _dup_pairs(wk, SW_HD), _dup_pairs(wv, SW_HD)
            sink_rows = {}
            q, kv32, kd, vd = {}, {}, {}, {}
            sk = (lp["sinks"].astype(_F32) * LOG2E).reshape(SW_KV, SW_REP, 1, 1)
            sink_cols = jnp.broadcast_to(sk.reshape(SW_KV, 1, SW_REP, 1),
                                         (SW_KV, 1, SW_REP, 2 * CHUNK)).reshape(SW_KV, 1, SW_REP * 2 * CHUNK)
            for n, rows in (("m", N_META), ("s", ds)):
                sink_rows[n] = jnp.broadcast_to(sk, (SW_KV, SW_REP, rows, LANES)).reshape(SW_KV, SW_REP * rows, LANES)
            for n in xs:
                (q[n],) = _project(hs[n], wq, [_MXU], *tabs64_sw[n])
                (kd[n],) = _project(hs[n], wkd, [_MXU], *tabs64[n])
                (vd[n],) = _project(hs[n], wvd, [_MXU])
                (z[n],) = _project(hs[n], wz, [_MXU])
            kmeta_blk = _pad_rows_front(kd["m"], WINDOW)
            vmeta_blk = _pad_rows_front(vd["m"], WINDOW)
            o["f"] = _swa_frames(q["f"], kd["f"], vd["f"], kmeta_blk, vmeta_blk, sink_cols, nb, seq)
            o["m"] = _swa_small(q["m"], kd["m"], vd["m"], None, None, sink_rows["m"], 1, N_META)
            ck, cv = lp["cache"]
            ckd = _dup_pairs(ck.reshape(db, sw_rows, kvd), SW_HD).astype(_MXU)
            cvd = _dup_pairs(cv.reshape(db, sw_rows, kvd), SW_HD).astype(_MXU)
            o["s"] = _swa_small(q["s"], kd["s"], vd["s"], ckd, cvd, sink_rows["s"], db, ds)
            assert seq >= sw_rows and sw_rows % 8 == 0
            h_tail = hs["f"].reshape(nb, seq, d)[:, seq - sw_rows:].reshape(nb * sw_rows, d)
            tail_pos = N_META + seq - sw_rows + jnp.arange(sw_rows)
            tail_tabs, tail_shift = _rope_tables(tail_pos, 64, 1.0)
            (kt,) = _project(h_tail, _pad_cols(wk), [_F32], tail_tabs, tail_shift)
            (vt,) = _project(h_tail, _pad_cols(wv), [_F32])
            p_st.append([kt[:, :kvd].reshape(nb, sw_rows, SW_KV, SW_HD), vt[:, :kvd].reshape(nb, sw_rows, SW_KV, SW_HD)])
            (ks,) = _project(hs["s"], _pad_cols(wk), [_F32], *tabs64["s"])
            (vs_,) = _project(hs["s"], _pad_cols(wv), [_F32])
            ks = ks[:, :kvd].reshape(db, ds, SW_KV, SW_HD)
            vs_ = vs_[:, :kvd].reshape(db, ds, SW_KV, SW_HD)
            k_all = jnp.concatenate([ck, ks], axis=1)
            v_all = jnp.concatenate([cv, vs_], axis=1)
            s_st.append([k_all[:, k_all.shape[1] - sw_rows:], v_all[:, v_all.shape[1] - sw_rows:]])
        else:
            kvd = DS_KV * DS_HD
            c0 = 0
            wq = w_in[:, c0:c0 + DS_WIDTH]; c0 += DS_WIDTH
            wk = w_in[:, c0:c0 + kvd]; c0 += kvd
            wv = w_in[:, c0:c0 + kvd]; c0 += kvd
            wz = w_in[:, c0:c0 + DS_WIDTH]; c0 += DS_WIDTH
            wqi = w_in[:, c0:c0 + IDX_HEADS * IDX_HD]; c0 += IDX_HEADS * IDX_HD
            wki = w_in[:, c0:c0 + IDX_HD]; c0 += IDX_HD
            wwi = w_in[:, c0:c0 + IDX_HEADS]
            wkid = jnp.concatenate([wki, wki], axis=1)
            wwi_p = _pad_cols(wwi)
            wi_scale = IDX_HEADS ** -0.5 * IDX_HD ** -0.5
            q, k32, k16, v32, v16, qi, kid32, kid16, wi = {}, {}, {}, {}, {}, {}, {}, {}, {}
            for n in xs:
                (q[n],) = _project(hs[n], wq, [_MXU], *tabs128_ds[n])
                k32[n], k16[n] = _project(hs[n], wk, [_F32, _MXU], *tabs128[n])
                if n == "f":
                    v32[n], vt_f = _project(hs[n], wv, [_F32], group_cols=DS_HD)
                else:
                    v32[n], v16[n] = _project(hs[n], wv, [_F32, _MXU])
                (z[n],) = _project(hs[n], wz, [_MXU])
                (qi[n],) = _project(hs[n], wqi, [_MXU], *tabs64[n])
                kid32[n], kid16[n] = _project(hs[n], wkid, [_F32, _MXU], *tabs64[n])
                (wi[n],) = _project(hs[n], wwi_p, [_F32], scale=wi_scale)
            topk_p = min(TOPK_MAX, seq // 4)
            topk_s = min(TOPK_MAX, (past + ds) // 4)
            km_blk = _pad_rows_back(k16["m"], 1, N_META, LANES)
            vm_blk = _pad_rows_back(v16["m"], 1, N_META, LANES)
            kidm_blk = _pad_rows_back(kid16["m"], 1, N_META, LANES)
            o["f"] = _dsa_frames(q["f"], k16["f"], vt_f, qi["f"], kid16["f"], wi["f"],
                                 km_blk, _transpose_with_ones(vm_blk, DS_HD), kidm_blk, nb, seq, topk_p)
            o["m"] = _dsa_small(q["m"], km_blk, vm_blk, qi["m"], kidm_blk, wi["m"], None, None, None,
                                1, N_META, N_META, topk_p)
            ck, cv, cki = lp["cache"]
            ckid = jnp.concatenate([cki, cki], axis=-1).astype(_MXU)
            o["s"] = _dsa_small(q["s"], _pad_rows_back(k16["s"], db, ds, LANES), _pad_rows_back(v16["s"], db, ds, LANES),
                                qi["s"], _pad_rows_back(kid16["s"], db, ds, LANES), wi["s"],
                                ck.reshape(db, past, kvd), cv.reshape(db, past, kvd), ckid,
                                db, ds, ds, topk_s)

            def with_meta(f, mrows, *tail):
                mm = jnp.broadcast_to(mrows.reshape(1, N_META, *tail), (nb, N_META, *tail))
                return jnp.concatenate([mm, f.reshape(nb, seq, *tail)], axis=1)

            p_st.append([with_meta(k32["f"], k32["m"], DS_KV, DS_HD), with_meta(v32["f"], v32["m"], DS_KV, DS_HD),
                         with_meta(kid32["f"][:, :IDX_HD], kid32["m"][:, :IDX_HD], IDX_HD)])
            s_st.append([k32["s"].reshape(db, ds, DS_KV, DS_HD), v32["s"].reshape(db, ds, DS_KV, DS_HD),
                         kid32["s"][:, :IDX_HD].reshape(db, ds, IDX_HD)])
        last = li == depth - 1
        for n in xs:
            res = _out_project(o[n], z[n], w_out, xs[n], norms[li], last)
            if last:
                (ys[n],) = res
            else:
                xs[n], hs[n] = res

    y_prompt = ys["f"].reshape(nb, seq, d)
    y_sample = ys["s"].reshape(db, ds, d)
    return (y_prompt, y_sample,
            p_st[0][0], p_st[0][1], s_st[0][0], s_st[0][1],
            p_st[1][0], p_st[1][1], s_st[1][0], s_st[1][1],
            p_st[2][0], p_st[2][1], p_st[2][2], s_st[2][0], s_st[2][1], s_st[2][2],
            p_st[3][0], p_st[3][1], s_st[3][0], s_st[3][1])


def _pad_cols(w):
    n = w.shape[1]
    return jnp.pad(w, ((0, 0), (0, (-n) % LANES)))
```

```python
import functools
import math

import jax
import jax.numpy as jnp
from jax import lax
from jax.experimental import pallas as pl
from jax.experimental.pallas import tpu as pltpu

CHUNK = 64
N_META = 16
ROPE_THETA = 500000.0
NORM_EPS = 1e-6
NEG_INF = -1e30
DA_HEADS = 8
DA_HD = 128
DA_WIDTH = DA_HEADS * 2 * DA_HD
SW_HEADS = 32
SW_KV = 4
SW_HD = 64
SW_WIDTH = SW_HEADS * SW_HD
SW_REP = SW_HEADS // SW_KV
WINDOW = 128
DS_HEADS = 16
DS_KV = 4
DS_HD = 128
DS_WIDTH = DS_HEADS * DS_HD
DS_REP = DS_HEADS // DS_KV
IDX_HEADS = 16
IDX_HD = 64
TOPK_MAX = 256
N_MIXERS = 3

LANES = 128
INT_MIN = -(2 ** 31)
LOG2E = math.log2(math.e)
ONES_ROWS = 16
VMEM_LIMIT = 56 * 1024 * 1024

_MXU = jnp.bfloat16
_F32 = jnp.float32

_NT = (((1,), (1,)), ((), ()))


def _cparams(sem):
    return pltpu.CompilerParams(dimension_semantics=sem, vmem_limit_bytes=VMEM_LIMIT)


def _row_tile(m, pref):
    t = pref
    while t > 8 and m % t:
        t //= 2
    assert m % t == 0, (m, pref)
    return t


def _dot(a, b):
    return jnp.dot(a, b, preferred_element_type=_F32)


def _dot_nt(a, b):
    return lax.dot_general(a, b, _NT, preferred_element_type=_F32)


def _norm_kernel(x_ref, g_ref, h_ref):
    x = x_ref[...]
    y = x * lax.rsqrt(jnp.mean(x * x, axis=-1, keepdims=True) + NORM_EPS)
    h_ref[...] = (y * g_ref[...]).astype(h_ref.dtype)


def _rms_norm_rows(x, g):
    m, d = x.shape
    out_dtype = _MXU
    tm = _row_tile(m, 512)
    return pl.pallas_call(
        _norm_kernel,
        grid=(m // tm,),
        in_specs=[pl.BlockSpec((tm, d), lambda i: (i, 0)),
                  pl.BlockSpec((1, d), lambda i: (0, 0))],
        out_specs=pl.BlockSpec((tm, d), lambda i: (i, 0)),
        out_shape=jax.ShapeDtypeStruct((m, d), out_dtype),
        compiler_params=_cparams(("parallel",)),
        name="rms_norm",
    )(x, g.reshape(1, d).astype(_F32))


def _proj_kernel(*refs, rope_shift, scale, n_out, tn, group_cols):
    h_ref, w_ref = refs[0], refs[1]
    pos = 2
    if rope_shift is not None:
        c_ref, a_ref, b_ref = refs[2:5]
        pos = 5
    outs = refs[pos:pos + n_out]
    ot_ref = refs[pos + n_out] if group_cols else None
    if ot_ref is not None:
        tm = ot_ref.shape[2]
        row = lax.broadcasted_iota(jnp.int32, (ONES_ROWS, tm), 0)
        ones_blk = jnp.where(row == 0, 1.0, 0.0).astype(ot_ref.dtype)
        for g in range(tn // group_cols):
            base = g * (group_cols + ONES_ROWS) + group_cols
            ot_ref[0, base:base + ONES_ROWS, :] = ones_blk
    h = h_ref[...]
    cw = 2 * LANES if tn % (2 * LANES) == 0 else LANES
    for c in range(tn // cw):
        wide = _dot(h, w_ref[:, c * cw:(c + 1) * cw])
        for sub in range(cw // LANES):
            acc = wide[:, sub * LANES:(sub + 1) * LANES]
            if rope_shift is not None:
                acc = (acc * c_ref[...]
                       + pltpu.roll(acc, rope_shift, 1) * a_ref[...]
                       + pltpu.roll(acc, LANES - rope_shift, 1) * b_ref[...])
            elif scale is not None:
                acc = acc * scale
            cols = slice(c * cw + sub * LANES, c * cw + (sub + 1) * LANES)
            for o in outs:
                o[:, cols] = acc.astype(o.dtype)
            if ot_ref is not None:
                lo = cols.start
                r0 = (lo // group_cols) * (group_cols + ONES_ROWS) + lo % group_cols
                ot_ref[0, r0:r0 + LANES, :] = acc.T.astype(ot_ref.dtype)


def _project(h, w, out_dtypes, tabs=None, rope_shift=None, scale=None, group_cols=0):
    m, d = h.shape
    n = w.shape[1]
    tm = _row_tile(m if tabs is None else math.gcd(m, tabs[0].shape[0]), 512)
    tn = n if n <= 2048 else 2048
    assert n % tn == 0 and tn % LANES == 0
    in_specs = [pl.BlockSpec((tm, d), lambda j, i: (i, 0)),
                pl.BlockSpec((d, tn), lambda j, i: (0, j))]
    args = [h, w]
    if tabs is not None:
        trows = tabs[0].shape[0]
        assert trows % tm == 0
        nrep = trows // tm
        for t in tabs:
            in_specs.append(pl.BlockSpec((tm, LANES), lambda j, i: (i % nrep, 0)))
            args.append(t)
    else:
        rope_shift = None
    out_specs = [pl.BlockSpec((tm, tn), lambda j, i: (i, j)) for _ in out_dtypes]
    out_shape = [jax.ShapeDtypeStruct((m, n), dt) for dt in out_dtypes]
    if group_cols:
        assert tn % group_cols == 0 and group_cols % LANES == 0
        grow = group_cols + ONES_ROWS
        out_specs.append(pl.BlockSpec((1, tn // group_cols * grow, tm), lambda j, i: (i, j, 0)))
        out_shape.append(jax.ShapeDtypeStruct((m // tm, n // group_cols * grow, tm), _MXU))
    outs = pl.pallas_call(
        functools.partial(_proj_kernel, rope_shift=rope_shift, scale=scale,
                          n_out=len(out_dtypes), tn=tn, group_cols=group_cols),
        grid=(n // tn, m // tm),
        in_specs=in_specs,
        out_specs=out_specs,
        out_shape=out_shape,
        compiler_params=_cparams(("parallel", "parallel")),
        name="in_proj",
    )(*args)
    return outs


def _outproj_kernel(o_ref, z_ref, w_ref, x_ref, g_ref, *outs, emit_x):
    z = z_ref[...].astype(_F32)
    a = o_ref[...].astype(_F32) * (z / (1.0 + jnp.exp(-z)))
    y = x_ref[...] + _dot(a.astype(w_ref.dtype), w_ref[...])
    if emit_x:
        outs[0][...] = y
    n = y * lax.rsqrt(jnp.mean(y * y, axis=-1, keepdims=True) + NORM_EPS) * g_ref[...]
    outs[-1][...] = n.astype(outs[-1].dtype)


def _out_project(o, z, w, x, g_next, last):
    m, d = x.shape
    wd = o.shape[1]
    tm = _row_tile(m, 512)
    row = lambda i: (i, 0)
    fixed = lambda i: (0, 0)
    if last:
        out_shape = [jax.ShapeDtypeStruct((m, d), _F32)]
    else:
        out_shape = [jax.ShapeDtypeStruct((m, d), _F32), jax.ShapeDtypeStruct((m, d), _MXU)]
    return pl.pallas_call(
        functools.partial(_outproj_kernel, emit_x=not last),
        grid=(m // tm,),
        in_specs=[pl.BlockSpec((tm, wd), row), pl.BlockSpec((tm, wd), row),
                  pl.BlockSpec((wd, d), fixed), pl.BlockSpec((tm, d), row),
                  pl.BlockSpec((1, d), fixed)],
        out_specs=[pl.BlockSpec((tm, d), row) for _ in out_shape],
        out_shape=out_shape,
        compiler_params=_cparams(("parallel",)),
        name="out_proj",
    )(o, z, w, x, g_next.reshape(1, d).astype(_F32))


def _rope_tables(pos, hd, scale=1.0):
    rd = hd // 4
    half = rd // 2
    lane = jnp.arange(LANES)
    d = lane % hd
    f = d % half
    inv_freq = ROPE_THETA ** (-(jnp.arange(half, dtype=_F32) * 2.0 / rd))
    ang = pos.astype(_F32)[:, None] * inv_freq[f][None, :]
    cos, sin = jnp.cos(ang), jnp.sin(ang)
    in_rot = (d < rd)[None, :]
    c = jnp.where(in_rot, cos, 1.0)
    a = jnp.where(((d >= half) & (d < rd))[None, :], sin, 0.0)
    b = jnp.where((d < half)[None, :], -sin, 0.0)
    s = jnp.float32(scale)
    return (c * s, a * s, b * s), half


def _lam_value(lam_ref, lam_init):
    l = lam_ref[...]
    s1 = jnp.sum(l[0:1] * l[1:2], axis=-1, keepdims=True)
    s2 = jnp.sum(l[2:3] * l[3:4], axis=-1, keepdims=True)
    return jnp.exp(s1) - jnp.exp(s2) + lam_init


def _diff_finish(acc1, l1, acc2, l2, lam, sub, lam_init):
    o = acc1 / l1 - lam * (acc2 / l2)
    o = o * lax.rsqrt(jnp.mean(o * o, axis=-1, keepdims=True) + NORM_EPS) * sub
    return o * (1.0 - lam_init)


def _diff_frames_kernel(q_ref, k_ref, vt_ref, km_ref, vmt_ref, lam_ref, subc_ref, o_ref,
                        acc_s, sa_s, sb_s, pa_s, pb_s, *, tq, lam_init):
    i = pl.program_id(2)
    hw = 2 * DA_HD

    def q_half(j):
        return q_ref[:, j * DA_HD:(j + 1) * DA_HD]

    km = km_ref[...]
    vmt = vmt_ref[...]
    ms = []
    for j in range(2):
        s = _dot_nt(km[:, j * DA_HD:(j + 1) * DA_HD], q_half(j))
        m = jnp.max(s, axis=0, keepdims=True)
        ms.append(m)
        acc_s[j] = _dot(vmt, jnp.exp2(s - m).astype(vmt.dtype))

    def scores_into(dst, t):
        start = pl.multiple_of(t * tq, tq)
        for j in range(2):
            dst[j] = _dot_nt(k_ref[pl.ds(start, tq), j * DA_HD:(j + 1) * DA_HD], q_half(j))

    def value_update(t, a, p_rd):
        vt = vt_ref[t]
        for j in range(2):
            acc_s[j] = a[j] * acc_s[j] + _dot(vt, p_rd[j])

    def tile_step(t, carry, src, dst, p_rd, p_wr, masked, has_next):
        m, a = carry
        if masked:
            kc = lax.broadcasted_iota(jnp.int32, (tq, tq), 0) // CHUNK
            qc = lax.broadcasted_iota(jnp.int32, (tq, tq), 1) // CHUNK
            ok = kc <= qc
        m_out, a_out = [], []
        for j in range(2):
            s = src[j]
            if masked:
                s = jnp.where(ok, s, NEG_INF)
            m_new = jnp.maximum(m[j], jnp.max(s, axis=0, keepdims=True))
            a_out.append(jnp.exp2(m[j] - m_new))
            m_out.append(m_new)
            p_wr[j] = jnp.exp2(s - m_new).astype(p_wr.dtype)
        value_update(jnp.maximum(t - 1, 0), a, p_rd)
        if has_next:
            scores_into(dst, t + 1)
        return tuple(m_out), tuple(a_out)

    scores_into(sa_s, 0)
    pb_s[...] = jnp.zeros_like(pb_s)
    one = jnp.ones((1, tq), _F32)
    carry = ((ms[0], ms[1]), (one, one))

    def pair_body(u, carry):
        carry = tile_step(2 * u, carry, sa_s, sb_s, pb_s, pa_s, False, True)
        return tile_step(2 * u + 1, carry, sb_s, sa_s, pa_s, pb_s, False, True)

    carry = lax.fori_loop(0, i // 2, pair_body, carry)

    @pl.when(i % 2 == 0)
    def _():
        _, a = tile_step(i, carry, sa_s, sb_s, pb_s, pa_s, True, False)
        value_update(i, a, pa_s)

    @pl.when(i % 2 == 1)
    def _():
        c = tile_step(i - 1, carry, sa_s, sb_s, pb_s, pa_s, False, True)
        _, a = tile_step(i, c, sb_s, sa_s, pa_s, pb_s, True, False)
        value_update(i, a, pb_s)

    acc0 = acc_s[0]
    acc1 = acc_s[1]
    lam = _lam_value(lam_ref, lam_init)
    inv0 = 1.0 / acc0[hw:hw + 1]
    inv1 = lam / acc1[hw:hw + 1]
    ot = acc0[:hw] * inv0 - acc1[:hw] * inv1
    sub = jnp.concatenate([subc_ref[...]] * (tq // LANES), axis=1)
    ot = ot * lax.rsqrt(jnp.mean(ot * ot, axis=0, keepdims=True) + NORM_EPS) * sub * (1.0 - lam_init)
    o_ref[...] = ot.T.astype(o_ref.dtype)


def _diff_frames(q, k, vt, km, vmt, lams, subln_col, lam_init, nb, seq):
    m = q.shape[0]
    hw = 2 * DA_HD
    hr = hw + ONES_ROWS
    tq = vt.shape[2]
    assert seq % tq == 0 and tq % LANES == 0
    nt = seq // tq
    return pl.pallas_call(
        functools.partial(_diff_frames_kernel, tq=tq, lam_init=lam_init),
        grid=(nb, DA_HEADS, nt),
        in_specs=[pl.BlockSpec((tq, hw), lambda b, h, i: (b * nt + i, h)),
                  pl.BlockSpec((seq, hw), lambda b, h, i: (b, h), pipeline_mode=pl.Buffered(1)),
                  pl.BlockSpec((nt, hr, tq), lambda b, h, i: (b, h, 0), pipeline_mode=pl.Buffered(1)),
                  pl.BlockSpec((N_META, hw), lambda b, h, i: (0, h)),
                  pl.BlockSpec((hr, N_META), lambda b, h, i: (h, 0)),
                  pl.BlockSpec((4, DA_HD), lambda b, h, i: (0, 0)),
                  pl.BlockSpec((hw, LANES), lambda b, h, i: (0, 0))],
        out_specs=pl.BlockSpec((tq, hw), lambda b, h, i: (b * nt + i, h)),
        out_shape=jax.ShapeDtypeStruct((m, DA_WIDTH), _MXU),
        scratch_shapes=[pltpu.VMEM((2, hr, tq), _F32),
                        pltpu.VMEM((2, tq, tq), _F32), pltpu.VMEM((2, tq, tq), _F32),
                        pltpu.VMEM((2, tq, tq), _MXU), pltpu.VMEM((2, tq, tq), _MXU)],
        compiler_params=_cparams(("parallel", "parallel", "arbitrary")),
        name="diff_frames",
    )(q, k, vt, km, vmt, lams, subln_col)


def _diff_small_kernel(*refs, has_cache, lam_init):
    if has_cache:
        q_ref, k_ref, v_ref, ck_ref, cv_ref, lam_ref, sub_ref, o_ref = refs
    else:
        q_ref, k_ref, v_ref, lam_ref, sub_ref, o_ref = refs
    q = q_ref[...]
    kn = k_ref[...]
    vn = v_ref[...]
    parts = [(kn, vn)]
    if has_cache:
        parts.append((ck_ref[0].astype(kn.dtype), cv_ref[0].astype(vn.dtype)))
    accs, ls = [], []
    for j in range(2):
        qj = q[:, j * DA_HD:(j + 1) * DA_HD]
        ss = [_dot_nt(qj, kp[:, j * DA_HD:(j + 1) * DA_HD]) for kp, _ in parts]
        m = ss[0].max(axis=-1, keepdims=True)
        for s in ss[1:]:
            m = jnp.maximum(m, s.max(axis=-1, keepdims=True))
        l = 0.0
        acc = 0.0
        for s, (_, vp) in zip(ss, parts):
            p = jnp.exp2(s - m)
            l = l + jnp.sum(p, axis=-1, keepdims=True)
            acc = acc + _dot(p.astype(vp.dtype), vp)
        accs.append(acc)
        ls.append(l)
    lam = _lam_value(lam_ref, lam_init)
    o = _diff_finish(accs[0], ls[0], accs[1], ls[1], lam, sub_ref[...], lam_init)
    o_ref[...] = o.astype(o_ref.dtype)


def _diff_small(q, k, v, cache_k, cache_v, lams, subln, lam_init, nb, nq):
    hw = 2 * DA_HD
    has_cache = cache_k is not None
    qspec = pl.BlockSpec((nq, hw), lambda b, h: (b, h))
    in_specs = [qspec, qspec, qspec]
    args = [q, k, v]
    if has_cache:
        past = cache_k.shape[1]
        cspec = pl.BlockSpec((1, past, hw), lambda b, h: (b, 0, h))
        in_specs += [cspec, cspec]
        args += [cache_k, cache_v]
    in_specs += [pl.BlockSpec((4, DA_HD), lambda b, h: (0, 0)),
                 pl.BlockSpec((1, hw), lambda b, h: (0, 0))]
    args += [lams, subln]
    return pl.pallas_call(
        functools.partial(_diff_small_kernel, has_cache=has_cache, lam_init=lam_init),
        grid=(nb, DA_HEADS),
        in_specs=in_specs,
        out_specs=qspec,
        out_shape=jax.ShapeDtypeStruct((nb * nq, DA_WIDTH), _MXU),
        compiler_params=_cparams(("parallel", "parallel")),
        name="diff_small",
    )(*args)


def _stack_heads(qg, nheads):
    n = qg.shape[0]
    lane = lax.broadcasted_iota(jnp.int32, (n, LANES), 1)
    lo = lane < SW_HD
    pieces = []
    for r in range(nheads):
        blk = qg[:, (r // 2) * LANES:(r // 2 + 1) * LANES]
        keep = lo if r % 2 == 0 else jnp.logical_not(lo)
        pieces.append(jnp.where(keep, blk, jnp.zeros_like(blk)))
    return jnp.concatenate(pieces, axis=0)


def _unstack_heads(o, n, nheads):
    lane = lax.broadcasted_iota(jnp.int32, (n, LANES), 1)
    lo = lane < SW_HD
    return [jnp.where(lo, o[(2 * p) * n:(2 * p + 1) * n], o[(2 * p + 1) * n:(2 * p + 2) * n])
            for p in range(nheads // 2)]


def _sink_softmax_pv(ss, vs, sk):
    m = sk
    for s in ss:
        m = jnp.maximum(m, jnp.max(s, axis=-1, keepdims=True))
    den = jnp.exp2(sk - m)
    acc = 0.0
    for s, v in zip(ss, vs):
        e = jnp.exp2(s - m)
        den = den + jnp.sum(e, axis=-1, keepdims=True)
        acc = acc + _dot(e.astype(v.dtype), v)
    return acc / den


def _swa_frames_kernel(q_ref, kc_ref, vc_ref, kp_ref, vp_ref, km_ref, vm_ref, sk_ref, o_ref,
                       kcat, vcat_t, bias_s, *, tq):
    i = pl.program_id(1)
    first = i == 0
    prev_rows = WINDOW
    pair = 2 * CHUNK
    kwin = 2 * pair
    ncol = SW_REP * pair
    kcat[0:prev_rows] = jnp.where(first, km_ref[...], kp_ref[...])
    kcat[prev_rows:prev_rows + tq] = kc_ref[...]
    vprev = jnp.where(first, vm_ref[...], vp_ref[...])
    vcat_t[:, 0:prev_rows] = vprev.astype(_F32).T.astype(vcat_t.dtype)
    vcat_t[:, prev_rows:prev_rows + tq] = vc_ref[...].astype(_F32).T.astype(vcat_t.dtype)

    krow = lax.broadcasted_iota(jnp.int32, (kwin, ncol), 0)
    second = (lax.broadcasted_iota(jnp.int32, (kwin, ncol), 1) % pair) >= CHUNK
    lo = jnp.where(second, CHUNK, 0)
    hi = jnp.where(second, kwin, kwin - CHUNK)
    bias_s[0] = jnp.where(jnp.logical_and(krow >= lo, krow < hi), 0.0, NEG_INF)
    lo_first = jnp.maximum(lo, jnp.where(first, prev_rows - N_META, 0))
    bias_s[1] = jnp.where(jnp.logical_and(krow >= lo_first, krow < hi), 0.0, NEG_INF)

    for k in range(tq // pair):
        rows = slice(k * pair, (k + 1) * pair)
        win = slice(k * pair, k * pair + kwin)
        bias = bias_s[1 if k == 0 else 0]
        for g in range(SW_KV):
            gl = slice(g * LANES, (g + 1) * LANES)
            qst = _stack_heads(q_ref[rows, g * SW_REP * SW_HD:(g + 1) * SW_REP * SW_HD], SW_REP)
            s = _dot_nt(kcat[win, gl], qst) + bias
            sk = sk_ref[g]
            m = jnp.maximum(jnp.max(s, axis=0, keepdims=True), sk)
            e = jnp.exp2(s - m)
            den = jnp.sum(e, axis=0, keepdims=True) + jnp.exp2(sk - m)
            ot = _dot(vcat_t[gl, win], e.astype(vcat_t.dtype)) * (1.0 / den)
            o = ot.T
            for p, blk in enumerate(_unstack_heads(o, pair, SW_REP)):
                lo = g * SW_REP * SW_HD + p * LANES
                o_ref[rows, lo:lo + LANES] = blk.astype(o_ref.dtype)


def _swa_frames(q, kd, vd, kmeta_blk, vmeta_blk, sink_cols, nb, seq):
    m = q.shape[0]
    tq = _row_tile(seq, 512)
    assert tq % (2 * CHUNK) == 0 and WINDOW == 2 * CHUNK
    nt = seq // tq
    pw = SW_KV * LANES
    per = tq // WINDOW
    cur = lambda b, i: (b * nt + i, 0)
    prev = lambda b, i: (jnp.maximum((b * nt + i) * per - 1, 0), 0)
    fixed = lambda b, i: (0, 0)
    return pl.pallas_call(
        functools.partial(_swa_frames_kernel, tq=tq),
        grid=(nb, nt),
        in_specs=[pl.BlockSpec((tq, SW_WIDTH), cur),
                  pl.BlockSpec((tq, pw), cur), pl.BlockSpec((tq, pw), cur),
                  pl.BlockSpec((WINDOW, pw), prev), pl.BlockSpec((WINDOW, pw), prev),
                  pl.BlockSpec((WINDOW, pw), fixed), pl.BlockSpec((WINDOW, pw), fixed),
                  pl.BlockSpec((SW_KV, 1, SW_REP * 2 * CHUNK), lambda b, i: (0, 0, 0))],
        out_specs=pl.BlockSpec((tq, SW_WIDTH), cur),
        out_shape=jax.ShapeDtypeStruct((m, SW_WIDTH), _MXU),
        scratch_shapes=[pltpu.VMEM((WINDOW + tq, pw), _MXU), pltpu.VMEM((pw, WINDOW + tq), _MXU),
                        pltpu.VMEM((2, 4 * CHUNK, SW_REP * 2 * CHUNK), _F32)],
        compiler_params=_cparams(("parallel", "arbitrary")),
        name="swa_frames",
    )(q, kd, vd, kd, vd, kmeta_blk, vmeta_blk, sink_cols)


def _swa_small_kernel(*refs, has_cache, nq):
    if has_cache:
        q_ref, k_ref, v_ref, ck_ref, cv_ref, sk_ref, o_ref = refs
    else:
        q_ref, k_ref, v_ref, sk_ref, o_ref = refs
    for g in range(SW_KV):
        gl = slice(g * LANES, (g + 1) * LANES)
        qst = _stack_heads(q_ref[:, g * SW_REP * SW_HD:(g + 1) * SW_REP * SW_HD], SW_REP)
        ks, vs = [k_ref[:, gl]], [v_ref[:, gl]]
        if has_cache:
            ks.append(ck_ref[0][:, gl])
            vs.append(cv_ref[0][:, gl])
        ss = [_dot_nt(qst, kk) for kk in ks]
        o = _sink_softmax_pv(ss, vs, sk_ref[g][:, 0:1])
        for p, blk in enumerate(_unstack_heads(o, nq, SW_REP)):
            lo = g * SW_REP * SW_HD + p * LANES
            o_ref[:, lo:lo + LANES] = blk.astype(o_ref.dtype)


def _swa_small(q, kd, vd, cache_kd, cache_vd, sink_rows, nb, nq):
    pw = SW_KV * LANES
    has_cache = cache_kd is not None
    row = lambda b: (b, 0)
    in_specs = [pl.BlockSpec((nq, SW_WIDTH), row), pl.BlockSpec((nq, pw), row), pl.BlockSpec((nq, pw), row)]
    args = [q, kd, vd]
    if has_cache:
        past = cache_kd.shape[1]
        cspec = pl.BlockSpec((1, past, pw), lambda b: (b, 0, 0))
        in_specs += [cspec, cspec]
        args += [cache_kd, cache_vd]
    in_specs.append(pl.BlockSpec((SW_KV, SW_REP * nq, LANES), lambda b: (0, 0, 0)))
    args.append(sink_rows)
    return pl.pallas_call(
        functools.partial(_swa_small_kernel, has_cache=has_cache, nq=nq),
        grid=(nb,),
        in_specs=in_specs,
        out_specs=pl.BlockSpec((nq, SW_WIDTH), row),
        out_shape=jax.ShapeDtypeStruct((nb * nq, SW_WIDTH), _MXU),
        compiler_params=_cparams(("parallel",)),
        name="swa_small",
    )(*args)


def _order_key(score, valid):
    bits = lax.bitcast_convert_type(score + 0.0, jnp.int32)
    key = jnp.where(bits < 0, bits ^ jnp.int32(0x7FFFFFFF), bits)
    return key if valid is None else jnp.where(valid, key, jnp.int32(INT_MIN))


def _stack_idx_heads(qi):
    return _stack_heads(qi, IDX_HEADS)


def _index_scores(qst, wb_of, kid, n):
    score = None
    hb = 4
    for h0 in range(0, IDX_HEADS, hb):
        sc = _dot_nt(qst[h0 * n:(h0 + hb) * n], kid)
        for h in range(h0, h0 + hb):
            part = jnp.maximum(sc[(h - h0) * n:(h - h0 + 1) * n], 0.0) * wb_of(h)
            score = part if score is None else score + part
    return score


def _index_scores_t(kid, qst, wit, n):
    score = None
    hb = 4
    for h0 in range(0, IDX_HEADS, hb):
        sc = _dot_nt(kid, qst[h0 * n:(h0 + hb) * n])
        for h in range(h0, h0 + hb):
            part = jnp.maximum(sc[:, (h - h0) * n:(h - h0 + 1) * n], 0.0) * wit[h:h + 1, :]
            score = part if score is None else score + part
    return score


def _dsa_frames_kernel(q_ref, k_ref, vt_ref, qi_ref, kid_ref, wi_ref, km_ref, vmt_ref, kidm_ref, o_ref,
                       keym_s, key_s, biasm_s, bias_s, qst_s, wit_s, qg_s, m_s, acc_s, *, tq, tk, topk):
    i = pl.program_id(1)
    n_t = (i * tq + tq + tk - 1) // tk
    qst_s[...] = _stack_idx_heads(qi_ref[...])
    wit_s[...] = wi_ref[...].T
    sc_m = _index_scores_t(kidm_ref[...], qst_s[...], wit_s[...], tq)
    row_m = lax.broadcasted_iota(jnp.int32, (LANES, tq), 0)
    keym_s[...] = _order_key(sc_m, row_m < N_META)

    q_chunk = (lax.broadcasted_iota(jnp.int32, (tk, tq), 1) + i * tq) // CHUNK

    def score_body(t, carry, masked):
        start = pl.multiple_of(t * tk, tk)
        sc = _index_scores_t(kid_ref[pl.ds(start, tk), :], qst_s[...], wit_s[...], tq)
        if masked:
            k_chunk = (lax.broadcasted_iota(jnp.int32, (tk, tq), 0) + t * tk) // CHUNK
            key_s[t] = _order_key(sc, k_chunk <= q_chunk)
        else:
            key_s[t] = _order_key(sc, None)
        return carry

    n_full = (i * tq) // tk
    lax.fori_loop(0, n_full, functools.partial(score_body, masked=False), 0)
    lax.fori_loop(n_full, n_t, functools.partial(score_body, masked=True), 0)

    def count_ge(thr):
        def fold(x):
            return jnp.sum(x.reshape(x.shape[0] // 64, 64, tq), axis=0)

        part = fold(jnp.where(keym_s[...] >= thr, 1.0, 0.0))

        def cbody(t, acc):
            return acc + fold(jnp.where(key_s[t] >= thr, 1.0, 0.0))

        part = lax.fori_loop(0, n_t, cbody, part)
        return jnp.sum(part, axis=0, keepdims=True)

    def bit_body(it, carry):
        u, cnt_u = carry
        bit = jnp.int32(31) - it
        cand = u | lax.shift_left(jnp.int32(1), bit)
        cnt = count_ge(cand ^ jnp.int32(INT_MIN))
        ok = cnt >= float(topk)
        return jnp.where(ok, cand, u), jnp.where(ok, cnt, cnt_u)

    u, cnt_u = lax.fori_loop(0, 32, bit_body, (jnp.zeros((1, tq), jnp.int32), jnp.zeros((1, tq), _F32)))
    thr = jnp.maximum(u ^ jnp.int32(INT_MIN), jnp.int32(INT_MIN + 1))

    biasm_s[...] = jnp.where(keym_s[...] >= thr, 0.0, NEG_INF)

    def bias_body(t, carry):
        bias_s[t] = jnp.where(key_s[t] >= thr, 0.0, NEG_INF)
        return carry

    lax.fori_loop(0, n_t, bias_body, 0)

    @pl.when(jnp.max(cnt_u) > float(topk))
    def _():
        need = float(topk) - count_ge(thr + 1)

        def tie_bias(key, seen):
            rows = key.shape[0]
            eq = key == thr
            r = lax.broadcasted_iota(jnp.int32, (rows, rows), 0)
            c = lax.broadcasted_iota(jnp.int32, (rows, rows), 1)
            upto = _dot(jnp.where(c <= r, 1.0, 0.0).astype(_MXU), jnp.where(eq, 1.0, 0.0).astype(_MXU))
            keep = jnp.logical_or(key > thr, jnp.logical_and(eq, seen + upto <= need))
            return jnp.where(keep, 0.0, NEG_INF), seen + upto[rows - 1:rows]

        biasm_s[...], seen = tie_bias(keym_s[...], jnp.zeros((1, tq), _F32))

        def tie_body(t, seen):
            bias_s[t], seen = tie_bias(key_s[t], seen)
            return seen

        lax.fori_loop(0, n_t, tie_body, seen)

    vr = DS_HD + ONES_ROWS
    groups = [(g, slice(g * DS_HD, (g + 1) * DS_HD), slice(g * vr, (g + 1) * vr)) for g in range(DS_KV)]
    for g, gl, gv in groups:
        qg_s[g] = jnp.concatenate(
            [q_ref[:, (g * DS_REP + r) * DS_HD:(g * DS_REP + r + 1) * DS_HD] for r in range(DS_REP)], axis=0)
        s = _dot_nt(km_ref[:, gl], qg_s[g]) + jnp.concatenate([biasm_s[...]] * DS_REP, axis=1)
        m0 = jnp.max(s, axis=0, keepdims=True)
        m_s[g] = m0
        acc_s[g] = _dot(vmt_ref[gv, :], jnp.exp2(s - m0).astype(_MXU))

    def att_body(t, carry):
        start = pl.multiple_of(t * tk, tk)
        bias = jnp.concatenate([bias_s[t]] * DS_REP, axis=1)
        for pair in (groups[:2], groups[2:]):
            ss = [_dot_nt(k_ref[pl.ds(start, tk), gl], qg_s[g]) for g, gl, _ in pair]
            for (g, _, gv), s in zip(pair, ss):
                s = s + bias
                m_prev = m_s[g]
                m_new = jnp.maximum(m_prev, jnp.max(s, axis=0, keepdims=True))
                alpha = jnp.exp2(m_prev - m_new)
                p = jnp.exp2(s - m_new).astype(_MXU)
                acc_s[g] = alpha * acc_s[g] + _dot(vt_ref[t, gv, :], p)
                m_s[g] = m_new
        return carry

    lax.fori_loop(0, n_t, att_body, 0)
    for g in range(DS_KV):
        acc = acc_s[g]
        ot = acc[:DS_HD] * (1.0 / acc[DS_HD:DS_HD + 1])
        for r in range(DS_REP):
            lo = (g * DS_REP + r) * DS_HD
            o_ref[:, lo:lo + DS_HD] = ot[:, r * tq:(r + 1) * tq].T.astype(o_ref.dtype)


def _dsa_frames(q, k, vt, qi, kid, wi, km_blk, vmt_blk, kidm_blk, nb, seq, topk):
    m = q.shape[0]
    tq = _row_tile(seq, 128)
    tk = vt.shape[2]
    assert seq % tk == 0 and tq == LANES
    nt = seq // tq
    nkt = seq // tk
    kvw = DS_KV * DS_HD
    kvr = DS_KV * (DS_HD + ONES_ROWS)
    cur = lambda b, i: (b * nt + i, 0)
    per_b = lambda b, i: (b, 0)
    fixed = lambda b, i: (0, 0)
    return pl.pallas_call(
        functools.partial(_dsa_frames_kernel, tq=tq, tk=tk, topk=topk),
        grid=(nb, nt),
        in_specs=[pl.BlockSpec((tq, DS_WIDTH), cur),
                  pl.BlockSpec((seq, kvw), per_b),
                  pl.BlockSpec((nkt, kvr, tk), lambda b, i: (b, 0, 0)),
                  pl.BlockSpec((tq, IDX_HEADS * IDX_HD), cur),
                  pl.BlockSpec((seq, LANES), per_b),
                  pl.BlockSpec((tq, LANES), cur),
                  pl.BlockSpec((LANES, kvw), fixed), pl.BlockSpec((kvr, LANES), fixed),
                  pl.BlockSpec((LANES, LANES), fixed)],
        out_specs=pl.BlockSpec((tq, DS_WIDTH), cur),
        out_shape=jax.ShapeDtypeStruct((m, DS_WIDTH), _MXU),
        scratch_shapes=[pltpu.VMEM((LANES, tq), jnp.int32), pltpu.VMEM((nkt, tk, tq), jnp.int32),
                        pltpu.VMEM((LANES, tq), _F32), pltpu.VMEM((nkt, tk, tq), _F32),
                        pltpu.VMEM((IDX_HEADS * tq, LANES), _MXU),
                        pltpu.VMEM((LANES, tq), _F32),
                        pltpu.VMEM((DS_KV, DS_REP * tq, DS_HD), _MXU),
                        pltpu.VMEM((DS_KV, 1, DS_REP * tq), _F32),
                        pltpu.VMEM((DS_KV, DS_HD + ONES_ROWS, DS_REP * tq), _F32)],
        compiler_params=_cparams(("parallel", "arbitrary")),
        name="dsa_frames",
    )(q, k, vt, qi, kid, wi, km_blk, vmt_blk, kidm_blk)


def _dsa_small_kernel(*refs, has_cache, nq, n_new, topk):
    if has_cache:
        q_ref, kn_ref, vn_ref, qi_ref, kidn_ref, wi_ref, ck_ref, cv_ref, ckid_ref, o_ref = refs
    else:
        q_ref, kn_ref, vn_ref, qi_ref, kidn_ref, wi_ref, o_ref = refs
    qst = _stack_idx_heads(qi_ref[...])
    wi = wi_ref[...]
    wbs = [jnp.broadcast_to(wi[:, h:h + 1], (nq, LANES)) for h in range(IDX_HEADS)]

    def wb_tile(width):
        return lambda h: jnp.concatenate([wbs[h]] * (width // LANES), axis=1)

    col_n = lax.broadcasted_iota(jnp.int32, (nq, LANES), 1)
    keys = [_order_key(_index_scores(qst, wb_tile(LANES), kidn_ref[...], nq), col_n < n_new)]
    if has_cache:
        past = ckid_ref.shape[1]
        sc = _index_scores(qst, wb_tile(past), ckid_ref[0], nq)
        keys.append(_order_key(sc, jnp.full(sc.shape, True)))

    def count_ge(thr):
        cnt = 0.0
        for kk in keys:
            cnt = cnt + jnp.sum(jnp.where(kk >= thr, 1.0, 0.0), axis=-1, keepdims=True)
        return cnt

    def bit_body(it, u):
        bit = jnp.int32(31) - it
        cand = u | lax.shift_left(jnp.int32(1), bit)
        cnt = count_ge(cand ^ jnp.int32(INT_MIN))
        return jnp.where(cnt >= float(topk), cand, u)

    u = lax.fori_loop(0, 32, bit_body, jnp.zeros((nq, 1), jnp.int32))
    thr = jnp.maximum(u ^ jnp.int32(INT_MIN), jnp.int32(INT_MIN + 1))

    need = float(topk) - count_ge(thr + 1)
    r = lax.broadcasted_iota(jnp.int32, (LANES, LANES), 0)
    c = lax.broadcasted_iota(jnp.int32, (LANES, LANES), 1)
    before = jnp.where(r <= c, 1.0, 0.0).astype(_MXU)
    seen = jnp.zeros((nq, 1), _F32)
    biases = [None] * len(keys)
    for idx in reversed(range(len(keys))):
        kk = keys[idx]
        cols = []
        for c0 in range(0, kk.shape[1], LANES):
            blk = kk[:, c0:c0 + LANES]
            eq = blk == thr
            upto = _dot(jnp.where(eq, 1.0, 0.0).astype(_MXU), before)
            keep = jnp.logical_or(blk > thr, jnp.logical_and(eq, seen + upto <= need))
            cols.append(jnp.where(keep, 0.0, NEG_INF))
            seen = seen + upto[:, LANES - 1:LANES]
        biases[idx] = jnp.concatenate(cols, axis=1)

    for g in range(DS_KV):
        gl = slice(g * DS_HD, (g + 1) * DS_HD)
        qg = jnp.concatenate(
            [q_ref[:, (g * DS_REP + r) * DS_HD:(g * DS_REP + r + 1) * DS_HD] for r in range(DS_REP)], axis=0)
        ks, vs = [kn_ref[:, gl]], [vn_ref[:, gl]]
        if has_cache:
            ks.append(ck_ref[0][:, gl].astype(_MXU))
            vs.append(cv_ref[0][:, gl].astype(_MXU))
        ss = [_dot_nt(qg, kk) + jnp.concatenate([bb] * DS_REP, axis=0) for kk, bb in zip(ks, biases)]
        m = ss[0].max(axis=-1, keepdims=True)
        for s in ss[1:]:
            m = jnp.maximum(m, s.max(axis=-1, keepdims=True))
        l = 0.0
        acc = 0.0
        for s, vv in zip(ss, vs):
            p = jnp.exp2(s - m)
            l = l + jnp.sum(p, axis=-1, keepdims=True)
            acc = acc + _dot(p.astype(_MXU), vv)
        og = acc / l
        for r in range(DS_REP):
            lo = (g * DS_REP + r) * DS_HD
            o_ref[:, lo:lo + DS_HD] = og[r * nq:(r + 1) * nq].astype(o_ref.dtype)


def _dsa_small(q, kn_blk, vn_blk, qi, kidn_blk, wi, cache_k, cache_v, cache_kid, nb, nq, n_new, topk):
    kvw = DS_KV * DS_HD
    has_cache = cache_k is not None
    row = lambda b: (b, 0)
    in_specs = [pl.BlockSpec((nq, DS_WIDTH), row),
                pl.BlockSpec((LANES, kvw), row), pl.BlockSpec((LANES, kvw), row),
                pl.BlockSpec((nq, IDX_HEADS * IDX_HD), row),
                pl.BlockSpec((LANES, LANES), row),
                pl.BlockSpec((nq, LANES), row)]
    args = [q, kn_blk, vn_blk, qi, kidn_blk, wi]
    if has_cache:
        past = cache_k.shape[1]
        in_specs += [pl.BlockSpec((1, past, kvw), lambda b: (b, 0, 0)),
                     pl.BlockSpec((1, past, kvw), lambda b: (b, 0, 0)),
                     pl.BlockSpec((1, past, LANES), lambda b: (b, 0, 0))]
        args += [cache_k, cache_v, cache_kid]
    return pl.pallas_call(
        functools.partial(_dsa_small_kernel, has_cache=has_cache, nq=nq, n_new=n_new, topk=topk),
        grid=(nb,),
        in_specs=in_specs,
        out_specs=pl.BlockSpec((nq, DS_WIDTH), row),
        out_shape=jax.ShapeDtypeStruct((nb * nq, DS_WIDTH), _MXU),
        compiler_params=_cparams(("parallel",)),
        name="dsa_small",
    )(*args)


def _pad_rows_front(a, rows):
    return jnp.pad(a, ((rows - a.shape[0], 0), (0, 0)))


def _pad_rows_back(a, nb, nq, rows):
    w = a.shape[1]
    return jnp.pad(a.reshape(nb, nq, w), ((0, 0), (0, rows - nq), (0, 0))).reshape(nb * rows, w)


def _transpose_with_ones(v, group_cols):
    n = v.shape[0]
    g = v.shape[1] // group_cols
    vt = v.T.reshape(g, group_cols, n)
    pad = jnp.zeros((g, ONES_ROWS, n), v.dtype).at[:, 0, :].set(1)
    return jnp.concatenate([vt, pad], axis=1).reshape(g * (group_cols + ONES_ROWS), n)


def _dup_pairs(a, hd):
    lead = a.shape[:-1]
    g = a.shape[-1] // hd
    a = a.reshape(*lead, g, 1, hd)
    return jnp.broadcast_to(a, (*lead, g, 2, hd)).reshape(*lead, g * 2 * hd)


def kernel(x_prompt, x_sample, cache_l0_k, cache_l0_v, cache_l1_k, cache_l1_v, cache_l2_k, cache_l2_v, cache_l2_kidx, cache_l3_k, cache_l3_v, meta_tokens, l0_norm, l0_w_in, l0_w_out, l0_lam_q1, l0_lam_k1, l0_lam_q2, l0_lam_k2, l0_subln, l1_norm, l1_w_in, l1_w_out, l1_sinks, l2_norm, l2_w_in, l2_w_out, l3_norm, l3_w_in, l3_w_out, l3_lam_q1, l3_lam_k1, l3_lam_q2, l3_lam_k2, l3_subln, final_norm):
    nb, seq, d = x_prompt.shape
    db, ds, _ = x_sample.shape
    past = cache_l0_k.shape[1]
    sw_rows = cache_l1_k.shape[1]
    assert seq % CHUNK == 0 and meta_tokens.shape[0] == N_META and ds <= LANES

    xs = {"f": x_prompt.reshape(nb * seq, d), "m": meta_tokens.astype(_F32), "s": x_sample.reshape(db * ds, d)}
    pos = {"f": N_META + jnp.arange(seq), "m": jnp.arange(N_META), "s": past + jnp.arange(ds)}
    pos["s"] = jnp.tile(pos["s"], db)

    tabs128 = {n: _rope_tables(p, 128, 1.0) for n, p in pos.items()}
    tabs128_da = {n: _rope_tables(p, 128, DA_HD ** -0.5 * LOG2E) for n, p in pos.items()}
    tabs128_ds = {n: _rope_tables(p, 128, DS_HD ** -0.5 * LOG2E) for n, p in pos.items()}
    tabs64 = {n: _rope_tables(p, 64, 1.0) for n, p in pos.items()}
    tabs64_sw = {n: _rope_tables(p, 64, SW_HD ** -0.5 * LOG2E) for n, p in pos.items()}

    layers = [
        dict(norm=l0_norm, w_in=l0_w_in, w_out=l0_w_out, lam=(l0_lam_q1, l0_lam_k1, l0_lam_q2, l0_lam_k2),
             subln=l0_subln, cache=(cache_l0_k, cache_l0_v)),
        dict(norm=l1_norm, w_in=l1_w_in, w_out=l1_w_out, sinks=l1_sinks, cache=(cache_l1_k, cache_l1_v)),
        dict(norm=l2_norm, w_in=l2_w_in, w_out=l2_w_out, cache=(cache_l2_k, cache_l2_v, cache_l2_kidx)),
        dict(norm=l3_norm, w_in=l3_w_in, w_out=l3_w_out, lam=(l3_lam_q1, l3_lam_k1, l3_lam_q2, l3_lam_k2),
             subln=l3_subln, cache=(cache_l3_k, cache_l3_v)),
    ]
    norms = [lp["norm"] for lp in layers[1:]] + [final_norm]
    depth = len(layers)

    hs = {n: _rms_norm_rows(x, layers[0]["norm"]) for n, x in xs.items()}
    p_st, s_st = [], []
    ys = {}
    for li, lp in enumerate(layers):
        kind = li % N_MIXERS
        w_in = lp["w_in"].astype(_MXU)
        w_out = lp["w_out"].astype(_MXU)
        o, z = {}, {}
        if kind == 0:
            lam_init = 0.8 - 0.6 * math.exp(-0.3 * li)
            wq, wk, wv, wz = (w_in[:, i * DA_WIDTH:(i + 1) * DA_WIDTH] for i in range(4))
            lams = jnp.stack([l.astype(_F32) for l in lp["lam"]])
            subln = lp["subln"].reshape(1, 2 * DA_HD).astype(_F32)
            subln_col = jnp.broadcast_to(lp["subln"].astype(_F32)[:, None], (2 * DA_HD, LANES))
            hshape = (DA_HEADS, 2 * DA_HD)
            q, k32, k16, v32, v16 = {}, {}, {}, {}, {}
            for n in xs:
                (q[n],) = _project(hs[n], wq, [_MXU], *tabs128_da[n])
                k32[n], k16[n] = _project(hs[n], wk, [_F32, _MXU], *tabs128[n])
                if n == "f":
                    v32[n], vt_f = _project(hs[n], wv, [_F32], group_cols=2 * DA_HD)
                else:
                    v32[n], v16[n] = _project(hs[n], wv, [_F32, _MXU])
                (z[n],) = _project(hs[n], wz, [_MXU])
            o["f"] = _diff_frames(q["f"], k16["f"], vt_f, k16["m"], _transpose_with_ones(v16["m"], 2 * DA_HD),
                                  lams, subln_col, lam_init, nb, seq)
            o["m"] = _diff_small(q["m"], k16["m"], v16["m"], None, None, lams, subln, lam_init, 1, N_META)
            ck, cv = lp["cache"]
            o["s"] = _diff_small(q["s"], k16["s"], v16["s"], ck.reshape(db, past, DA_WIDTH),
                                 cv.reshape(db, past, DA_WIDTH), lams, subln, lam_init, db, ds)

            def with_meta(f, mrows):
                mm = jnp.broadcast_to(mrows.reshape(1, N_META, *hshape), (nb, N_META, *hshape))
                return jnp.concatenate([mm, f.reshape(nb, seq, *hshape)], axis=1)

            p_st.append([with_meta(k32["f"], k32["m"]), with_meta(v32["f"], v32["m"])])
            s_st.append([k32["s"].reshape(db, ds, *hshape), v32["s"].reshape(db, ds, *hshape)])
        elif kind == 1:
            kvd = SW_KV * SW_HD
            wq = w_in[:, :SW_WIDTH]
            wk = w_in[:, SW_WIDTH:SW_WIDTH + kvd]
            wv = w_in[:, SW_WIDTH + kvd:SW_WIDTH + 2 * kvd]
            wz = w_in[:, SW_WIDTH + 2 * kvd:]
            wkd, wvd = _dup_pairs(wk, SW_HD), _dup_pairs(wv, SW_HD)
            sink_rows = {}
            q, kv32, kd, vd = {}, {}, {}, {}
            sk = (lp["sinks"].astype(_F32) * LOG2E).reshape(SW_KV, SW_REP, 1, 1)
            sink_cols = jnp.broadcast_to(sk.reshape(SW_KV, 1, SW_REP, 1),
                                         (SW_KV, 1, SW_REP, 2 * CHUNK)).reshape(SW_KV, 1, SW_REP * 2 * CHUNK)
            for n, rows in (("m", N_META), ("s", ds)):
                sink_rows[n] = jnp.broadcast_to(sk, (SW_KV, SW_REP, rows, LANES)).reshape(SW_KV, SW_REP * rows, LANES)
            for n in xs:
                (q[n],) = _project(hs[n], wq, [_MXU], *tabs64_sw[n])
                (kd[n],) = _project(hs[n], wkd, [_MXU], *tabs64[n])
                (vd[n],) = _project(hs[n], wvd, [_MXU])
                (z[n],) = _project(hs[n], wz, [_MXU])
            kmeta_blk = _pad_rows_front(kd["m"], WINDOW)
            vmeta_blk = _pad_rows_front(vd["m"], WINDOW)
            o["f"] = _swa_frames(q["f"], kd["f"], vd["f"], kmeta_blk, vmeta_blk, sink_cols, nb, seq)
            o["m"] = _swa_small(q["m"], kd["m"], vd["m"], None, None, sink_rows["m"], 1, N_META)
            ck, cv = lp["cache"]
            ckd = _dup_pairs(ck.reshape(db, sw_rows, kvd), SW_HD).astype(_MXU)
            cvd = _dup_pairs(cv.reshape(db, sw_rows, kvd), SW_HD).astype(_MXU)
            o["s"] = _swa_small(q["s"], kd["s"], vd["s"], ckd, cvd, sink_rows["s"], db, ds)
            assert seq >= sw_rows and sw_rows % 8 == 0
            h_tail = hs["f"].reshape(nb, seq, d)[:, seq - sw_rows:].reshape(nb * sw_rows, d)
            tail_pos = N_META + seq - sw_rows + jnp.arange(sw_rows)
            tail_tabs, tail_shift = _rope_tables(tail_pos, 64, 1.0)
            (kt,) = _project(h_tail, _pad_cols(wk), [_F32], tail_tabs, tail_shift)
            (vt,) = _project(h_tail, _pad_cols(wv), [_F32])
            p_st.append([kt[:, :kvd].reshape(nb, sw_rows, SW_KV, SW_HD), vt[:, :kvd].reshape(nb, sw_rows, SW_KV, SW_HD)])
            (ks,) = _project(hs["s"], _pad_cols(wk), [_F32], *tabs64["s"])
            (vs_,) = _project(hs["s"], _pad_cols(wv), [_F32])
            ks = ks[:, :kvd].reshape(db, ds, SW_KV, SW_HD)
            vs_ = vs_[:, :kvd].reshape(db, ds, SW_KV, SW_HD)
            k_all = jnp.concatenate([ck, ks], axis=1)
            v_all = jnp.concatenate([cv, vs_], axis=1)
            s_st.append([k_all[:, k_all.shape[1] - sw_rows:], v_all[:, v_all.shape[1] - sw_rows:]])
        else:
            kvd = DS_KV * DS_HD
            c0 = 0
            wq = w_in[:, c0:c0 + DS_WIDTH]; c0 += DS_WIDTH
            wk = w_in[:, c0:c0 + kvd]; c0 += kvd
            wv = w_in[:, c0:c0 + kvd]; c0 += kvd
            wz = w_in[:, c0:c0 + DS_WIDTH]; c0 += DS_WIDTH
            wqi = w_in[:, c0:c0 + IDX_HEADS * IDX_HD]; c0 += IDX_HEADS * IDX_HD
            wki = w_in[:, c0:c0 + IDX_HD]; c0 += IDX_HD
            wwi = w_in[:, c0:c0 + IDX_HEADS]
            wkid = jnp.concatenate([wki, wki], axis=1)
            wwi_p = _pad_cols(wwi)
            wi_scale = IDX_HEADS ** -0.5 * IDX_HD ** -0.5
            q, k32, k16, v32, v16, qi, kid32, kid16, wi = {}, {}, {}, {}, {}, {}, {}, {}, {}
            for n in xs:
                (q[n],) = _project(hs[n], wq, [_MXU], *tabs128_ds[n])
                k32[n], k16[n] = _project(hs[n], wk, [_F32, _MXU], *tabs128[n])
                if n == "f":
                    v32[n], vt_f = _project(hs[n], wv, [_F32], group_cols=DS_HD)
                else:
                    v32[n], v16[n] = _project(hs[n], wv, [_F32, _MXU])
                (z[n],) = _project(hs[n], wz, [_MXU])
                (qi[n],) = _project(hs[n], wqi, [_MXU], *tabs64[n])
                kid32[n], kid16[n] = _project(hs[n], wkid, [_F32, _MXU], *tabs64[n])
                (wi[n],) = _project(hs[n], wwi_p, [_F32], scale=wi_scale)
            topk_p = min(TOPK_MAX, seq // 4)
            topk_s = min(TOPK_MAX, (past + ds) // 4)
            km_blk = _pad_rows_back(k16["m"], 1, N_META, LANES)
            vm_blk = _pad_rows_back(v16["m"], 1, N_META, LANES)
            kidm_blk = _pad_rows_back(kid16["m"], 1, N_META, LANES)
            o["f"] = _dsa_frames(q["f"], k16["f"], vt_f, qi["f"], kid16["f"], wi["f"],
                                 km_blk, _transpose_with_ones(vm_blk, DS_HD), kidm_blk, nb, seq, topk_p)
            o["m"] = _dsa_small(q["m"], km_blk, vm_blk, qi["m"], kidm_blk, wi["m"], None, None, None,
                                1, N_META, N_META, topk_p)
            ck, cv, cki = lp["cache"]
            ckid = jnp.concatenate([cki, cki], axis=-1).astype(_MXU)
            o["s"] = _dsa_small(q["s"], _pad_rows_back(k16["s"], db, ds, LANES), _pad_rows_back(v16["s"], db, ds, LANES),
                                qi["s"], _pad_rows_back(kid16["s"], db, ds, LANES), wi["s"],
                                ck.reshape(db, past, kvd), cv.reshape(db, past, kvd), ckid,
                                db, ds, ds, topk_s)

            def with_meta(f, mrows, *tail):
                mm = jnp.broadcast_to(mrows.reshape(1, N_META, *tail), (nb, N_META, *tail))
                return jnp.concatenate([mm, f.reshape(nb, seq, *tail)], axis=1)

            p_st.append([with_meta(k32["f"], k32["m"], DS_KV, DS_HD), with_meta(v32["f"], v32["m"], DS_KV, DS_HD),
                         with_meta(kid32["f"][:, :IDX_HD], kid32["m"][:, :IDX_HD], IDX_HD)])
            s_st.append([k32["s"].reshape(db, ds, DS_KV, DS_HD), v32["s"].reshape(db, ds, DS_KV, DS_HD),
                         kid32["s"][:, :IDX_HD].reshape(db, ds, IDX_HD)])
        last = li == depth - 1
        for n in xs:
            res = _out_project(o[n], z[n], w_out, xs[n], norms[li], last)
            if last:
                (ys[n],) = res
            else:
                xs[n], hs[n] = res

    y_prompt = ys["f"].reshape(nb, seq, d)
    y_sample = ys["s"].reshape(db, ds, d)
    return (y_prompt, y_sample,
            p_st[0][0], p_st[0][1], s_st[0][0], s_st[0][1],
            p_st[1][0], p_st[1][1], s_st[1][0], s_st[1][1],
            p_st[2][0], p_st[2][1], p_st[2][2], s_st[2][0], s_st[2][1], s_st[2][2],
            p_st[3][0], p_st[3][1], s_st[3][0], s_st[3][1])


def _pad_cols(w):
    n = w.shape[1]
    return jnp.pad(w, ((0, 0), (0, (-n) % LANES)))
```
